```python
import math
import jax, jax.numpy as jnp
from jax import lax
import numpy as np

D_MODEL = 1024
BATCH = 4
SEQ = 4096
DEPTH = 4
DEC_BATCH = 128
DEC_SEQ = 1
PAST_LEN = 2048
PAGE_SIZE = 128

N_A_LAYERS = DEPTH // 2
N_B_LAYERS = DEPTH - N_A_LAYERS
LRU_WIDTH = D_MODEL
LRU_HEADS = 8
LRU_BLOCK = LRU_WIDTH // LRU_HEADS
CONV_WIDTH = 4
RGLRU_C = 8.0
N_HEADS = 16
HEAD_DIM = D_MODEL // N_HEADS
N_KV = 4
GROUP = N_HEADS // N_KV
CMP_BLOCK = 32
CMP_STRIDE = 16
CMP_RATIO = CMP_BLOCK // CMP_STRIDE
CMP_HIDDEN = 2 * HEAD_DIM
SEL_BLOCK = 64
N_SELECT = 16
WINDOW = 512
NSA_Q_BLOCK = 32
N_KV_PARTS = 6
REL_BUCKETS = 32
REL_MAX_DIST = 128
N_EXPERTS = 32
TOP_K = 4
D_EXPERT = D_MODEL
SWIGLU_LIMIT = 7.0
SWIGLU_ALPHA = 1.702
MOE_ROWS = 256
DEEPNORM_ALPHA = (2 * DEPTH) ** 0.25
DEEPNORM_BETA = (8 * DEPTH) ** -0.25
LN_EPS = 1e-5
SEL_BIG = 1e9
NEG_INF = -1e30

kernel_name = 'hybrid_rglru_nsa_moe_decoder_step'


def layer_norm(x, g, b):
    xf = x.astype(jnp.float32)
    mu = xf.mean(-1, keepdims=True)
    var = jnp.square(xf - mu).mean(-1, keepdims=True)
    return ((xf - mu) * lax.rsqrt(var + LN_EPS) * g + b).astype(x.dtype)


def masked_softmax(s, mask):
    s = jnp.where(mask, s, NEG_INF)
    e = jnp.where(mask, jnp.exp(s - s.max(-1, keepdims=True)), 0.0)
    z = e.sum(-1, keepdims=True)
    return e / jnp.where(z > 0, z, 1.0)


def rel_bucket(dist):
    d = jnp.maximum(dist, 0)
    exact = REL_BUCKETS // 2
    df = jnp.maximum(d, 1).astype(jnp.float32)
    large = exact + (jnp.log(df / exact) / math.log(REL_MAX_DIST / exact) * (REL_BUCKETS - exact)).astype(jnp.int32)
    return jnp.where(d < exact, d, jnp.minimum(large, REL_BUCKETS - 1))


def rglru_mixer(h, conv_buf, h0, w_in, w_conv, b_conv, w_gate, b_gate, lam, w_out):
    B, T, _ = h.shape
    u = h @ w_in
    y_branch = jax.nn.gelu(u[..., :LRU_WIDTH])
    xb = u[..., LRU_WIDTH:]
    xpad = jnp.concatenate([conv_buf.astype(xb.dtype), xb], axis=1)
    xc = b_conv + sum(xpad[:, k:k + T] * w_conv[k] for k in range(CONV_WIDTH))
    gates = jnp.einsum('bthi,ghij->gbthj', xc.reshape(B, T, LRU_HEADS, LRU_BLOCK), w_gate).reshape(2, B, T, LRU_WIDTH) + b_gate[:, None, None]
    r = jax.nn.sigmoid(gates[0].astype(jnp.float32))
    i_g = jax.nn.sigmoid(gates[1].astype(jnp.float32))
    log_a = -RGLRU_C * r * jax.nn.softplus(-lam.astype(jnp.float32))
    a = jnp.exp(log_a)
    b_in = jnp.sqrt(-jnp.expm1(2.0 * log_a)) * (i_g * xc.astype(jnp.float32))

    def step(hc, ab):
        hn = ab[0] * hc + ab[1]
        return hn, hn

    h_last, hs = lax.scan(step, h0.astype(jnp.float32), (a.swapaxes(0, 1), b_in.swapaxes(0, 1)))
    y = (hs.swapaxes(0, 1).astype(h.dtype) * y_branch) @ w_out
    return y, xpad[:, T:], h_last.astype(h0.dtype)


def moe(h, w_r, b_r, w_gu, b_gu, w_dn, b_dn):
    B, T, D = h.shape
    x2 = h.reshape(-1, D)
    n_tok = x2.shape[0]
    logits = jnp.einsum('td,de->te', x2, w_r, preferred_element_type=jnp.float32) + b_r
    top_v, top_e = lax.top_k(logits, TOP_K)
    gate = jax.nn.softmax(top_v, axis=-1)
    n_as = n_tok * TOP_K
    blk = max(8, min(MOE_ROWS, n_as // N_EXPERTS))
    n_blocks = -(-n_as // blk) + N_EXPERTS
    flat_e = top_e.reshape(-1)
    order = jnp.argsort(flat_e)
    se = flat_e[order]
    counts = jnp.bincount(flat_e, length=N_EXPERTS)
    padded = (counts + blk - 1) // blk * blk
    pad_end = jnp.cumsum(padded)
    pad_start = pad_end - padded
    start = jnp.cumsum(counts) - counts
    dest = pad_start[se] + jnp.arange(n_as) - start[se]
    tok = order // TOP_K
    row_tok = jnp.full((n_blocks * blk,), n_tok, jnp.int32).at[dest].set(tok)
    xs = jnp.concatenate([x2, jnp.zeros((1, D), x2.dtype)])[row_tok].reshape(n_blocks, blk, D)
    block_e = jnp.minimum(jnp.searchsorted(pad_end, jnp.arange(n_blocks) * blk, side='right'), N_EXPERTS - 1)

    def run(args):
        xb, e = args
        gu = xb @ w_gu[e] + b_gu[e]
        g = jnp.minimum(gu[:, :D_EXPERT], SWIGLU_LIMIT)
        u = jnp.clip(gu[:, D_EXPERT:], -SWIGLU_LIMIT, SWIGLU_LIMIT)
        return (g * jax.nn.sigmoid(SWIGLU_ALPHA * g) * (u + 1)) @ w_dn[e] + b_dn[e]

    ys = lax.map(run, (xs, block_e)).reshape(-1, D)
    contrib = ys[dest] * gate.reshape(-1)[order][:, None].astype(ys.dtype)
    return jax.ops.segment_sum(contrib, tok, num_segments=n_tok).reshape(B, T, D)


def compress(k, w1, b1, w2, b2, pe):
    B, L = k.shape[:2]
    nc = (L - CMP_BLOCK) // CMP_STRIDE + 1
    nch = nc + CMP_RATIO - 1
    chunks = k[:, :nch * CMP_STRIDE].reshape(B, nch, CMP_STRIDE, N_KV, HEAD_DIM)
    part = jnp.einsum('bnpgd,updh->bungh', chunks, w1.reshape(CMP_RATIO, CMP_STRIDE, HEAD_DIM, CMP_HIDDEN))
    pre = sum(part[:, u:u + nc, u] for u in range(CMP_RATIO)) + jnp.einsum('pd,pdh->h', pe, w1) + b1
    return jnp.einsum('bngh,hd->bngd', jax.nn.gelu(pre), w2) + b2


def compressed_kv(k, v, p):
    kc = compress(k, p['cmp_w1'][0], p['cmp_b1'][0], p['cmp_w2'][0], p['cmp_b2'][0], p['cmp_pe'][0])
    vc = compress(v, p['cmp_w1'][1], p['cmp_b1'][1], p['cmp_w2'][1], p['cmp_b2'][1], p['cmp_pe'][1])
    cend = jnp.arange(kc.shape[1]) * CMP_STRIDE + CMP_BLOCK - 1
    return kc, vc, cend


def to_blocks(k):
    B, L = k.shape[:2]
    ns = -(-L // SEL_BLOCK)
    k = jnp.pad(k, ((0, 0), (0, ns * SEL_BLOCK - L), (0, 0), (0, 0)))
    return k.reshape(B, ns, SEL_BLOCK, N_KV, HEAD_DIM).transpose(0, 3, 1, 2, 4)


def block_overlap(nc, ns):
    i = jnp.arange(nc)[:, None]
    j = jnp.arange(ns)[None, :]
    return ((i * CMP_STRIDE < (j + 1) * SEL_BLOCK) & (i * CMP_STRIDE + CMP_BLOCK > j * SEL_BLOCK)).astype(jnp.float32)


def nsa_attend(q, gates, qpos, kc, vc, cend, ksb, vsb, kw, vw, wpos, rel_table):
    f32 = jnp.float32
    B, Q = q.shape[:2]
    scale = HEAD_DIM ** -0.5
    qg = q.reshape(B, Q, N_KV, GROUP, HEAD_DIM)
    table = rel_table.astype(f32).reshape(REL_BUCKETS, N_KV, GROUP)
    dist_c = qpos[:, None] - cend[None, :]
    s_c = jnp.einsum('bqgrd,bcgd->bqgrc', qg, kc, preferred_element_type=f32) * scale + table[rel_bucket(dist_c)].transpose(0, 2, 3, 1)
    p_c = masked_softmax(s_c, (dist_c >= 0)[:, None, None, :])
    o_c = jnp.einsum('bqgrc,bcgd->bqgrd', p_c.astype(vc.dtype), vc)
    ns = ksb.shape[2]
    imp = jnp.einsum('bqgc,cn->bqgn', p_c.sum(3), block_overlap(cend.shape[0], ns))
    blk = jnp.arange(ns)[None, :]
    cur = (qpos // SEL_BLOCK)[:, None]
    forced = (blk == 0) | ((blk >= cur - 1) & (blk <= cur))
    valid = blk * SEL_BLOCK <= qpos[:, None]
    score = jnp.where(forced[:, None], SEL_BIG, jnp.where(valid[:, None], imp, -SEL_BIG))
    _, sel = lax.top_k(score, min(N_SELECT, ns))
    n_len = sel.shape[-1] * SEL_BLOCK
    b_idx = jnp.arange(B)[:, None, None, None]
    g_idx = jnp.arange(N_KV)[None, None, :, None]
    k_s = ksb[b_idx, g_idx, sel].reshape(B, Q, N_KV, n_len, HEAD_DIM)
    v_s = vsb[b_idx, g_idx, sel].reshape(B, Q, N_KV, n_len, HEAD_DIM)
    kpos = (sel[..., None] * SEL_BLOCK + jnp.arange(SEL_BLOCK)).reshape(B, Q, N_KV, n_len)
    dist_s = qpos[None, :, None, None] - kpos
    bias_s = jnp.moveaxis(table[rel_bucket(dist_s), g_idx], -1, 3)
    s_s = jnp.einsum('bqgrd,bqgld->bqgrl', qg, k_s, preferred_element_type=f32) * scale + bias_s
    p_s = masked_softmax(s_s, (dist_s >= 0)[:, :, :, None])
    o_s = jnp.einsum('bqgrl,bqgld->bqgrd', p_s.astype(v_s.dtype), v_s)
    dist_w = qpos[:, None] - wpos[None, :]
    mask_w = (dist_w >= 0) & (dist_w <= WINDOW) & (wpos >= 0)[None, :]
    s_w = jnp.einsum('bqgrd,bkgd->bqgrk', qg, kw, preferred_element_type=f32) * scale + table[rel_bucket(dist_w)].transpose(0, 2, 3, 1)
    p_w = masked_softmax(s_w, mask_w[:, None, None, :])
    o_w = jnp.einsum('bqgrk,bkgd->bqgrd', p_w.astype(vw.dtype), vw)
    g = gates.reshape(B, Q, N_KV, GROUP, 3).astype(o_c.dtype)
    o = g[..., 0:1] * o_c + g[..., 1:2] * o_s + g[..., 2:3] * o_w
    return o.reshape(B, Q, N_HEADS * HEAD_DIM)


def make_prompt_attend(kv, p):
    B, S = kv.shape[:2]
    kc, vc, cend = compressed_kv(kv[:, :, 0], kv[:, :, 1], p)
    ksb, vsb = to_blocks(kv[:, :, 2]), to_blocks(kv[:, :, 3])
    pad = ((0, 0), (WINDOW, 0), (0, 0), (0, 0))
    kw, vw = jnp.pad(kv[:, :, 4], pad), jnp.pad(kv[:, :, 5], pad)
    qb_len = min(NSA_Q_BLOCK, S)
    nqb = S // qb_len
    span = WINDOW + qb_len

    def attend(q, gates):
        qb = q.reshape(B, nqb, qb_len, N_HEADS, HEAD_DIM).swapaxes(0, 1)
        gb = gates.reshape(B, nqb, qb_len, N_HEADS, 3).swapaxes(0, 1)

        def one_block(args):
            qi, gi, q0 = args
            return nsa_attend(qi, gi, q0 + jnp.arange(qb_len), kc, vc, cend, ksb, vsb,
                              lax.dynamic_slice_in_dim(kw, q0, span, axis=1),
                              lax.dynamic_slice_in_dim(vw, q0, span, axis=1),
                              q0 - WINDOW + jnp.arange(span), p['rel_table'])

        o = lax.map(one_block, (qb, gb, jnp.arange(nqb) * qb_len))
        return o.swapaxes(0, 1).reshape(B, S, N_HEADS * HEAD_DIM)

    n_win = min(WINDOW, S)
    return attend, (kv[:, :, :4], kv[:, S - n_win:, 4:])


def make_sample_attend(kv, cache_kv, cache_win, page_table, p):
    DB, DS = kv.shape[:2]
    past_len = page_table.shape[1] * cache_kv.shape[1]

    def full(u):
        past = cache_kv[page_table, :, u].reshape(DB, past_len, N_KV, HEAD_DIM)
        return jnp.concatenate([past.astype(kv.dtype), kv[:, :, u]], axis=1)

    kc, vc, cend = compressed_kv(full(0), full(1), p)
    ksb, vsb = to_blocks(full(2)), to_blocks(full(3))
    n_buf = cache_win.shape[1]
    win = jnp.concatenate([cache_win.astype(kv.dtype), kv[:, :, 4:]], axis=1)
    wpos = past_len - n_buf + jnp.arange(n_buf + DS)
    qpos = past_len + jnp.arange(DS)

    def attend(q, gates):
        return nsa_attend(q, gates, qpos, kc, vc, cend, ksb, vsb, win[:, :, 0], win[:, :, 1], wpos, p['rel_table'])

    return attend, (kv[:, :, :4], win[:, -n_buf:])


def shared_kv(x, c, w_ada_kv, b_ada_kv, w_kv):
    sh, sc = jnp.split((jax.nn.silu(c) @ w_ada_kv + b_ada_kv)[:, None], 2, axis=-1)
    B, T, _ = x.shape
    return ((x * (1 + sc) + sh) @ w_kv).reshape(B, T, N_KV_PARTS, N_KV, HEAD_DIM)


def nsa_mixer(h, attend, w_in, w_out):
    B, T, _ = h.shape
    hd = N_HEADS * HEAD_DIM
    proj = h @ w_in
    q = proj[..., :hd].reshape(B, T, N_HEADS, HEAD_DIM)
    gates = jax.nn.sigmoid(proj[..., hd:]).reshape(B, T, N_HEADS, 3)
    return attend(q, gates) @ w_out


def trunk(x, c, conv0, h0, make_attend, p):
    conv_new, h_new = [], []
    attend, kv_state = None, None
    for l in range(DEPTH):
        mod = jax.nn.silu(c) @ p['w_ada'][l] + p['b_ada'][l]
        sh1, sc1, g1, sh2, sc2, g2 = jnp.split(mod[:, None], 6, axis=-1)
        hin = x * (1 + sc1) + sh1
        if l < N_A_LAYERS:
            out, cb, hl = rglru_mixer(hin, conv0[l], h0[l], p['w_in_a'][l], p['w_conv'][l], p['b_conv'][l],
                                      p['w_gate_a'][l], p['b_gate_a'][l], p['lru_lambda'][l], p['w_out_a'][l])
            conv_new.append(cb)
            h_new.append(hl)
        else:
            if l == N_A_LAYERS:
                attend, kv_state = make_attend(shared_kv(x, c, p['w_ada_kv'], p['b_ada_kv'], p['w_kv']))
            lb = l - N_A_LAYERS
            out = nsa_mixer(hin, attend, p['w_in_b'][lb], p['w_out_b'][lb])
        x = layer_norm(DEEPNORM_ALPHA * x + (1 + g1) * out, p['ln_g'][l, 0], p['ln_b'][l, 0])
        hin = x * (1 + sc2) + sh2
        ff = moe(hin, p['w_router'][l], p['b_router'][l], p['w_gu'][l], p['b_gu'][l], p['w_down'][l], p['b_down'][l])
        x = layer_norm(DEEPNORM_ALPHA * x + (1 + g2) * ff, p['ln_g'][l, 1], p['ln_b'][l, 1])
    return x, jnp.stack(conv_new), jnp.stack(h_new), kv_state


def setup_inputs(seed: int = 0) -> dict:
    key = jax.random.key(seed)
    ks = iter(jax.random.split(key, 48))

    def nrm(shape, s):
        return jax.random.normal(next(ks), shape, jnp.float32) * s

    D = D_MODEL
    n_pages = PAST_LEN // PAGE_SIZE
    n_pool = (DEC_BATCH * n_pages * 5 + 3) // 4
    w_buf = min(WINDOW, PAST_LEN)
    u = jax.random.uniform(next(ks), (N_A_LAYERS, LRU_WIDTH), jnp.float32, 0.9, 0.999)
    a_base = u ** (1.0 / RGLRU_C)
    lru_lambda = jnp.log(a_base) - jnp.log1p(-a_base)
    page_table = jax.random.permutation(next(ks), n_pool)[:DEC_BATCH * n_pages].reshape(DEC_BATCH, n_pages).astype(jnp.int32)
    hd = N_HEADS * HEAD_DIM
    return {
        'x_prompt': nrm((BATCH, SEQ, D), 1.0),
        'x_sample': nrm((DEC_BATCH, DEC_SEQ, D), 1.0),
        'c_prompt': nrm((BATCH, D), 1.0),
        'c_sample': nrm((DEC_BATCH, D), 1.0),
        'state_conv': nrm((N_A_LAYERS, DEC_BATCH, CONV_WIDTH - 1, LRU_WIDTH), 1.0),
        'state_h': nrm((N_A_LAYERS, DEC_BATCH, LRU_WIDTH), 0.5),
        'cache_kv': nrm((n_pool, PAGE_SIZE, 4, N_KV, HEAD_DIM), 1.0),
        'cache_win': nrm((DEC_BATCH, w_buf, 2, N_KV, HEAD_DIM), 1.0),
        'page_table': page_table,
        'w_ada': nrm((DEPTH, D, 6 * D), 0.2 * D ** -0.5),
        'b_ada': nrm((DEPTH, 6 * D), 0.01),
        'ln_g': 1.0 + nrm((DEPTH, 2, D), 0.05),
        'ln_b': nrm((DEPTH, 2, D), 0.01),
        'w_in_a': nrm((N_A_LAYERS, D, 2 * LRU_WIDTH), D ** -0.5),
        'w_conv': nrm((N_A_LAYERS, CONV_WIDTH, LRU_WIDTH), CONV_WIDTH ** -0.5),
        'b_conv': nrm((N_A_LAYERS, LRU_WIDTH), 0.01),
        'w_gate_a': nrm((N_A_LAYERS, 2, LRU_HEADS, LRU_BLOCK, LRU_BLOCK), LRU_BLOCK ** -0.5),
        'b_gate_a': nrm((N_A_LAYERS, 2, LRU_WIDTH), 0.01),
        'lru_lambda': lru_lambda,
        'w_out_a': nrm((N_A_LAYERS, LRU_WIDTH, D), LRU_WIDTH ** -0.5 * DEEPNORM_BETA),
        'w_ada_kv': nrm((D, 2 * D), 0.2 * D ** -0.5),
        'b_ada_kv': nrm((2 * D,), 0.01),
        'w_kv': nrm((D, N_KV_PARTS * N_KV * HEAD_DIM), D ** -0.5),
        'cmp_w1': nrm((2, CMP_BLOCK, HEAD_DIM, CMP_HIDDEN), (CMP_BLOCK * HEAD_DIM) ** -0.5),
        'cmp_b1': nrm((2, CMP_HIDDEN), 0.01),
        'cmp_w2': nrm((2, CMP_HIDDEN, HEAD_DIM), CMP_HIDDEN ** -0.5),
        'cmp_b2': nrm((2, HEAD_DIM), 0.01),
        'cmp_pe': nrm((2, CMP_BLOCK, HEAD_DIM), 0.1),
        'w_in_b': nrm((N_B_LAYERS, D, hd + 3 * N_HEADS), D ** -0.5),
        'w_out_b': nrm((N_B_LAYERS, hd, D), hd ** -0.5 * DEEPNORM_BETA),
        'rel_table': nrm((REL_BUCKETS, N_HEADS), 0.5),
        'w_router': nrm((DEPTH, D, N_EXPERTS), D ** -0.5),
        'b_router': nrm((DEPTH, N_EXPERTS), 0.01),
        'w_gu': nrm((DEPTH, N_EXPERTS, D, 2 * D_EXPERT), D ** -0.5),
        'b_gu': nrm((DEPTH, N_EXPERTS, 2 * D_EXPERT), 0.01),
        'w_down': nrm((DEPTH, N_EXPERTS, D_EXPERT, D), D_EXPERT ** -0.5 * DEEPNORM_BETA),
        'b_down': nrm((DEPTH, N_EXPERTS, D), 0.01),
    }


def reference(x_prompt, x_sample, c_prompt, c_sample, state_conv, state_h, cache_kv, cache_win, page_table,
              w_ada, b_ada, ln_g, ln_b, w_in_a, w_conv, b_conv, w_gate_a, b_gate_a, lru_lambda, w_out_a,
              w_ada_kv, b_ada_kv, w_kv, cmp_w1, cmp_b1, cmp_w2, cmp_b2, cmp_pe, w_in_b, w_out_b, rel_table,
              w_router, b_router, w_gu, b_gu, w_down, b_down):
    p = dict(w_ada=w_ada, b_ada=b_ada, ln_g=ln_g, ln_b=ln_b, w_in_a=w_in_a, w_conv=w_conv, b_conv=b_conv,
             w_gate_a=w_gate_a, b_gate_a=b_gate_a, lru_lambda=lru_lambda, w_out_a=w_out_a,
             w_ada_kv=w_ada_kv, b_ada_kv=b_ada_kv, w_kv=w_kv, cmp_w1=cmp_w1, cmp_b1=cmp_b1, cmp_w2=cmp_w2,
             cmp_b2=cmp_b2, cmp_pe=cmp_pe, w_in_b=w_in_b, w_out_b=w_out_b, rel_table=rel_table,
             w_router=w_router, b_router=b_router, w_gu=w_gu, b_gu=b_gu, w_down=w_down, b_down=b_down)
    B = x_prompt.shape[0]
    conv0 = jnp.zeros((N_A_LAYERS, B, CONV_WIDTH - 1, LRU_WIDTH), x_prompt.dtype)
    h0 = jnp.zeros((N_A_LAYERS, B, LRU_WIDTH), x_prompt.dtype)
    y_prompt, prompt_conv, prompt_h, (prompt_kv, prompt_win) = trunk(
        x_prompt, c_prompt, conv0, h0, lambda kv: make_prompt_attend(kv, p), p)
    y_sample, sample_conv, sample_h, (sample_kv, sample_win) = trunk(
        x_sample, c_sample, state_conv, state_h,
        lambda kv: make_sample_attend(kv, cache_kv, cache_win, page_table, p), p)
    return (y_prompt, y_sample, prompt_conv, prompt_h, prompt_kv, prompt_win, sample_conv, sample_h, sample_kv, sample_win)
```

```python
import functools
import math

import numpy as np
import jax
import jax.numpy as jnp
from jax import lax
from jax.experimental import pallas as pl
from jax.experimental.pallas import tpu as pltpu

F32 = jnp.float32
BF16 = jnp.bfloat16
HIGHEST = lax.Precision.HIGHEST

CMP_STRIDE = 16
SEL_BLOCK = 64
N_SELECT = 16
WINDOW = 512
REL_MAX_DIST = 128
TOP_K = 4
RGLRU_C = 8.0
SWIGLU_LIMIT = 7.0
SWIGLU_ALPHA = 1.702
MOE_ROWS = 256
LN_EPS = 1e-5
SEL_BIG = 1e9
NEG_INF = -1e30
M_INIT = -1e29

LANES = 128
VMEM_LIMIT = 48 * 1024 * 1024


def _cp(sem, vmem=VMEM_LIMIT):
    return pltpu.CompilerParams(dimension_semantics=sem, vmem_limit_bytes=vmem)


def _rel_thresholds(n_buckets):
    exact = n_buckets // 2
    thr = []
    d_max = REL_MAX_DIST + 1
    buckets = []
    for d in range(d_max + 1):
        if d < exact:
            buckets.append(d)
        else:
            large = exact + int(math.log(max(d, 1) / exact) / math.log(REL_MAX_DIST / exact) * (n_buckets - exact))
            buckets.append(min(large, n_buckets - 1))
    for b in range(n_buckets):
        thr.append(next(d for d in range(d_max + 1) if buckets[d] >= b))
    return thr


def _linear_kernel(x_ref, w_ref, b_ref, o_ref, *, silu_in):
    x = x_ref[...]
    if silu_in:
        x = x * jax.nn.sigmoid(x)
    y = jnp.dot(x.astype(BF16), w_ref[0].astype(BF16), preferred_element_type=F32)
    o_ref[0] = y + b_ref[0]


def _linear(x, w, b, *, silu_in=False, tn=512):
    m, k = x.shape
    nl, _, n = w.shape
    tn = min(tn, n)
    return pl.pallas_call(
        functools.partial(_linear_kernel, silu_in=silu_in),
        grid=(nl, n // tn),
        in_specs=[
            pl.BlockSpec((m, k), lambda l, j: (0, 0)),
            pl.BlockSpec((1, k, tn), lambda l, j: (l, 0, j)),
            pl.BlockSpec((1, 1, tn), lambda l, j: (l, 0, j)),
        ],
        out_specs=pl.BlockSpec((1, m, tn), lambda l, j: (l, 0, j)),
        out_shape=jax.ShapeDtypeStruct((nl, m, n), F32),
        compiler_params=_cp(("parallel", "parallel")),
        name="linear",
    )(x, w, b)


def _mod_spec(mod, tt):
    d = mod.shape[-1]
    if mod.shape[1] == 1:
        return pl.BlockSpec((1, 1, d), lambda b, i: (b, 0, 0))
    return pl.BlockSpec((1, tt, d), lambda b, i: (b, i, 0))


def _modlinear_kernel(x_ref, sc_ref, sh_ref, w_ref, o_ref):
    hin = x_ref[0] * (1.0 + sc_ref[0]) + sh_ref[0]
    o_ref[0] = jnp.dot(hin.astype(BF16), w_ref[...], preferred_element_type=F32)


def _modlinear(x, sc, sh, w_bf, *, tt):
    bsz, t, d = x.shape
    n = w_bf.shape[1]
    return pl.pallas_call(
        _modlinear_kernel,
        grid=(bsz, t // tt),
        in_specs=[
            pl.BlockSpec((1, tt, d), lambda b, i: (b, i, 0)),
            _mod_spec(sc, tt),
            _mod_spec(sh, tt),
            pl.BlockSpec((d, n), lambda b, i: (0, 0)),
        ],
        out_specs=pl.BlockSpec((1, tt, n), lambda b, i: (b, i, 0)),
        out_shape=jax.ShapeDtypeStruct((bsz, t, n), F32),
        compiler_params=_cp(("parallel", "parallel")),
        name="modlinear",
    )(x, sc, sh, w_bf)


def _qproj_kernel(x_ref, sc_ref, sh_ref, w_ref, o_ref, *, n_kv, qw):
    hin = x_ref[0] * (1.0 + sc_ref[0]) + sh_ref[0]
    y = jnp.dot(hin.astype(BF16), w_ref[...], preferred_element_type=F32)
    gw = qw + LANES
    for g in range(n_kv):
        o_ref[0, g, :, :qw] = y[:, g * gw:g * gw + qw]
        o_ref[0, g, :, qw:] = jax.nn.sigmoid(y[:, g * gw + qw:(g + 1) * gw])


def _qproj(x, sc, sh, w_bf, *, n_kv, qw, tt):
    bsz, t, d = x.shape
    gw = qw + LANES
    return pl.pallas_call(
        functools.partial(_qproj_kernel, n_kv=n_kv, qw=qw),
        grid=(bsz, t // tt),
        in_specs=[
            pl.BlockSpec((1, tt, d), lambda b, i: (b, i, 0)),
            _mod_spec(sc, tt),
            _mod_spec(sh, tt),
            pl.BlockSpec((d, n_kv * gw), lambda b, i: (0, 0)),
        ],
        out_specs=pl.BlockSpec((1, n_kv, tt, gw), lambda b, i: (b, 0, i, 0)),
        out_shape=jax.ShapeDtypeStruct((bsz, n_kv, t, gw), F32),
        compiler_params=_cp(("parallel", "parallel")),
        name="qproj",
    )(x, sc, sh, w_bf)


def _log1p(x):
    u = 1.0 + x
    return jnp.where(u == 1.0, x, jnp.log(u) * x / jnp.where(u == 1.0, 1.0, u - 1.0))


def _expm1(x):
    u = jnp.exp(x)
    safe = (u != 1.0) & (u > 0.0)
    return jnp.where(u == 1.0, x, jnp.where(u > 0.0, (u - 1.0) * x / jnp.where(safe, jnp.log(u), 1.0), -1.0))


def _softplus(z):
    return jnp.maximum(z, 0.0) + _log1p(jnp.exp(-jnp.abs(z)))


def _rglru_gates(xc, wg_ref, bg_ref, lam_ref, n_heads):
    w = xc.shape[1]
    bw = w // n_heads
    xcb = xc.astype(BF16)
    gates = []
    for g in range(2):
        cols = [jnp.dot(xcb[:, i * bw:(i + 1) * bw], wg_ref[g, i], preferred_element_type=F32) for i in range(n_heads)]
        gates.append(jnp.concatenate(cols, axis=1) + bg_ref[g:g + 1, :])
    r = jax.nn.sigmoid(gates[0])
    i_g = jax.nn.sigmoid(gates[1])
    log_a = (-RGLRU_C * _softplus(-lam_ref[...])) * r
    a = jnp.exp(log_a)
    b_in = jnp.sqrt(-_expm1(2.0 * log_a)) * (i_g * xc)
    return a, b_in


def _rglru_seq_kernel(x_ref, sc_ref, sh_ref, conv0_ref, h0_ref, win_ref, wconv_ref, bconv_ref, wg_ref, bg_ref, lam_ref,
                      gated_ref, convout_ref, hout_ref, ext_ref, a_ref, b_ref, yb_ref, h_ref, *, tt, n_heads, cw):
    t = pl.program_id(1)
    w = a_ref.shape[1]
    pad = 8

    @pl.when(t == 0)
    def _():
        ext_ref[0:pad - (cw - 1), :] = jnp.zeros((pad - (cw - 1), w), F32)
        ext_ref[pad - (cw - 1):pad, :] = conv0_ref[0]
        h_ref[...] = h0_ref[0]

    hin = x_ref[0] * (1.0 + sc_ref[0]) + sh_ref[0]
    u = jnp.dot(hin.astype(BF16), win_ref[...], preferred_element_type=F32)
    yb_ref[...] = jax.nn.gelu(u[:, :w])
    ext_ref[pad:pad + tt, :] = u[:, w:]
    xc = bconv_ref[...] + sum(ext_ref[pad - (cw - 1) + k:pad - (cw - 1) + k + tt, :] * wconv_ref[k:k + 1, :] for k in range(cw))
    tail = ext_ref[tt:tt + pad, :]
    ext_ref[0:pad, :] = tail

    a, b_in = _rglru_gates(xc, wg_ref, bg_ref, lam_ref, n_heads)
    a_ref[...] = a
    b_ref[...] = b_in

    row = lax.broadcasted_iota(jnp.int32, (tt, LANES), 0)
    for c in range(w // LANES):
        cs = slice(c * LANES, (c + 1) * LANES)
        av = a_ref[:, cs]
        bv = b_ref[:, cs]
        s = 1
        while s < tt:
            keep = row >= s
            b_sh = jnp.where(keep, pltpu.roll(bv, s, 0), 0.0)
            a_sh = jnp.where(keep, pltpu.roll(av, s, 0), 1.0)
            bv = av * b_sh + bv
            av = av * a_sh
            s *= 2
        hs = av * h_ref[:, cs] + bv
        h_ref[:, cs] = hs[tt - 1:tt, :]
        gated_ref[0, :, cs] = (hs * yb_ref[:, cs]).astype(gated_ref.dtype)

    @pl.when(t == pl.num_programs(1) - 1)
    def _():
        convout_ref[0] = tail[pad - (cw - 1):pad, :]
        hout_ref[0] = h_ref[...]


def _rglru_seq(x, sc, sh, conv0, h0, win_bf, wconv, bconv, wg_bf, bg, lam, *, tt):
    bsz, t, d = x.shape
    cw, w = wconv.shape
    n_heads = wg_bf.shape[1]
    full = lambda shape: pl.BlockSpec(shape, lambda b, i: (0,) * len(shape))
    return pl.pallas_call(
        functools.partial(_rglru_seq_kernel, tt=tt, n_heads=n_heads, cw=cw),
        grid=(bsz, t // tt),
        in_specs=[
            pl.BlockSpec((1, tt, d), lambda b, i: (b, i, 0)),
            _mod_spec(sc, tt),
            _mod_spec(sh, tt),
            pl.BlockSpec((1, cw - 1, w), lambda b, i: (b, 0, 0)),
            pl.BlockSpec((1, 1, w), lambda b, i: (b, 0, 0)),
            full((d, 2 * w)),
            full((cw, w)),
            full((1, w)),
            full(wg_bf.shape),
            full((2, w)),
            full((1, w)),
        ],
        out_specs=[
            pl.BlockSpec((1, tt, w), lambda b, i: (b, i, 0)),
            pl.BlockSpec((1, cw - 1, w), lambda b, i: (b, 0, 0)),
            pl.BlockSpec((1, 1, w), lambda b, i: (b, 0, 0)),
        ],
        out_shape=[
            jax.ShapeDtypeStruct((bsz, t, w), BF16),
            jax.ShapeDtypeStruct((bsz, cw - 1, w), F32),
            jax.ShapeDtypeStruct((bsz, 1, w), F32),
        ],
        scratch_shapes=[
            pltpu.VMEM((tt + 8, w), F32),
            pltpu.VMEM((tt, w), F32),
            pltpu.VMEM((tt, w), F32),
            pltpu.VMEM((tt, w), F32),
            pltpu.VMEM((1, w), F32),
        ],
        compiler_params=_cp(("parallel", "arbitrary")),
        name="rglru_seq",
    )(x, sc, sh, conv0, h0, win_bf, wconv, bconv, wg_bf, bg, lam)


def _rglru_step_kernel(x_ref, sc_ref, sh_ref, conv_ref, h0_ref, win_ref, wconv_ref, bconv_ref, wg_ref, bg_ref, lam_ref,
                       gated_ref, convout_ref, hout_ref, *, n_heads, cw):
    w = h0_ref.shape[1]
    hin = x_ref[...] * (1.0 + sc_ref[...]) + sh_ref[...]
    u = jnp.dot(hin.astype(BF16), win_ref[...], preferred_element_type=F32)
    yb = jax.nn.gelu(u[:, :w])
    xb = u[:, w:]
    taps = [conv_ref[k] for k in range(cw - 1)] + [xb]
    xc = bconv_ref[...] + sum(taps[k] * wconv_ref[k:k + 1, :] for k in range(cw))
    a, b_in = _rglru_gates(xc, wg_ref, bg_ref, lam_ref, n_heads)
    h = a * h0_ref[...] + b_in
    gated_ref[...] = (h * yb).astype(gated_ref.dtype)
    for k in range(cw - 1):
        convout_ref[k] = taps[k + 1]
    hout_ref[...] = h


def _rglru_step(x, sc, sh, conv, h0, win_bf, wconv, bconv, wg_bf, bg, lam):
    r, _ = x.shape
    cw, w = wconv.shape
    n_heads = wg_bf.shape[1]
    return pl.pallas_call(
        functools.partial(_rglru_step_kernel, n_heads=n_heads, cw=cw),
        out_shape=[
            jax.ShapeDtypeStruct((r, w), BF16),
            jax.ShapeDtypeStruct((cw - 1, r, w), F32),
            jax.ShapeDtypeStruct((r, w), F32),
        ],
        compiler_params=pltpu.CompilerParams(vmem_limit_bytes=VMEM_LIMIT),
        name="rglru_step",
    )(x, sc, sh, conv, h0, win_bf, wconv, bconv, wg_bf, bg, lam)


def _layer_norm(v, g, b):
    mu = jnp.mean(v, axis=-1, keepdims=True)
    dv = v - mu
    var = jnp.mean(dv * dv, axis=-1, keepdims=True)
    return dv * lax.rsqrt(var + LN_EPS) * g + b


def _post_kernel(a_ref, x_ref, g1_ref, sc2_ref, sh2_ref, wout_ref, lng_ref, lnb_ref, wr_ref, br_ref,
                 x1_ref, hin2_ref, rt_ref, *, alpha, n_experts):
    y = jnp.dot(a_ref[0].astype(BF16), wout_ref[...], preferred_element_type=F32)
    x1 = _layer_norm(alpha * x_ref[0] + (1.0 + g1_ref[0]) * y, lng_ref[...], lnb_ref[...])
    x1_ref[0] = x1
    hin2 = x1 * (1.0 + sc2_ref[0]) + sh2_ref[0]
    hin2_ref[0] = hin2.astype(hin2_ref.dtype)
    logits = jnp.dot(hin2, wr_ref[...], preferred_element_type=F32, precision=HIGHEST) + br_ref[...]
    lane = lax.broadcasted_iota(jnp.int32, logits.shape, 1)
    work = jnp.where(lane < n_experts, logits, -jnp.inf)
    vals, idxs = [], []
    for _ in range(TOP_K):
        m = jnp.max(work, axis=-1, keepdims=True)
        idx = jnp.min(jnp.where(work == m, lane, LANES), axis=-1, keepdims=True)
        vals.append(m)
        idxs.append(idx)
        work = jnp.where(lane == idx, -jnp.inf, work)
    es = [jnp.exp(v - vals[0]) for v in vals]
    z = sum(es)
    rt = jnp.zeros(logits.shape, F32)
    for k in range(TOP_K):
        rt = jnp.where(lane == k, es[k] / z, rt)
        rt = jnp.where(lane == TOP_K + k, idxs[k].astype(F32), rt)
    rt_ref[0] = rt


def _post(a, x, g1, sc2, sh2, wout_bf, lng, lnb, wr_pad, br_pad, *, alpha, n_experts, tt):
    bsz, t, d = x.shape
    dk = a.shape[-1]
    full = lambda shape: pl.BlockSpec(shape, lambda b, i: (0,) * len(shape))
    row = pl.BlockSpec((1, tt, d), lambda b, i: (b, i, 0))
    return pl.pallas_call(
        functools.partial(_post_kernel, alpha=alpha, n_experts=n_experts),
        grid=(bsz, t // tt),
        in_specs=[
            pl.BlockSpec((1, tt, dk), lambda b, i: (b, i, 0)),
            row,
            _mod_spec(g1, tt),
            _mod_spec(sc2, tt),
            _mod_spec(sh2, tt),
            full((dk, d)),
            full((1, d)),
            full((1, d)),
            full((d, LANES)),
            full((1, LANES)),
        ],
        out_specs=[row, row, pl.BlockSpec((1, tt, LANES), lambda b, i: (b, i, 0))],
        out_shape=[
            jax.ShapeDtypeStruct((bsz, t, d), F32),
            jax.ShapeDtypeStruct((bsz, t, d), BF16),
            jax.ShapeDtypeStruct((bsz, t, LANES), F32),
        ],
        compiler_params=_cp(("parallel", "parallel")),
        name="post",
    )(a, x, g1, sc2, sh2, wout_bf, lng, lnb, wr_pad, br_pad)


def _swiglu(g, u):
    g = jnp.minimum(g, SWIGLU_LIMIT)
    u = jnp.clip(u, -SWIGLU_LIMIT, SWIGLU_LIMIT)
    return g * jax.nn.sigmoid(SWIGLU_ALPHA * g) * (u + 1.0)


def _moe_grouped_kernel(be_ref, bv_ref, xs_ref, wgu_ref, bgu_ref, wdn_ref, bdn_ref, ys_ref, wgu_bf, wdn_bf, *, chunk):
    i = pl.program_id(0)
    d, de2 = wgu_bf.shape
    de = de2 // 2
    e = be_ref[i]
    prev = be_ref[jnp.maximum(i - 1, 0)]

    @pl.when((i == 0) | (e != prev))
    def _():
        for c in range(de2 // chunk):
            wgu_bf[:, c * chunk:(c + 1) * chunk] = wgu_ref[0, :, c * chunk:(c + 1) * chunk].astype(BF16)
        for c in range(de // chunk):
            wdn_bf[c * chunk:(c + 1) * chunk, :] = wdn_ref[0, c * chunk:(c + 1) * chunk, :].astype(BF16)

    @pl.when(bv_ref[i] > 0)
    def _():
        x = xs_ref[...]
        acc = jnp.zeros(ys_ref.shape, F32) + bdn_ref[0]
        for c in range(de // chunk):
            g = jnp.dot(x, wgu_bf[:, c * chunk:(c + 1) * chunk], preferred_element_type=F32) + bgu_ref[0, :, c * chunk:(c + 1) * chunk]
            u = jnp.dot(x, wgu_bf[:, de + c * chunk:de + (c + 1) * chunk], preferred_element_type=F32) + bgu_ref[0, :, de + c * chunk:de + (c + 1) * chunk]
            h = _swiglu(g, u).astype(BF16)
            acc = acc + jnp.dot(h, wdn_bf[c * chunk:(c + 1) * chunk, :], preferred_element_type=F32)
        ys_ref[...] = acc

    @pl.when(bv_ref[i] == 0)
    def _():
        ys_ref[...] = jnp.zeros(ys_ref.shape, F32)


def _moe_grouped(block_e, block_valid, xs, w_gu, b_gu, w_dn, b_dn, *, tm):
    n_rows, d = xs.shape
    n_e, _, de2 = w_gu.shape
    de = de2 // 2
    grid_spec = pltpu.PrefetchScalarGridSpec(
        num_scalar_prefetch=2,
        grid=(n_rows // tm,),
        in_specs=[
            pl.BlockSpec((tm, d), lambda i, be, bv: (i, 0)),
            pl.BlockSpec((1, d, de2), lambda i, be, bv: (be[i], 0, 0)),
            pl.BlockSpec((1, 1, de2), lambda i, be, bv: (be[i], 0, 0)),
            pl.BlockSpec((1, de, d), lambda i, be, bv: (be[i], 0, 0)),
            pl.BlockSpec((1, 1, d), lambda i, be, bv: (be[i], 0, 0)),
        ],
        out_specs=pl.BlockSpec((tm, d), lambda i, be, bv: (i, 0)),
        scratch_shapes=[pltpu.VMEM((d, de2), BF16), pltpu.VMEM((de, d), BF16)],
    )
    return pl.pallas_call(
        functools.partial(_moe_grouped_kernel, chunk=min(512, de)),
        grid_spec=grid_spec,
        out_shape=jax.ShapeDtypeStruct((n_rows, d), F32),
        compiler_params=_cp(("arbitrary",), 56 * 1024 * 1024),
        name="moe_grouped",
    )(block_e, block_valid, xs, w_gu, b_gu.reshape(n_e, 1, de2), w_dn, b_dn.reshape(n_e, 1, d))


def _moe_dense_kernel(x_ref, gd_ref, wg_ref, wu_ref, bg_ref, bu_ref, wdn_ref, bdn_ref, o_ref):
    e = pl.program_id(0)
    c = pl.program_id(1)

    @pl.when((e == 0) & (c == 0))
    def _():
        o_ref[...] = jnp.zeros(o_ref.shape, F32)

    gd = gd_ref[...]
    lane = lax.broadcasted_iota(jnp.int32, gd.shape, 1)
    gate = jnp.sum(jnp.where(lane == e, gd, 0.0), axis=-1, keepdims=True)
    x = x_ref[...]
    g = jnp.dot(x, wg_ref[0].astype(BF16), preferred_element_type=F32) + bg_ref[0]
    u = jnp.dot(x, wu_ref[0].astype(BF16), preferred_element_type=F32) + bu_ref[0]
    h = _swiglu(g, u).astype(BF16)
    y = jnp.dot(h, wdn_ref[0].astype(BF16), preferred_element_type=F32)
    y = y + jnp.where(c == 0, 1.0, 0.0) * bdn_ref[0]
    o_ref[...] += gate * y


def _moe_dense(x_bf, gate_dense, w_gu, b_gu, w_dn, b_dn, *, chunk=256):
    r, d = x_bf.shape
    n_e, _, de2 = w_gu.shape
    de = de2 // 2
    nc = de // chunk
    b_gu3 = b_gu.reshape(n_e, 1, de2)
    return pl.pallas_call(
        _moe_dense_kernel,
        grid=(n_e, nc),
        in_specs=[
            pl.BlockSpec((r, d), lambda e, c: (0, 0)),
            pl.BlockSpec((r, LANES), lambda e, c: (0, 0)),
            pl.BlockSpec((1, d, chunk), lambda e, c: (e, 0, c)),
            pl.BlockSpec((1, d, chunk), lambda e, c: (e, 0, nc + c)),
            pl.BlockSpec((1, 1, chunk), lambda e, c: (e, 0, c)),
            pl.BlockSpec((1, 1, chunk), lambda e, c: (e, 0, nc + c)),
            pl.BlockSpec((1, chunk, d), lambda e, c: (e, c, 0)),
            pl.BlockSpec((1, 1, d), lambda e, c: (e, 0, 0)),
        ],
        out_specs=pl.BlockSpec((r, d), lambda e, c: (0, 0)),
        out_shape=jax.ShapeDtypeStruct((r, d), F32),
        compiler_params=_cp(("arbitrary", "arbitrary")),
        name="moe_dense",
    )(x_bf, gate_dense, w_gu, w_gu, b_gu3, b_gu3, w_dn, b_dn.reshape(n_e, 1, d))


def _final_kernel(x_ref, ff_ref, rt_ref, g2_ref, lng_ref, lnb_ref, o_ref, *, alpha, combine):
    if combine:
        rt = rt_ref[0]
        ff = sum(rt[:, k:k + 1] * ff_ref[k, 0] for k in range(TOP_K))
    else:
        ff = ff_ref[0]
    o_ref[0] = _layer_norm(alpha * x_ref[0] + (1.0 + g2_ref[0]) * ff, lng_ref[...], lnb_ref[...])


def _final(x1, ff, rt, g2, lng, lnb, *, alpha, combine, tt):
    bsz, t, d = x1.shape
    row = pl.BlockSpec((1, tt, d), lambda b, i: (b, i, 0))
    ff_spec = pl.BlockSpec((TOP_K, 1, tt, d), lambda b, i: (0, b, i, 0)) if combine else row
    full = lambda shape: pl.BlockSpec(shape, lambda b, i: (0,) * len(shape))
    return pl.pallas_call(
        functools.partial(_final_kernel, alpha=alpha, combine=combine),
        grid=(bsz, t // tt),
        in_specs=[row, ff_spec, pl.BlockSpec((1, tt, LANES), lambda b, i: (b, i, 0)), _mod_spec(g2, tt), full((1, d)), full((1, d))],
        out_specs=row,
        out_shape=jax.ShapeDtypeStruct((bsz, t, d), F32),
        compiler_params=_cp(("parallel", "parallel")),
        name="final",
    )(x1, ff, rt, g2, lng, lnb)


def _bias_chain(dist, tab_ref, h0, n_heads_tab, group, thr):
    out = [jnp.full(dist.shape, tab_ref[h0 + r], F32) for r in range(group)]
    for b in range(1, len(thr)):
        ind = dist >= thr[b]
        out = [jnp.where(ind, tab_ref[b * n_heads_tab + h0 + r], out[r]) for r in range(group)]
    return out


def _online_update(s, v, m_ref, l_ref, acc_ref):
    m_old = m_ref[...]
    m_new = jnp.maximum(m_old, jnp.max(s, axis=-1, keepdims=True))
    e = jnp.exp(s - m_new)
    alpha = jnp.exp(m_old - m_new)
    l_ref[...] = alpha * l_ref[...] + jnp.sum(e, axis=-1, keepdims=True)
    acc_ref[...] = alpha * acc_ref[...] + jnp.dot(e.astype(BF16), v, preferred_element_type=F32)
    m_ref[...] = m_new


def _nsa_seq_kernel(tab_ref, qg_ref, kc_ref, vc_ref, ks_ref, vs_ref, kw_ref, vw_ref, o_ref,
                    m_ref, l_ref, acc_ref, out_ref, *, tq, tk, tw, group, dh, n_heads, thr, scale, cmp_block,
                    n_cmp_real):
    g = pl.program_id(1)
    q0 = pl.program_id(2) * tq
    h0 = g * group
    far = thr[-1]
    n_cmp = kc_ref.shape[2]
    qw = group * dh
    rows = group * tq
    nt = (((1,), (1,)), ((), ()))

    qv = (qg_ref[0, 0, :, :qw] * scale).astype(BF16)
    q4 = jnp.concatenate([qv[:, r * dh:(r + 1) * dh] for r in range(group)], axis=0)
    gates = qg_ref[0, 0, :, qw:]
    qpos = q0 + lax.broadcasted_iota(jnp.int32, (tq, 1), 0)
    last_bias = [tab_ref[(len(thr) - 1) * n_heads + h0 + r] for r in range(group)]

    def finish(branch):
        l = l_ref[...]
        o = acc_ref[...] / jnp.where(l > 0, l, 1.0)
        for r in range(group):
            gr = gates[:, r * 3 + branch:r * 3 + branch + 1]
            sl = slice(r * tq, (r + 1) * tq)
            if branch == 0:
                out_ref[sl, :] = gr * o[sl]
            else:
                out_ref[sl, :] += gr * o[sl]

    def reset():
        m_ref[...] = jnp.full(m_ref.shape, M_INIT, F32)
        l_ref[...] = jnp.zeros(l_ref.shape, F32)
        acc_ref[...] = jnp.zeros(acc_ref.shape, F32)

    cend = lax.broadcasted_iota(jnp.int32, (1, n_cmp), 1) * CMP_STRIDE + (cmp_block - 1)
    dist_c = qpos - cend
    s_c = lax.dot_general(q4, kc_ref[0, 0], nt, preferred_element_type=F32)
    bias_c = _bias_chain(dist_c, tab_ref, h0, n_heads, group, thr)
    mask_c = (dist_c >= 0) & (lax.broadcasted_iota(jnp.int32, (1, n_cmp), 1) < n_cmp_real)
    s_c = jnp.concatenate([jnp.where(mask_c, s_c[r * tq:(r + 1) * tq] + bias_c[r], NEG_INF) for r in range(group)], axis=0)
    m_c = jnp.maximum(jnp.max(s_c, axis=-1, keepdims=True), M_INIT)
    e_c = jnp.exp(s_c - m_c)
    z_c = jnp.sum(e_c, axis=-1, keepdims=True)
    p_c = e_c / jnp.where(z_c > 0, z_c, 1.0)
    l_ref[...] = jnp.ones(l_ref.shape, F32)
    acc_ref[...] = jnp.dot(p_c.astype(BF16), vc_ref[0, 0], preferred_element_type=F32)
    finish(0)

    psum = sum(p_c[r * tq:(r + 1) * tq] for r in range(group))
    ci = lax.broadcasted_iota(jnp.int32, (n_cmp, LANES), 0) * CMP_STRIDE
    bj = lax.broadcasted_iota(jnp.int32, (n_cmp, LANES), 1) * SEL_BLOCK
    overlap = jnp.where((ci < bj + SEL_BLOCK) & (ci + cmp_block > bj), 1.0, 0.0)
    imp = jnp.dot(psum, overlap, preferred_element_type=F32, precision=HIGHEST)
    blk = lax.broadcasted_iota(jnp.int32, (tq, LANES), 1)
    cur = qpos // SEL_BLOCK
    n_blk = ks_ref.shape[2] // SEL_BLOCK
    forced = (blk == 0) | ((blk >= cur - 1) & (blk <= cur))
    valid = blk * SEL_BLOCK <= qpos
    score = jnp.where(forced, SEL_BIG, jnp.where(valid, imp, -SEL_BIG))
    score = jnp.where(blk < n_blk, score, -jnp.inf)
    sel = jnp.zeros((tq, LANES), F32)
    for _ in range(min(N_SELECT, n_blk)):
        mx = jnp.max(score, axis=-1, keepdims=True)
        idx = jnp.min(jnp.where(score == mx, blk, LANES), axis=-1, keepdims=True)
        hit = blk == idx
        sel = jnp.where(hit, 1.0, sel)
        score = jnp.where(hit, -jnp.inf, score)
    sel_bf = sel.astype(BF16)

    reset()

    def sel_tile(t, near):
        k0 = pl.multiple_of(t * tk, tk)
        kt = ks_ref[0, 0, pl.ds(k0, tk), :]
        vt = vs_ref[0, 0, pl.ds(k0, tk), :]
        ej = lax.broadcasted_iota(jnp.int32, (LANES, tk), 0)
        ek = lax.broadcasted_iota(jnp.int32, (LANES, tk), 1) // SEL_BLOCK + k0 // SEL_BLOCK
        expand = jnp.where(ej == ek, 1.0, 0.0).astype(BF16)
        mask = jnp.dot(sel_bf, expand, preferred_element_type=F32) > 0.5
        s = lax.dot_general(q4, kt, nt, preferred_element_type=F32)
        if near:
            dist = qpos - (k0 + lax.broadcasted_iota(jnp.int32, (1, tk), 1))
            bias = _bias_chain(dist, tab_ref, h0, n_heads, group, thr)
            mask = mask & (dist >= 0)
        else:
            bias = last_bias
        s = jnp.concatenate([jnp.where(mask, s[r * tq:(r + 1) * tq] + bias[r], NEG_INF) for r in range(group)], axis=0)
        _online_update(s, vt, m_ref, l_ref, acc_ref)

    n_t = (q0 + tq + tk - 1) // tk
    n_far = jnp.maximum(q0 - far + 1, 0) // tk

    def far_body(t, carry):
        sel_tile(t, False)
        return carry

    def near_body(t, carry):
        sel_tile(t, True)
        return carry

    lax.fori_loop(0, n_far, far_body, 0)
    lax.fori_loop(n_far, n_t, near_body, 0)
    finish(1)

    reset()
    n_w = WINDOW // tw + tq // tw
    for t in range(n_w):
        off = WINDOW - t * tw
        lo, hi = off - (tw - 1), off + (tq - 1)

        def win_tile(off=off, lo=lo, hi=hi):
            k0 = pl.multiple_of(q0 - off, tw)
            kt = kw_ref[0, 0, pl.ds(k0, tw), :]
            vt = vw_ref[0, 0, pl.ds(k0, tw), :]
            s = lax.dot_general(q4, kt, nt, preferred_element_type=F32)
            dist = off + lax.broadcasted_iota(jnp.int32, (tq, tw), 0) - lax.broadcasted_iota(jnp.int32, (tq, tw), 1)
            bias = _bias_chain(dist, tab_ref, h0, n_heads, group, thr) if lo < far else last_bias
            mask = None
            if lo < 0:
                mask = dist >= 0
            if hi > WINDOW:
                m2 = dist <= WINDOW
                mask = m2 if mask is None else (mask & m2)
            parts = []
            for r in range(group):
                sr = s[r * tq:(r + 1) * tq] + bias[r]
                parts.append(sr if mask is None else jnp.where(mask, sr, NEG_INF))
            _online_update(jnp.concatenate(parts, axis=0), vt, m_ref, l_ref, acc_ref)

        if off > 0:
            pl.when(q0 - off >= 0)(win_tile)
        else:
            win_tile()
    finish(2)

    o = out_ref[...]
    o_ref[0] = jnp.concatenate([o[r * tq:(r + 1) * tq] for r in range(group)], axis=1)


def _nsa_seq(tab, qg, kc, vc, ks, vs, kw, vw, *, group, dh, cmp_block, n_cmp_real, tq=128, tk=256, tw=128):
    bsz, n_kv, s, gw = qg.shape
    n_heads = n_kv * group
    n_cmp = kc.shape[2]
    thr = tuple(_rel_thresholds(tab.shape[0] // n_heads))
    assert thr[-1] <= tw and WINDOW % tw == 0 and tq % tw == 0 and s % tk == 0 and s % tq == 0
    kv_spec = pl.BlockSpec((1, 1, s, dh), lambda b, g, i: (b, g, 0, 0))
    c_spec = pl.BlockSpec((1, 1, n_cmp, dh), lambda b, g, i: (b, g, 0, 0))
    rows = group * tq
    kern = functools.partial(_nsa_seq_kernel, tq=tq, tk=tk, tw=tw, group=group, dh=dh, n_heads=n_heads, thr=thr,
                             scale=dh ** -0.5, cmp_block=cmp_block, n_cmp_real=n_cmp_real)
    return pl.pallas_call(
        kern,
        grid=(bsz, n_kv, s // tq),
        in_specs=[
            pl.BlockSpec(memory_space=pltpu.SMEM),
            pl.BlockSpec((1, 1, tq, gw), lambda b, g, i: (b, g, i, 0)),
            c_spec, c_spec, kv_spec, kv_spec, kv_spec, kv_spec,
        ],
        out_specs=pl.BlockSpec((1, tq, group * dh), lambda b, g, i: (b, i, g)),
        out_shape=jax.ShapeDtypeStruct((bsz, s, n_heads * dh), F32),
        scratch_shapes=[
            pltpu.VMEM((rows, 1), F32),
            pltpu.VMEM((rows, 1), F32),
            pltpu.VMEM((rows, dh), F32),
            pltpu.VMEM((rows, dh), F32),
        ],
        compiler_params=_cp(("parallel", "parallel", "arbitrary")),
        name="nsa_seq",
    )(tab, qg, kc, vc, ks, vs, kw, vw)


def _rel_bucket(dist, n_buckets):
    d = jnp.maximum(dist, 0)
    exact = n_buckets // 2
    df = jnp.maximum(d, 1).astype(F32)
    large = exact + (jnp.log(df / exact) / math.log(REL_MAX_DIST / exact) * (n_buckets - exact)).astype(jnp.int32)
    return jnp.where(d < exact, d, jnp.minimum(large, n_buckets - 1))


def _masked_softmax(s, mask):
    s = jnp.where(mask, s, NEG_INF)
    e = jnp.where(mask, jnp.exp(s - s.max(-1, keepdims=True)), 0.0)
    z = e.sum(-1, keepdims=True)
    return e / jnp.where(z > 0, z, 1.0)


def _compress(k, w1, b1, w2, b2, pe):
    bsz, length, n_kv, dh = k.shape
    cmp_block = pe.shape[0]
    ratio = cmp_block // CMP_STRIDE
    hidden = w1.shape[-1]
    nc = (length - cmp_block) // CMP_STRIDE + 1
    nch = nc + ratio - 1
    chunks = k[:, :nch * CMP_STRIDE].reshape(bsz, nch, CMP_STRIDE, n_kv, dh)
    part = jnp.einsum('bnpgd,updh->bungh', chunks, w1.reshape(ratio, CMP_STRIDE, dh, hidden))
    pre = sum(part[:, u:u + nc, u] for u in range(ratio)) + jnp.einsum('pd,pdh->h', pe, w1) + b1
    return jnp.einsum('bngh,hd->bngd', jax.nn.gelu(pre), w2) + b2


def _nsa_attend_single(q, gates, qpos, kc, vc, ks, vs, kw, vw, wpos, rel_table, n_kv, cmp_block):
    bsz, n_heads, dh = q.shape
    group = n_heads // n_kv
    n_buckets = rel_table.shape[0]
    scale = dh ** -0.5
    qg = q.reshape(bsz, n_kv, group, dh)
    table = rel_table.astype(F32).reshape(n_buckets, n_kv, group)
    n_cmp = kc.shape[1]
    cend = jnp.arange(n_cmp) * CMP_STRIDE + cmp_block - 1
    dist_c = qpos - cend
    s_c = jnp.einsum('bgrd,bcgd->bgrc', qg, kc, preferred_element_type=F32) * scale + table[_rel_bucket(dist_c, n_buckets)].transpose(1, 2, 0)
    p_c = _masked_softmax(s_c, (dist_c >= 0)[None, None, None, :])
    o_c = jnp.einsum('bgrc,bcgd->bgrd', p_c, vc)
    length = ks.shape[1]
    ns = -(-length // SEL_BLOCK)
    ci = jnp.arange(n_cmp)[:, None]
    bj = jnp.arange(ns)[None, :]
    overlap = ((ci * CMP_STRIDE < (bj + 1) * SEL_BLOCK) & (ci * CMP_STRIDE + cmp_block > bj * SEL_BLOCK)).astype(F32)
    imp = jnp.einsum('bgc,cn->bgn', p_c.sum(2), overlap, precision=HIGHEST)
    blk = jnp.arange(ns)
    cur = qpos // SEL_BLOCK
    forced = (blk == 0) | ((blk >= cur - 1) & (blk <= cur))
    valid = blk * SEL_BLOCK <= qpos
    score = jnp.where(forced, SEL_BIG, jnp.where(valid, imp, -SEL_BIG))
    _, sel = lax.top_k(score, min(N_SELECT, ns))
    selmask = jnp.zeros((bsz, n_kv, ns), bool).at[jnp.arange(bsz)[:, None, None], jnp.arange(n_kv)[None, :, None], sel].set(True)
    kpos = jnp.arange(length)
    keymask = selmask[:, :, kpos // SEL_BLOCK] & (kpos <= qpos)[None, None, :]
    dist_s = qpos - kpos
    s_s = jnp.einsum('bgrd,blgd->bgrl', qg, ks, preferred_element_type=F32) * scale + table[_rel_bucket(dist_s, n_buckets)].transpose(1, 2, 0)
    p_s = _masked_softmax(s_s, keymask[:, :, None, :])
    o_s = jnp.einsum('bgrl,blgd->bgrd', p_s, vs)
    dist_w = qpos - wpos
    mask_w = (dist_w >= 0) & (dist_w <= WINDOW) & (wpos >= 0)
    s_w = jnp.einsum('bgrd,bkgd->bgrk', qg, kw, preferred_element_type=F32) * scale + table[_rel_bucket(dist_w, n_buckets)].transpose(1, 2, 0)
    p_w = _masked_softmax(s_w, mask_w[None, None, None, :])
    o_w = jnp.einsum('bgrk,bkgd->bgrd', p_w, vw)
    gt = gates.reshape(bsz, n_kv, group, 3)
    o = gt[..., 0:1] * o_c + gt[..., 1:2] * o_s + gt[..., 2:3] * o_w
    return o.reshape(bsz, n_heads * dh)


def _route(rt, n_experts, blk):
    n_tok = rt.shape[0]
    top_e = rt[:, TOP_K:2 * TOP_K].astype(jnp.int32)
    n_as = n_tok * TOP_K
    n_blocks = -(-n_as // blk) + n_experts
    flat_e = top_e.reshape(-1)
    order = jnp.argsort(flat_e)
    se = flat_e[order]
    counts = jnp.bincount(flat_e, length=n_experts)
    padded = (counts + blk - 1) // blk * blk
    pad_end = jnp.cumsum(padded)
    pad_start = pad_end - padded
    start = jnp.cumsum(counts) - counts
    dest = (pad_start[se] + jnp.arange(n_as) - start[se]).astype(jnp.int32)
    tok = (order // TOP_K).astype(jnp.int32)
    row_tok = jnp.full((n_blocks * blk,), n_tok, jnp.int32).at[dest].set(tok)
    block_start = jnp.arange(n_blocks) * blk
    block_e = jnp.minimum(jnp.searchsorted(pad_end, block_start, side='right'), n_experts - 1).astype(jnp.int32)
    block_valid = (block_start < pad_end[-1]).astype(jnp.int32)
    pos = jnp.zeros((n_as,), jnp.int32).at[order].set(dest).reshape(n_tok, TOP_K)
    return row_tok, block_e, block_valid, pos


def _split_mod(mod, bsz, per_row):
    parts = jnp.split(mod, 6, axis=-1)
    if per_row:
        return [p[None] for p in parts]
    return [p[:, None] for p in parts]


def _trunk(x, mods, kv_mod, conv0, h0, make_attend, p, *, per_row):
    depth = p['w_ada'].shape[0]
    n_a = p['w_in_a'].shape[0]
    bsz, t, d = x.shape
    n_experts = p['w_router'].shape[-1]
    alpha = (2 * depth) ** 0.25
    tt = min(256, t)
    conv_new, h_new = [], []
    attend, kv_state = None, None
    for l in range(depth):
        sh1, sc1, g1, sh2, sc2, g2 = _split_mod(mods[l], bsz, per_row)
        if l < n_a:
            win_bf = p['w_in_a'][l].astype(BF16)
            wg_bf = p['w_gate_a'][l].astype(BF16)
            args = (win_bf, p['w_conv'][l], p['b_conv'][l][None], wg_bf, p['b_gate_a'][l], p['lru_lambda'][l][None])
            if per_row:
                gated, cb, hl = _rglru_step(x[0], sc1[0], sh1[0], conv0[l].swapaxes(0, 1), h0[l], *args)
                gated, cb = gated[None], cb.swapaxes(0, 1)
            else:
                gated, cb, hl = _rglru_seq(x, sc1, sh1, conv0[l], h0[l][:, None], *args, tt=tt)
                hl = hl[:, 0]
            conv_new.append(cb)
            h_new.append(hl)
            mix, wout = gated, p['w_out_a'][l]
        else:
            if l == n_a:
                ksh, ksc = jnp.split(kv_mod, 2, axis=-1)
                ksh, ksc = (ksh[None], ksc[None]) if per_row else (ksh[:, None], ksc[:, None])
                kv = _modlinear(x, ksc, ksh, p['w_kv'].astype(BF16), tt=tt)
                attend, kv_state = make_attend(kv)
            lb = l - n_a
            mix, wout = attend(x, sc1, sh1, p['w_in_b'][lb]), p['w_out_b'][lb]
        wr_pad = jnp.zeros((d, LANES), F32).at[:, :n_experts].set(p['w_router'][l])
        br_pad = jnp.zeros((1, LANES), F32).at[0, :n_experts].set(p['b_router'][l])
        x1, hin2, rt = _post(mix, x, g1, sc2, sh2, wout.astype(BF16), p['ln_g'][l, 0][None], p['ln_b'][l, 0][None],
                             wr_pad, br_pad, alpha=alpha, n_experts=n_experts, tt=tt)
        n_tok = bsz * t
        rt2 = rt.reshape(n_tok, LANES)
        moe_w = (p['w_gu'][l], p['b_gu'][l], p['w_down'][l], p['b_down'][l])
        if per_row:
            top_e = rt2[:, TOP_K:2 * TOP_K].astype(jnp.int32)
            gate_dense = jnp.zeros((n_tok, LANES), F32).at[jnp.arange(n_tok)[:, None], top_e].add(rt2[:, :TOP_K])
            ff = _moe_dense(hin2.reshape(n_tok, d), gate_dense, *moe_w).reshape(bsz, t, d)
            x = _final(x1, ff, rt, g2, p['ln_g'][l, 1][None], p['ln_b'][l, 1][None], alpha=alpha, combine=False, tt=tt)
        else:
            blk = max(8, min(MOE_ROWS, n_tok * TOP_K // n_experts))
            row_tok, block_e, block_valid, pos = _route(rt2, n_experts, blk)
            xs = jnp.concatenate([hin2.reshape(n_tok, d), jnp.zeros((1, d), hin2.dtype)])[row_tok]
            ys = _moe_grouped(block_e, block_valid, xs, *moe_w, tm=blk)
            yg = ys[pos.T].reshape(TOP_K, bsz, t, d)
            x = _final(x1, yg, rt, g2, p['ln_g'][l, 1][None], p['ln_b'][l, 1][None], alpha=alpha, combine=True, tt=tt)
    return x, jnp.stack(conv_new), jnp.stack(h_new), kv_state


def _qg_weight(w_in, n_kv, group, dh):
    d = w_in.shape[0]
    hd = n_kv * group * dh
    qw = group * dh
    wq = w_in[:, :hd].reshape(d, n_kv, qw)
    wg = w_in[:, hd:].reshape(d, n_kv, 3 * group)
    wg = jnp.pad(wg, ((0, 0), (0, 0), (0, LANES - 3 * group)))
    return jnp.concatenate([wq, wg], axis=-1).reshape(d, n_kv * (qw + LANES)).astype(BF16)


def kernel(x_prompt, x_sample, c_prompt, c_sample, state_conv, state_h, cache_kv, cache_win, page_table, w_ada, b_ada, ln_g, ln_b, w_in_a, w_conv, b_conv, w_gate_a, b_gate_a, lru_lambda, w_out_a, w_ada_kv, b_ada_kv, w_kv, cmp_w1, cmp_b1, cmp_w2, cmp_b2, cmp_pe, w_in_b, w_out_b, rel_table, w_router, b_router, w_gu, b_gu, w_down, b_down):
    p = dict(w_ada=w_ada, b_ada=b_ada, ln_g=ln_g, ln_b=ln_b, w_in_a=w_in_a, w_conv=w_conv, b_conv=b_conv,
             w_gate_a=w_gate_a, b_gate_a=b_gate_a, lru_lambda=lru_lambda, w_out_a=w_out_a, w_kv=w_kv,
             w_in_b=w_in_b, w_out_b=w_out_b, w_router=w_router, b_router=b_router, w_gu=w_gu, b_gu=b_gu,
             w_down=w_down, b_down=b_down)
    bsz, seq, d = x_prompt.shape
    dec_b = x_sample.shape[0]
    depth = w_ada.shape[0]
    n_a = w_in_a.shape[0]
    cw = w_conv.shape[1]
    lru_w = w_conv.shape[2]
    n_kv, dh = cache_kv.shape[3], cache_kv.shape[4]
    n_heads = rel_table.shape[1]
    group = n_heads // n_kv
    qw = group * dh
    n_parts = w_kv.shape[1] // (n_kv * dh)

    c_all = jnp.concatenate([c_prompt, c_sample], axis=0)
    mods = _linear(c_all, w_ada, b_ada[:, None, :], silu_in=True)
    kv_mod = _linear(c_all, w_ada_kv[None], b_ada_kv[None, None, :], silu_in=True)[0]
    tab = rel_table.astype(F32).reshape(-1)

    def cmp_kv(k, v):
        kc = _compress(k, cmp_w1[0], cmp_b1[0], cmp_w2[0], cmp_b2[0], cmp_pe[0])
        vc = _compress(v, cmp_w1[1], cmp_b1[1], cmp_w2[1], cmp_b2[1], cmp_pe[1])
        return kc, vc

    def make_prompt_attend(kv):
        kv6 = kv.reshape(bsz, seq, n_parts, n_kv, dh)
        kc, vc = cmp_kv(kv6[:, :, 0], kv6[:, :, 1])
        n_cmp = kc.shape[1]
        n_cmp_pad = -(-n_cmp // LANES) * LANES
        to_g = lambda a: jnp.pad(a, ((0, 0), (0, n_cmp_pad - n_cmp), (0, 0), (0, 0))).transpose(0, 2, 1, 3).astype(BF16)
        kc_g, vc_g = to_g(kc), to_g(vc)
        parts = kv6[:, :, 2:].transpose(2, 0, 3, 1, 4).astype(BF16)

        def attend(x, sc1, sh1, w_in):
            qg = _qproj(x, sc1, sh1, _qg_weight(w_in, n_kv, group, dh), n_kv=n_kv, qw=qw, tt=min(256, seq))
            return _nsa_seq(tab, qg, kc_g, vc_g, parts[0], parts[1], parts[2], parts[3], group=group, dh=dh,
                            cmp_block=cmp_pe.shape[1], n_cmp_real=n_cmp)

        n_win = min(WINDOW, seq)
        return attend, (kv6[:, :, :4], kv6[:, seq - n_win:, 4:])

    def make_sample_attend(kv):
        kv6 = kv.reshape(dec_b, 1, n_parts, n_kv, dh)
        past_len = page_table.shape[1] * cache_kv.shape[1]

        def full(u):
            past = cache_kv[page_table, :, u].reshape(dec_b, past_len, n_kv, dh)
            return jnp.concatenate([past, kv6[:, :, u]], axis=1)

        kc, vc = cmp_kv(full(0), full(1))
        ks, vs = full(2), full(3)
        n_buf = cache_win.shape[1]
        win = jnp.concatenate([cache_win, kv6[:, :, 4:]], axis=1)
        wpos = past_len - n_buf + jnp.arange(n_buf + 1)

        def attend(x, sc1, sh1, w_in):
            qg = _qproj(x, sc1, sh1, _qg_weight(w_in, n_kv, group, dh), n_kv=n_kv, qw=qw, tt=dec_b)[0]
            q = qg[:, :, :qw].transpose(1, 0, 2).reshape(dec_b, n_heads, dh)
            gates = qg[:, :, qw:qw + 3 * group].transpose(1, 0, 2).reshape(dec_b, n_heads, 3)
            o = _nsa_attend_single(q, gates, past_len, kc, vc, ks, vs, win[:, :, 0], win[:, :, 1], wpos, rel_table, n_kv,
                                   cmp_pe.shape[1])
            return o[None]

        return attend, (kv6[:, :, :4], win[:, -n_buf:])

    conv0 = jnp.zeros((n_a, bsz, cw - 1, lru_w), F32)
    h0 = jnp.zeros((n_a, bsz, lru_w), F32)
    y_prompt, prompt_conv, prompt_h, (prompt_kv, prompt_win) = _trunk(
        x_prompt, mods[:, :bsz], kv_mod[:bsz], conv0, h0, make_prompt_attend, p, per_row=False)
    y_sample, sample_conv, sample_h, (sample_kv, sample_win) = _trunk(
        x_sample.reshape(1, dec_b, d), mods[:, bsz:], kv_mod[bsz:], state_conv, state_h, make_sample_attend, p, per_row=True)
    return (y_prompt, y_sample.reshape(dec_b, 1, d), prompt_conv, prompt_h, prompt_kv, prompt_win,
            sample_conv, sample_h, sample_kv, sample_win)
```

```python
import functools
import math

import numpy as np
import jax
import jax.numpy as jnp
from jax import lax
from jax.experimental import pallas as pl
from jax.experimental.pallas import tpu as pltpu

F32 = jnp.float32
BF16 = jnp.bfloat16
HIGHEST = lax.Precision.HIGHEST
NT_DIMS = (((1,), (1,)), ((), ()))

CMP_STRIDE = 16
SEL_BLOCK = 64
N_SELECT = 16
WINDOW = 512
REL_MAX_DIST = 128
TOP_K = 4
RGLRU_C = 8.0
SWIGLU_LIMIT = 7.0
SWIGLU_ALPHA = 1.702
MOE_ROWS = 256
LN_EPS = 1e-5
SEL_BIG = 1e9
NEG_INF = -1e30
M_INIT = -1e29

LANES = 128
SUBLANES_BF16 = 16
VMEM_LIMIT = 48 * 1024 * 1024
NSA_TILE = 256


def _cp(sem, vmem=VMEM_LIMIT):
    return pltpu.CompilerParams(dimension_semantics=sem, vmem_limit_bytes=vmem)


def _rel_thresholds(n_buckets):
    exact = n_buckets // 2
    d_max = REL_MAX_DIST + 1
    buckets = []
    for d in range(d_max + 1):
        if d < exact:
            buckets.append(d)
        else:
            large = exact + int(math.log(max(d, 1) / exact) / math.log(REL_MAX_DIST / exact) * (n_buckets - exact))
            buckets.append(min(large, n_buckets - 1))
    return [next(d for d in range(d_max + 1) if buckets[d] >= b) for b in range(n_buckets)]


def _linear_kernel(x_ref, w_ref, b_ref, o_ref, *, silu_in):
    x = x_ref[...]
    if silu_in:
        x = x * jax.nn.sigmoid(x)
    y = jnp.dot(x.astype(BF16), w_ref[0].astype(BF16), preferred_element_type=F32)
    o_ref[0] = y + b_ref[0]


def _linear(x, w, b, *, silu_in=False, tn=512):
    m, k = x.shape
    nl, _, n = w.shape
    tn = min(tn, n)
    return pl.pallas_call(
        functools.partial(_linear_kernel, silu_in=silu_in),
        grid=(nl, n // tn),
        in_specs=[
            pl.BlockSpec((m, k), lambda l, j: (0, 0)),
            pl.BlockSpec((1, k, tn), lambda l, j: (l, 0, j)),
            pl.BlockSpec((1, 1, tn), lambda l, j: (l, 0, j)),
        ],
        out_specs=pl.BlockSpec((1, m, tn), lambda l, j: (l, 0, j)),
        out_shape=jax.ShapeDtypeStruct((nl, m, n), F32),
        compiler_params=_cp(("parallel", "parallel")),
        name="linear",
    )(x, w, b)


def _mod_spec(mod, tt):
    d = mod.shape[-1]
    if mod.shape[1] == 1:
        return pl.BlockSpec((1, 1, d), lambda b, i: (b, 0, 0))
    return pl.BlockSpec((1, tt, d), lambda b, i: (b, i, 0))


def _modlinear_kernel(x_ref, sc_ref, sh_ref, w_ref, o_ref):
    hin = x_ref[0] * (1.0 + sc_ref[0]) + sh_ref[0]
    o_ref[0] = jnp.dot(hin.astype(BF16), w_ref[...], preferred_element_type=F32)


def _modlinear(x, sc, sh, w_bf, *, tt):
    bsz, t, d = x.shape
    n = w_bf.shape[1]
    return pl.pallas_call(
        _modlinear_kernel,
        grid=(bsz, t // tt),
        in_specs=[
            pl.BlockSpec((1, tt, d), lambda b, i: (b, i, 0)),
            _mod_spec(sc, tt),
            _mod_spec(sh, tt),
            pl.BlockSpec((d, n), lambda b, i: (0, 0)),
        ],
        out_specs=pl.BlockSpec((1, tt, n), lambda b, i: (b, i, 0)),
        out_shape=jax.ShapeDtypeStruct((bsz, t, n), F32),
        compiler_params=_cp(("parallel", "parallel")),
        name="modlinear",
    )(x, sc, sh, w_bf)


def _kvproj_kernel(x_ref, sc_ref, sh_ref, w_ref, wvt_ref, kv_ref, kk_ref, vt_ref, *, n_kv, dh, k_parts):
    hin = (x_ref[0] * (1.0 + sc_ref[0]) + sh_ref[0]).astype(BF16)
    y = jnp.dot(hin, w_ref[...], preferred_element_type=F32)
    kv_ref[0] = y
    gd = n_kv * dh
    for j, part in enumerate(k_parts):
        for g in range(n_kv):
            kk_ref[j, 0, g] = y[:, part * gd + g * dh:part * gd + (g + 1) * dh].astype(BF16)
    yt = lax.dot_general(wvt_ref[...], hin, NT_DIMS, preferred_element_type=F32)
    for j in range(vt_ref.shape[1]):
        vt_ref[0, j, 0] = yt[j * dh:(j + 1) * dh].astype(BF16)


def _kvproj(x, sc, sh, w_bf, wvt_bf, *, n_kv, dh, k_parts, tt):
    bsz, t, d = x.shape
    n = w_bf.shape[1]
    nv = wvt_bf.shape[0] // dh
    return pl.pallas_call(
        functools.partial(_kvproj_kernel, n_kv=n_kv, dh=dh, k_parts=k_parts),
        grid=(bsz, t // tt),
        in_specs=[
            pl.BlockSpec((1, tt, d), lambda b, i: (b, i, 0)),
            _mod_spec(sc, tt),
            _mod_spec(sh, tt),
            pl.BlockSpec((d, n), lambda b, i: (0, 0)),
            pl.BlockSpec((nv * dh, d), lambda b, i: (0, 0)),
        ],
        out_specs=[
            pl.BlockSpec((1, tt, n), lambda b, i: (b, i, 0)),
            pl.BlockSpec((len(k_parts), 1, n_kv, tt, dh), lambda b, i: (0, b, 0, i, 0)),
            pl.BlockSpec((1, nv, 1, dh, tt), lambda b, i: (b, 0, i, 0, 0)),
        ],
        out_shape=[
            jax.ShapeDtypeStruct((bsz, t, n), F32),
            jax.ShapeDtypeStruct((len(k_parts), bsz, n_kv, t, dh), BF16),
            jax.ShapeDtypeStruct((bsz, nv, t // tt, dh, tt), BF16),
        ],
        compiler_params=_cp(("parallel", "parallel")),
        name="kvproj",
    )(x, sc, sh, w_bf, wvt_bf)


def _qproj_kernel(x_ref, sc_ref, sh_ref, wt_ref, qt_ref, gt_ref, *, hd, scale):
    hin = (x_ref[0] * (1.0 + sc_ref[0]) + sh_ref[0]).astype(BF16)
    yt = lax.dot_general(wt_ref[...], hin, NT_DIMS, preferred_element_type=F32)
    qt_ref[0] = (yt[:hd] * scale).astype(BF16)
    gt_ref[0] = jax.nn.sigmoid(yt[hd:])


def _qproj(x, sc, sh, wt_bf, *, hd, scale, tt):
    bsz, t, d = x.shape
    n = wt_bf.shape[0]
    return pl.pallas_call(
        functools.partial(_qproj_kernel, hd=hd, scale=scale),
        grid=(bsz, t // tt),
        in_specs=[
            pl.BlockSpec((1, tt, d), lambda b, i: (b, i, 0)),
            _mod_spec(sc, tt),
            _mod_spec(sh, tt),
            pl.BlockSpec((n, d), lambda b, i: (0, 0)),
        ],
        out_specs=[
            pl.BlockSpec((1, hd, tt), lambda b, i: (b, 0, i)),
            pl.BlockSpec((1, n - hd, tt), lambda b, i: (b, 0, i)),
        ],
        out_shape=[
            jax.ShapeDtypeStruct((bsz, hd, t), BF16),
            jax.ShapeDtypeStruct((bsz, n - hd, t), F32),
        ],
        compiler_params=_cp(("parallel", "parallel")),
        name="qproj",
    )(x, sc, sh, wt_bf)


def _log1p(x):
    u = 1.0 + x
    return jnp.where(u == 1.0, x, jnp.log(u) * x / jnp.where(u == 1.0, 1.0, u - 1.0))


def _expm1(x):
    u = jnp.exp(x)
    safe = (u != 1.0) & (u > 0.0)
    return jnp.where(u == 1.0, x, jnp.where(u > 0.0, (u - 1.0) * x / jnp.where(safe, jnp.log(u), 1.0), -1.0))


def _softplus(z):
    return jnp.maximum(z, 0.0) + _log1p(jnp.exp(-jnp.abs(z)))


def _rglru_gates(xc, wg_ref, bg_ref, lam_ref, n_heads):
    w = xc.shape[1]
    bw = w // n_heads
    xcb = xc.astype(BF16)
    gates = []
    for g in range(2):
        cols = [jnp.dot(xcb[:, i * bw:(i + 1) * bw], wg_ref[g, i], preferred_element_type=F32) for i in range(n_heads)]
        gates.append(jnp.concatenate(cols, axis=1) + bg_ref[g:g + 1, :])
    r = jax.nn.sigmoid(gates[0])
    i_g = jax.nn.sigmoid(gates[1])
    log_a = (-RGLRU_C * _softplus(-lam_ref[...])) * r
    a = jnp.exp(log_a)
    b_in = jnp.sqrt(-_expm1(2.0 * log_a)) * (i_g * xc)
    return a, b_in


def _rglru_seq_kernel(x_ref, sc_ref, sh_ref, conv0_ref, h0_ref, win_ref, wconv_ref, bconv_ref, wg_ref, bg_ref, lam_ref,
                      gated_ref, convout_ref, hout_ref, ext_ref, a_ref, b_ref, yb_ref, h_ref, *, tt, n_heads, cw):
    t = pl.program_id(1)
    w = a_ref.shape[1]
    pad = 8

    @pl.when(t == 0)
    def _():
        ext_ref[0:pad - (cw - 1), :] = jnp.zeros((pad - (cw - 1), w), F32)
        ext_ref[pad - (cw - 1):pad, :] = conv0_ref[0]
        h_ref[...] = h0_ref[0]

    hin = x_ref[0] * (1.0 + sc_ref[0]) + sh_ref[0]
    u = jnp.dot(hin.astype(BF16), win_ref[...], preferred_element_type=F32)
    yb_ref[...] = jax.nn.gelu(u[:, :w])
    ext_ref[pad:pad + tt, :] = u[:, w:]
    xc = bconv_ref[...] + sum(ext_ref[pad - (cw - 1) + k:pad - (cw - 1) + k + tt, :] * wconv_ref[k:k + 1, :] for k in range(cw))
    tail = ext_ref[tt:tt + pad, :]
    ext_ref[0:pad, :] = tail

    a, b_in = _rglru_gates(xc, wg_ref, bg_ref, lam_ref, n_heads)
    a_ref[...] = a
    b_ref[...] = b_in

    row = lax.broadcasted_iota(jnp.int32, (tt, LANES), 0)
    for c in range(w // LANES):
        cs = slice(c * LANES, (c + 1) * LANES)
        av = a_ref[:, cs]
        bv = b_ref[:, cs]
        s = 1
        while s < tt:
            keep = row >= s
            b_sh = jnp.where(keep, pltpu.roll(bv, s, 0), 0.0)
            a_sh = jnp.where(keep, pltpu.roll(av, s, 0), 1.0)
            bv = av * b_sh + bv
            av = av * a_sh
            s *= 2
        hs = av * h_ref[:, cs] + bv
        h_ref[:, cs] = hs[tt - 1:tt, :]
        gated_ref[0, :, cs] = (hs * yb_ref[:, cs]).astype(gated_ref.dtype)

    @pl.when(t == pl.num_programs(1) - 1)
    def _():
        convout_ref[0] = tail[pad - (cw - 1):pad, :]
        hout_ref[0] = h_ref[...]


def _rglru_seq(x, sc, sh, conv0, h0, win_bf, wconv, bconv, wg_bf, bg, lam, *, tt):
    bsz, t, d = x.shape
    cw, w = wconv.shape
    n_heads = wg_bf.shape[1]
    full = lambda shape: pl.BlockSpec(shape, lambda b, i: (0,) * len(shape))
    return pl.pallas_call(
        functools.partial(_rglru_seq_kernel, tt=tt, n_heads=n_heads, cw=cw),
        grid=(bsz, t // tt),
        in_specs=[
            pl.BlockSpec((1, tt, d), lambda b, i: (b, i, 0)),
            _mod_spec(sc, tt),
            _mod_spec(sh, tt),
            pl.BlockSpec((1, cw - 1, w), lambda b, i: (b, 0, 0)),
            pl.BlockSpec((1, 1, w), lambda b, i: (b, 0, 0)),
            full((d, 2 * w)),
            full((cw, w)),
            full((1, w)),
            full(wg_bf.shape),
            full((2, w)),
            full((1, w)),
        ],
        out_specs=[
            pl.BlockSpec((1, tt, w), lambda b, i: (b, i, 0)),
            pl.BlockSpec((1, cw - 1, w), lambda b, i: (b, 0, 0)),
            pl.BlockSpec((1, 1, w), lambda b, i: (b, 0, 0)),
        ],
        out_shape=[
            jax.ShapeDtypeStruct((bsz, t, w), BF16),
            jax.ShapeDtypeStruct((bsz, cw - 1, w), F32),
            jax.ShapeDtypeStruct((bsz, 1, w), F32),
        ],
        scratch_shapes=[
            pltpu.VMEM((tt + 8, w), F32),
            pltpu.VMEM((tt, w), F32),
            pltpu.VMEM((tt, w), F32),
            pltpu.VMEM((tt, w), F32),
            pltpu.VMEM((1, w), F32),
        ],
        compiler_params=_cp(("parallel", "arbitrary")),
        name="rglru_seq",
    )(x, sc, sh, conv0, h0, win_bf, wconv, bconv, wg_bf, bg, lam)


def _rglru_step_kernel(x_ref, sc_ref, sh_ref, conv_ref, h0_ref, win_ref, wconv_ref, bconv_ref, wg_ref, bg_ref, lam_ref,
                       gated_ref, convout_ref, hout_ref, *, n_heads, cw):
    w = h0_ref.shape[1]
    hin = x_ref[...] * (1.0 + sc_ref[...]) + sh_ref[...]
    u = jnp.dot(hin.astype(BF16), win_ref[...], preferred_element_type=F32)
    yb = jax.nn.gelu(u[:, :w])
    xb = u[:, w:]
    taps = [conv_ref[k] for k in range(cw - 1)] + [xb]
    xc = bconv_ref[...] + sum(taps[k] * wconv_ref[k:k + 1, :] for k in range(cw))
    a, b_in = _rglru_gates(xc, wg_ref, bg_ref, lam_ref, n_heads)
    h = a * h0_ref[...] + b_in
    gated_ref[...] = (h * yb).astype(gated_ref.dtype)
    for k in range(cw - 1):
        convout_ref[k] = taps[k + 1]
    hout_ref[...] = h


def _rglru_step(x, sc, sh, conv, h0, win_bf, wconv, bconv, wg_bf, bg, lam):
    r, _ = x.shape
    cw, w = wconv.shape
    n_heads = wg_bf.shape[1]
    return pl.pallas_call(
        functools.partial(_rglru_step_kernel, n_heads=n_heads, cw=cw),
        out_shape=[
            jax.ShapeDtypeStruct((r, w), BF16),
            jax.ShapeDtypeStruct((cw - 1, r, w), F32),
            jax.ShapeDtypeStruct((r, w), F32),
        ],
        compiler_params=pltpu.CompilerParams(vmem_limit_bytes=VMEM_LIMIT),
        name="rglru_step",
    )(x, sc, sh, conv, h0, win_bf, wconv, bconv, wg_bf, bg, lam)


def _layer_norm(v, g, b):
    mu = jnp.mean(v, axis=-1, keepdims=True)
    dv = v - mu
    var = jnp.mean(dv * dv, axis=-1, keepdims=True)
    return dv * lax.rsqrt(var + LN_EPS) * g + b


def _post_kernel(a_ref, x_ref, g1_ref, sc2_ref, sh2_ref, wout_ref, lng_ref, lnb_ref, wr_ref, br_ref,
                 x1_ref, hin2_ref, rt_ref, *, alpha, n_experts):
    y = jnp.dot(a_ref[0].astype(BF16), wout_ref[...], preferred_element_type=F32)
    x1 = _layer_norm(alpha * x_ref[0] + (1.0 + g1_ref[0]) * y, lng_ref[...], lnb_ref[...])
    x1_ref[0] = x1
    hin2 = x1 * (1.0 + sc2_ref[0]) + sh2_ref[0]
    hin2_ref[0] = hin2.astype(hin2_ref.dtype)
    logits = jnp.dot(hin2, wr_ref[...], preferred_element_type=F32, precision=HIGHEST) + br_ref[...]
    lane = lax.broadcasted_iota(jnp.int32, logits.shape, 1)
    work = jnp.where(lane < n_experts, logits, -jnp.inf)
    vals, idxs = [], []
    for _ in range(TOP_K):
        m = jnp.max(work, axis=-1, keepdims=True)
        idx = jnp.min(jnp.where(work == m, lane, LANES), axis=-1, keepdims=True)
        vals.append(m)
        idxs.append(idx)
        work = jnp.where(lane == idx, -jnp.inf, work)
    es = [jnp.exp(v - vals[0]) for v in vals]
    z = sum(es)
    rt = jnp.zeros(logits.shape, F32)
    for k in range(TOP_K):
        rt = jnp.where(lane == k, es[k] / z, rt)
        rt = jnp.where(lane == TOP_K + k, idxs[k].astype(F32), rt)
    rt_ref[0] = rt


def _post(a, x, g1, sc2, sh2, wout_bf, lng, lnb, wr_pad, br_pad, *, alpha, n_experts, tt):
    bsz, t, d = x.shape
    dk = a.shape[-1]
    full = lambda shape: pl.BlockSpec(shape, lambda b, i: (0,) * len(shape))
    row = pl.BlockSpec((1, tt, d), lambda b, i: (b, i, 0))
    return pl.pallas_call(
        functools.partial(_post_kernel, alpha=alpha, n_experts=n_experts),
        grid=(bsz, t // tt),
        in_specs=[
            pl.BlockSpec((1, tt, dk), lambda b, i: (b, i, 0)),
            row,
            _mod_spec(g1, tt),
            _mod_spec(sc2, tt),
            _mod_spec(sh2, tt),
            full((dk, d)),
            full((1, d)),
            full((1, d)),
            full((d, LANES)),
            full((1, LANES)),
        ],
        out_specs=[row, row, pl.BlockSpec((1, tt, LANES), lambda b, i: (b, i, 0))],
        out_shape=[
            jax.ShapeDtypeStruct((bsz, t, d), F32),
            jax.ShapeDtypeStruct((bsz, t, d), BF16),
            jax.ShapeDtypeStruct((bsz, t, LANES), F32),
        ],
        compiler_params=_cp(("parallel", "parallel")),
        name="post",
    )(a, x, g1, sc2, sh2, wout_bf, lng, lnb, wr_pad, br_pad)


def _swiglu(g, u):
    g = jnp.minimum(g, SWIGLU_LIMIT)
    u = jnp.clip(u, -SWIGLU_LIMIT, SWIGLU_LIMIT)
    return g * jax.nn.sigmoid(SWIGLU_ALPHA * g) * (u + 1.0)


def _moe_grouped_kernel(vb_ref, ve_ref, lo_ref, hi_ref, xs_ref, wgu_ref, bgu_ref, wdn_ref, bdn_ref, ys_ref,
                        wgu_bf, wdn_bf, *, chunk, tm):
    v = pl.program_id(0)
    d, de2 = wgu_bf.shape
    de = de2 // 2
    e = ve_ref[v]
    blk = vb_ref[v]
    pv = jnp.maximum(v - 1, 0)

    @pl.when((v == 0) | (e != ve_ref[pv]))
    def _():
        for c in range(de2 // chunk):
            wgu_bf[:, c * chunk:(c + 1) * chunk] = wgu_ref[0, :, c * chunk:(c + 1) * chunk].astype(BF16)
        for c in range(de // chunk):
            wdn_bf[c * chunk:(c + 1) * chunk, :] = wdn_ref[0, c * chunk:(c + 1) * chunk, :].astype(BF16)

    @pl.when((v == 0) | (blk != vb_ref[pv]))
    def _():
        ys_ref[...] = jnp.zeros(ys_ref.shape, F32)

    lo = lo_ref[v]
    hi = hi_ref[v]

    @pl.when(hi > lo)
    def _():
        x = xs_ref[...]
        acc = jnp.zeros(ys_ref.shape, F32) + bdn_ref[0]
        for c in range(de // chunk):
            g = jnp.dot(x, wgu_bf[:, c * chunk:(c + 1) * chunk], preferred_element_type=F32) + bgu_ref[0, :, c * chunk:(c + 1) * chunk]
            u = jnp.dot(x, wgu_bf[:, de + c * chunk:de + (c + 1) * chunk], preferred_element_type=F32) + bgu_ref[0, :, de + c * chunk:de + (c + 1) * chunk]
            h = _swiglu(g, u).astype(BF16)
            acc = acc + jnp.dot(h, wdn_bf[c * chunk:(c + 1) * chunk, :], preferred_element_type=F32)
        row = blk * tm + lax.broadcasted_iota(jnp.int32, (tm, 1), 0)
        ys_ref[...] += jnp.where((row >= lo) & (row < hi), acc, 0.0)


def _moe_grouped(visits, xs, w_gu, b_gu, w_dn, b_dn, *, tm):
    n_rows, d = xs.shape
    n_e, _, de2 = w_gu.shape
    de = de2 // 2
    n_vis = visits[0].shape[0]
    grid_spec = pltpu.PrefetchScalarGridSpec(
        num_scalar_prefetch=4,
        grid=(n_vis,),
        in_specs=[
            pl.BlockSpec((tm, d), lambda v, vb, ve, lo, hi: (vb[v], 0)),
            pl.BlockSpec((1, d, de2), lambda v, vb, ve, lo, hi: (ve[v], 0, 0)),
            pl.BlockSpec((1, 1, de2), lambda v, vb, ve, lo, hi: (ve[v], 0, 0)),
            pl.BlockSpec((1, de, d), lambda v, vb, ve, lo, hi: (ve[v], 0, 0)),
            pl.BlockSpec((1, 1, d), lambda v, vb, ve, lo, hi: (ve[v], 0, 0)),
        ],
        out_specs=pl.BlockSpec((tm, d), lambda v, vb, ve, lo, hi: (vb[v], 0)),
        scratch_shapes=[pltpu.VMEM((d, de2), BF16), pltpu.VMEM((de, d), BF16)],
    )
    return pl.pallas_call(
        functools.partial(_moe_grouped_kernel, chunk=min(512, de), tm=tm),
        grid_spec=grid_spec,
        out_shape=jax.ShapeDtypeStruct((n_rows, d), F32),
        compiler_params=_cp(("arbitrary",), 56 * 1024 * 1024),
        name="moe_grouped",
    )(*visits, xs, w_gu, b_gu.reshape(n_e, 1, de2), w_dn, b_dn.reshape(n_e, 1, d))


def _moe_dense_kernel(x_ref, rt_ref, wg_ref, wu_ref, bg_ref, bu_ref, wdn_ref, bdn_ref, o_ref):
    e = pl.program_id(0)
    c = pl.program_id(1)

    @pl.when((e == 0) & (c == 0))
    def _():
        o_ref[...] = jnp.zeros(o_ref.shape, F32)

    rt = rt_ref[...]
    ef = e.astype(F32)
    gate = sum(jnp.where(rt[:, TOP_K + k:TOP_K + k + 1] == ef, rt[:, k:k + 1], 0.0) for k in range(TOP_K))
    x = x_ref[...]
    g = jnp.dot(x, wg_ref[0].astype(BF16), preferred_element_type=F32) + bg_ref[0]
    u = jnp.dot(x, wu_ref[0].astype(BF16), preferred_element_type=F32) + bu_ref[0]
    h = _swiglu(g, u).astype(BF16)
    y = jnp.dot(h, wdn_ref[0].astype(BF16), preferred_element_type=F32)
    y = y + jnp.where(c == 0, 1.0, 0.0) * bdn_ref[0]
    o_ref[...] += gate * y


def _moe_dense(x_bf, rt, w_gu, b_gu, w_dn, b_dn, *, chunk=256):
    r, d = x_bf.shape
    n_e, _, de2 = w_gu.shape
    de = de2 // 2
    nc = de // chunk
    b_gu3 = b_gu.reshape(n_e, 1, de2)
    return pl.pallas_call(
        _moe_dense_kernel,
        grid=(n_e, nc),
        in_specs=[
            pl.BlockSpec((r, d), lambda e, c: (0, 0)),
            pl.BlockSpec((r, LANES), lambda e, c: (0, 0)),
            pl.BlockSpec((1, d, chunk), lambda e, c: (e, 0, c)),
            pl.BlockSpec((1, d, chunk), lambda e, c: (e, 0, nc + c)),
            pl.BlockSpec((1, 1, chunk), lambda e, c: (e, 0, c)),
            pl.BlockSpec((1, 1, chunk), lambda e, c: (e, 0, nc + c)),
            pl.BlockSpec((1, chunk, d), lambda e, c: (e, c, 0)),
            pl.BlockSpec((1, 1, d), lambda e, c: (e, 0, 0)),
        ],
        out_specs=pl.BlockSpec((r, d), lambda e, c: (0, 0)),
        out_shape=jax.ShapeDtypeStruct((r, d), F32),
        compiler_params=_cp(("arbitrary", "arbitrary")),
        name="moe_dense",
    )(x_bf, rt, w_gu, w_gu, b_gu3, b_gu3, w_dn, b_dn.reshape(n_e, 1, d))


def _final_kernel(x_ref, ff_ref, rt_ref, g2_ref, lng_ref, lnb_ref, o_ref, *, alpha, combine):
    if combine:
        rt = rt_ref[0]
        ff = sum(rt[:, k:k + 1] * ff_ref[k, 0] for k in range(TOP_K))
    else:
        ff = ff_ref[0]
    o_ref[0] = _layer_norm(alpha * x_ref[0] + (1.0 + g2_ref[0]) * ff, lng_ref[...], lnb_ref[...])


def _final(x1, ff, rt, g2, lng, lnb, *, alpha, combine, tt):
    bsz, t, d = x1.shape
    row = pl.BlockSpec((1, tt, d), lambda b, i: (b, i, 0))
    ff_spec = pl.BlockSpec((TOP_K, 1, tt, d), lambda b, i: (0, b, i, 0)) if combine else row
    full = lambda shape: pl.BlockSpec(shape, lambda b, i: (0,) * len(shape))
    return pl.pallas_call(
        functools.partial(_final_kernel, alpha=alpha, combine=combine),
        grid=(bsz, t // tt),
        in_specs=[row, ff_spec, pl.BlockSpec((1, tt, LANES), lambda b, i: (b, i, 0)), _mod_spec(g2, tt), full((1, d)), full((1, d))],
        out_specs=row,
        out_shape=jax.ShapeDtypeStruct((bsz, t, d), F32),
        compiler_params=_cp(("parallel", "parallel")),
        name="final",
    )(x1, ff, rt, g2, lng, lnb)


def _bias_chain(dist, tab_ref, h0, n_heads_tab, group, thr):
    out = [jnp.full(dist.shape, tab_ref[h0 + r], F32) for r in range(group)]
    for b in range(1, len(thr)):
        ind = dist >= thr[b]
        out = [jnp.where(ind, tab_ref[b * n_heads_tab + h0 + r], out[r]) for r in range(group)]
    return out


def _nsa_seq_kernel(tab_ref, qt_ref, gt_ref, kc_ref, vct_ref, ks_ref, vst_ref, kw_ref, vwt_ref, toe_ref, o_ref,
                    sel_ref, flag_ref, ms_ref, accs_ref, mw_ref, accw_ref,
                    *, tq, group, dh, n_heads, thr, cmp_block, n_cmp_real, n_blk):
    g = pl.program_id(1)
    it = pl.program_id(2)
    q0 = it * tq
    h0 = g * group
    n_bkt = len(thr)
    far = thr[-1]
    bpt = tq // SEL_BLOCK
    n_tiles = ks_ref.shape[0] // tq
    assert far <= tq + 1 and WINDOW % tq == 0 and WINDOW >= 2 * tq - 1

    def lanes(parts):
        return jnp.concatenate(parts, axis=1)

    q4t = lanes([qt_ref[0, r * dh:(r + 1) * dh, :] for r in range(group)])
    far_bias = lanes([jnp.full((1, tq), tab_ref[(n_bkt - 1) * n_heads + h0 + r], F32) for r in range(group)])
    qi = lax.broadcasted_iota(jnp.int32, (1, tq), 1)
    ki = lax.broadcasted_iota(jnp.int32, (tq, 1), 0)
    qpos = q0 + qi

    ncp = kc_ref.shape[0]
    cidx = lax.broadcasted_iota(jnp.int32, (ncp, 1), 0)
    dist_c = qpos - (cidx * CMP_STRIDE + (cmp_block - 1))
    madd_c = jnp.where((dist_c >= 0) & (cidx < n_cmp_real), 0.0, NEG_INF)
    bias_c = _bias_chain(dist_c, tab_ref, h0, n_heads, group, thr)
    s_c = jnp.dot(kc_ref[...], q4t, preferred_element_type=F32) + lanes([b + madd_c for b in bias_c])
    m_c = jnp.maximum(jnp.max(s_c, axis=0, keepdims=True), M_INIT)
    e_c = jnp.exp(s_c - m_c)
    z_c = jnp.sum(e_c, axis=0, keepdims=True)
    p_c = e_c / jnp.where(z_c > 0, z_c, 1.0)
    o_c = jnp.dot(vct_ref[...], p_c.astype(BF16), preferred_element_type=F32)

    psum = sum(p_c[:, r * tq:(r + 1) * tq] for r in range(group))
    bj = lax.broadcasted_iota(jnp.int32, (n_blk, ncp), 0) * SEL_BLOCK
    ci = lax.broadcasted_iota(jnp.int32, (n_blk, ncp), 1) * CMP_STRIDE
    overlap_t = jnp.where((ci < bj + SEL_BLOCK) & (ci + cmp_block > bj), 1.0, 0.0)
    imp = jnp.dot(overlap_t, psum, preferred_element_type=F32, precision=HIGHEST)
    blk = lax.broadcasted_iota(jnp.int32, (n_blk, tq), 0)
    cur = qpos // SEL_BLOCK
    forced = (blk == 0) | ((blk >= cur - 1) & (blk <= cur))
    valid = blk * SEL_BLOCK <= qpos
    score = jnp.where(forced, SEL_BIG, jnp.where(valid, imp, -SEL_BIG))
    sel = jnp.full((n_blk, tq), NEG_INF, F32)
    for _ in range(min(N_SELECT, n_blk)):
        mx = jnp.max(score, axis=0, keepdims=True)
        idx = jnp.min(jnp.where(score == mx, blk, n_blk), axis=0, keepdims=True)
        hit = blk == idx
        sel = jnp.where(hit, 0.0, sel)
        score = jnp.where(hit, -jnp.inf, score)
    sel_ref[...] = sel
    for t in range(n_tiles):
        flag_ref[t] = jnp.max(sel[t * bpt:(t + 1) * bpt, :])

    ms_ref[...] = jnp.full(ms_ref.shape, M_INIT, F32)
    mw_ref[...] = jnp.full(mw_ref.shape, M_INIT, F32)
    accs_ref[...] = jnp.zeros(accs_ref.shape, F32)
    accw_ref[...] = jnp.zeros(accw_ref.shape, F32)
    ones = jnp.ones((SUBLANES_BF16, tq), BF16)

    def sel_madd(t):
        b0 = t * bpt
        return jnp.concatenate([jnp.broadcast_to(sel_ref[pl.ds(b0 + j, 1), :], (SEL_BLOCK, tq)) for j in range(bpt)], axis=0)

    def toeplitz(kind):
        return lanes([toe_ref[r, kind] for r in range(group)])

    def attend(k_ref, vt_ref, t, bias, madd, m_ref, acc_ref):
        kt = k_ref[pl.ds(pl.multiple_of(t * tq, tq), tq), :]
        s = jnp.dot(kt, q4t, preferred_element_type=F32) + bias
        if madd is not None:
            s = s + lanes([madd] * group)
        m_old = m_ref[...]
        m_new = jnp.maximum(m_old, jnp.max(s, axis=0, keepdims=True))
        e = jnp.exp(s - m_new).astype(BF16)
        vt = jnp.concatenate([vt_ref[t], ones], axis=0)
        acc_ref[...] = jnp.exp(m_old - m_new) * acc_ref[...] + jnp.dot(vt, e, preferred_element_type=F32)
        m_ref[...] = m_new

    causal = jnp.where(qi >= ki, 0.0, NEG_INF)

    def far_body(t, carry):
        @pl.when(flag_ref[t] > -1.0)
        def _():
            attend(ks_ref, vst_ref, t, far_bias, sel_madd(t), ms_ref, accs_ref)
        return carry

    lax.fori_loop(0, jnp.maximum(it - 1, 0), far_body, 0)

    t_prev = jnp.maximum(it - 1, 0)
    kill_prev = jnp.where(it >= 1, 0.0, NEG_INF)
    attend(ks_ref, vst_ref, t_prev, toeplitz(1), sel_madd(t_prev) + kill_prev, ms_ref, accs_ref)
    attend(ks_ref, vst_ref, it, toeplitz(0), sel_madd(it) + causal, ms_ref, accs_ref)

    for j in range(WINDOW // tq + 1):
        off = WINDOW - j * tq
        t_w = jnp.maximum(it - off // tq, 0)
        madd = None
        if off == 0:
            bias, madd = toeplitz(0), causal
        elif off == tq:
            bias = toeplitz(1)
        else:
            bias = far_bias
            if off + tq - 1 > WINDOW:
                madd = jnp.where(off + qi - ki <= WINDOW, 0.0, NEG_INF)
        if off > 0:
            kill = jnp.where(it >= off // tq, 0.0, NEG_INF)
            madd = kill if madd is None else madd + kill
            if madd.shape != (tq, tq):
                madd = jnp.broadcast_to(madd, (tq, tq))
        attend(kw_ref, vwt_ref, t_w, bias, madd, mw_ref, accw_ref)

    def finish(acc_ref):
        acc = acc_ref[...]
        l = acc[dh:dh + 1, :]
        return acc[:dh, :] / jnp.where(l > 0, l, 1.0)

    def gate(j):
        return lanes([gt_ref[0, r * 3 + j:r * 3 + j + 1, :] for r in range(group)])

    ot = gate(0) * o_c + gate(1) * finish(accs_ref) + gate(2) * finish(accw_ref)
    heads_per_store = LANES // dh
    for r in range(0, group, heads_per_store):
        stacked = jnp.concatenate([ot[:, (r + u) * tq:(r + u + 1) * tq] for u in range(heads_per_store)], axis=0)
        o_ref[0, :, r * dh:(r + heads_per_store) * dh] = stacked.T


def _nsa_seq(tab, qt, gt, kc, vct, kk, vt, toe, *, group, dh, cmp_block, n_cmp_real, tq):
    bsz, hd, s = qt.shape
    n_kv = hd // (group * dh)
    n_heads = n_kv * group
    ncp = kc.shape[2]
    n_tiles = s // tq
    n_blk = -(-s // SEL_BLOCK // 8) * 8
    thr = tuple(_rel_thresholds(tab.shape[0] // n_heads))
    rows = group * tq
    gpad = gt.shape[1] // n_kv
    assert s % tq == 0 and tq % SEL_BLOCK == 0 and dh * (LANES // dh) == LANES and group % (LANES // dh) == 0
    sq = lambda *dims: pl.BlockSpec(tuple(None if d == 0 else d for d in dims[:-1]), dims[-1])
    kern = functools.partial(_nsa_seq_kernel, tq=tq, group=group, dh=dh, n_heads=n_heads, thr=thr, cmp_block=cmp_block,
                             n_cmp_real=n_cmp_real, n_blk=n_blk)
    return pl.pallas_call(
        kern,
        grid=(bsz, n_kv, n_tiles),
        in_specs=[
            pl.BlockSpec(memory_space=pltpu.SMEM),
            pl.BlockSpec((1, group * dh, tq), lambda b, g, i: (b, g, i)),
            pl.BlockSpec((1, gpad, tq), lambda b, g, i: (b, g, i)),
            pl.BlockSpec((None, None, ncp, dh), lambda b, g, i: (b, g, 0, 0)),
            pl.BlockSpec((None, None, dh, ncp), lambda b, g, i: (b, g, 0, 0)),
            pl.BlockSpec((None, None, None, s, dh), lambda b, g, i: (0, b, g, 0, 0)),
            pl.BlockSpec((None, None, n_tiles, dh, tq), lambda b, g, i: (b, g, 0, 0, 0)),
            pl.BlockSpec((None, None, None, s, dh), lambda b, g, i: (1, b, g, 0, 0)),
            pl.BlockSpec((None, None, n_tiles, dh, tq), lambda b, g, i: (b, n_kv + g, 0, 0, 0)),
            pl.BlockSpec((group, 2, tq, tq), lambda b, g, i: (g, 0, 0, 0)),
        ],
        out_specs=pl.BlockSpec((1, tq, group * dh), lambda b, g, i: (b, i, g)),
        out_shape=jax.ShapeDtypeStruct((bsz, s, hd), F32),
        scratch_shapes=[
            pltpu.VMEM((n_blk, tq), F32),
            pltpu.SMEM((n_tiles,), F32),
            pltpu.VMEM((1, rows), F32),
            pltpu.VMEM((dh + SUBLANES_BF16, rows), F32),
            pltpu.VMEM((1, rows), F32),
            pltpu.VMEM((dh + SUBLANES_BF16, rows), F32),
        ],
        compiler_params=_cp(("parallel", "parallel", "arbitrary")),
        name="nsa_seq",
    )(tab, qt, gt, kc, vct, kk, vt, kk, vt, toe)


def _rel_bucket(dist, n_buckets):
    d = jnp.maximum(dist, 0)
    exact = n_buckets // 2
    df = jnp.maximum(d, 1).astype(F32)
    large = exact + (jnp.log(df / exact) / math.log(REL_MAX_DIST / exact) * (n_buckets - exact)).astype(jnp.int32)
    return jnp.where(d < exact, d, jnp.minimum(large, n_buckets - 1))


def _masked_softmax(s, mask):
    s = jnp.where(mask, s, NEG_INF)
    e = jnp.where(mask, jnp.exp(s - s.max(-1, keepdims=True)), 0.0)
    z = e.sum(-1, keepdims=True)
    return e / jnp.where(z > 0, z, 1.0)


def _compress(k, w1, b1, w2, b2, pe):
    bsz, length, n_kv, dh = k.shape
    cmp_block = pe.shape[0]
    ratio = cmp_block // CMP_STRIDE
    hidden = w1.shape[-1]
    nc = (length - cmp_block) // CMP_STRIDE + 1
    nch = nc + ratio - 1
    chunks = k[:, :nch * CMP_STRIDE].reshape(bsz, nch, CMP_STRIDE, n_kv, dh)
    part = jnp.einsum('bnpgd,updh->bungh', chunks, w1.reshape(ratio, CMP_STRIDE, dh, hidden))
    pre = sum(part[:, u:u + nc, u] for u in range(ratio)) + jnp.einsum('pd,pdh->h', pe, w1) + b1
    return jnp.einsum('bngh,hd->bngd', jax.nn.gelu(pre), w2) + b2


def _nsa_attend_single(q, gates, qpos, kc, vc, ks, vs, kw, vw, wpos, rel_table, n_kv, cmp_block):
    bsz, n_heads, dh = q.shape
    group = n_heads // n_kv
    n_buckets = rel_table.shape[0]
    qg = q.reshape(bsz, n_kv, group, dh)
    table = rel_table.astype(F32).reshape(n_buckets, n_kv, group)
    n_cmp = kc.shape[1]
    cend = jnp.arange(n_cmp) * CMP_STRIDE + cmp_block - 1
    dist_c = qpos - cend
    s_c = jnp.einsum('bgrd,bcgd->bgrc', qg, kc, preferred_element_type=F32) + table[_rel_bucket(dist_c, n_buckets)].transpose(1, 2, 0)
    p_c = _masked_softmax(s_c, (dist_c >= 0)[None, None, None, :])
    o_c = jnp.einsum('bgrc,bcgd->bgrd', p_c, vc)
    length = ks.shape[1]
    ns = -(-length // SEL_BLOCK)
    ci = jnp.arange(n_cmp)[:, None]
    bj = jnp.arange(ns)[None, :]
    overlap = ((ci * CMP_STRIDE < (bj + 1) * SEL_BLOCK) & (ci * CMP_STRIDE + cmp_block > bj * SEL_BLOCK)).astype(F32)
    imp = jnp.einsum('bgc,cn->bgn', p_c.sum(2), overlap, precision=HIGHEST)
    blk = jnp.arange(ns)
    cur = qpos // SEL_BLOCK
    forced = (blk == 0) | ((blk >= cur - 1) & (blk <= cur))
    valid = blk * SEL_BLOCK <= qpos
    score = jnp.where(forced, SEL_BIG, jnp.where(valid, imp, -SEL_BIG))
    _, sel = lax.top_k(score, min(N_SELECT, ns))
    selmask = jnp.zeros((bsz, n_kv, ns), bool).at[jnp.arange(bsz)[:, None, None], jnp.arange(n_kv)[None, :, None], sel].set(True)
    kpos = jnp.arange(length)
    keymask = selmask[:, :, kpos // SEL_BLOCK] & (kpos <= qpos)[None, None, :]
    dist_s = qpos - kpos
    s_s = jnp.einsum('bgrd,blgd->bgrl', qg, ks, preferred_element_type=F32) + table[_rel_bucket(dist_s, n_buckets)].transpose(1, 2, 0)
    p_s = _masked_softmax(s_s, keymask[:, :, None, :])
    o_s = jnp.einsum('bgrl,blgd->bgrd', p_s, vs)
    dist_w = qpos - wpos
    mask_w = (dist_w >= 0) & (dist_w <= WINDOW) & (wpos >= 0)
    s_w = jnp.einsum('bgrd,bkgd->bgrk', qg, kw, preferred_element_type=F32) + table[_rel_bucket(dist_w, n_buckets)].transpose(1, 2, 0)
    p_w = _masked_softmax(s_w, mask_w[None, None, None, :])
    o_w = jnp.einsum('bgrk,bkgd->bgrd', p_w, vw)
    gt = gates.reshape(bsz, n_kv, group, 3)
    o = gt[..., 0:1] * o_c + gt[..., 1:2] * o_s + gt[..., 2:3] * o_w
    return o.reshape(bsz, n_heads * dh)


def _route(rt, n_experts, tm):
    n_tok = rt.shape[0]
    n_as = n_tok * TOP_K
    assert n_as % tm == 0
    n_blocks = n_as // tm
    flat_e = rt[:, TOP_K:2 * TOP_K].astype(jnp.int32).reshape(-1)
    order = jnp.argsort(flat_e)
    se = flat_e[order]
    tok_sorted = (order // TOP_K).astype(jnp.int32)
    pos = jnp.argsort(order).astype(jnp.int32).reshape(n_tok, TOP_K)
    experts = jnp.arange(n_experts)
    starts = jnp.searchsorted(se, experts, side='left').astype(jnp.int32)
    ends = jnp.searchsorted(se, experts, side='right').astype(jnp.int32)
    first_blk = starts // tm
    n_vis_e = jnp.where(ends > starts, (ends - 1) // tm - first_blk + 1, 0)
    cum = jnp.cumsum(n_vis_e)
    v = jnp.arange(n_blocks + n_experts - 1)
    ve = jnp.minimum(jnp.searchsorted(cum, v, side='right'), n_experts - 1)
    live = v < cum[-1]
    vb = first_blk[ve] + v - (cum[ve] - n_vis_e[ve])
    e_last = jnp.max(jnp.where(ends > starts, experts, 0))
    ve = jnp.where(live, ve, e_last).astype(jnp.int32)
    vb = jnp.where(live, vb, n_blocks - 1).astype(jnp.int32)
    lo = jnp.where(live, starts[ve], 0).astype(jnp.int32)
    hi = jnp.where(live, ends[ve], 0).astype(jnp.int32)
    return tok_sorted, pos, (vb, ve, lo, hi)


def _split_mod(mod, bsz, per_row):
    parts = jnp.split(mod, 6, axis=-1)
    if per_row:
        return [p[None] for p in parts]
    return [p[:, None] for p in parts]


def _trunk(x, mods, kv_mod, conv0, h0, make_attend, p, *, per_row):
    depth = p['w_ada'].shape[0]
    n_a = p['w_in_a'].shape[0]
    bsz, t, d = x.shape
    n_experts = p['w_router'].shape[-1]
    alpha = (2 * depth) ** 0.25
    tt = min(256, t)
    conv_new, h_new = [], []
    attend, kv_state = None, None
    for l in range(depth):
        sh1, sc1, g1, sh2, sc2, g2 = _split_mod(mods[l], bsz, per_row)
        if l < n_a:
            win_bf = p['w_in_a'][l].astype(BF16)
            wg_bf = p['w_gate_a'][l].astype(BF16)
            args = (win_bf, p['w_conv'][l], p['b_conv'][l][None], wg_bf, p['b_gate_a'][l], p['lru_lambda'][l][None])
            if per_row:
                gated, cb, hl = _rglru_step(x[0], sc1[0], sh1[0], conv0[l].swapaxes(0, 1), h0[l], *args)
                gated, cb = gated[None], cb.swapaxes(0, 1)
            else:
                gated, cb, hl = _rglru_seq(x, sc1, sh1, conv0[l], h0[l][:, None], *args, tt=tt)
                hl = hl[:, 0]
            conv_new.append(cb)
            h_new.append(hl)
            mix, wout = gated, p['w_out_a'][l]
        else:
            if l == n_a:
                ksh, ksc = jnp.split(kv_mod, 2, axis=-1)
                ksh, ksc = (ksh[None], ksc[None]) if per_row else (ksh[:, None], ksc[:, None])
                attend, kv_state = make_attend(x, ksc, ksh)
            lb = l - n_a
            mix, wout = attend(x, sc1, sh1, p['w_in_b'][lb]), p['w_out_b'][lb]
        wr_pad = jnp.zeros((d, LANES), F32).at[:, :n_experts].set(p['w_router'][l])
        br_pad = jnp.zeros((1, LANES), F32).at[0, :n_experts].set(p['b_router'][l])
        x1, hin2, rt = _post(mix, x, g1, sc2, sh2, wout.astype(BF16), p['ln_g'][l, 0][None], p['ln_b'][l, 0][None],
                             wr_pad, br_pad, alpha=alpha, n_experts=n_experts, tt=tt)
        n_tok = bsz * t
        rt2 = rt.reshape(n_tok, LANES)
        moe_w = (p['w_gu'][l], p['b_gu'][l], p['w_down'][l], p['b_down'][l])
        if per_row:
            ff = _moe_dense(hin2.reshape(n_tok, d), rt2, *moe_w).reshape(bsz, t, d)
            x = _final(x1, ff, rt, g2, p['ln_g'][l, 1][None], p['ln_b'][l, 1][None], alpha=alpha, combine=False, tt=tt)
        else:
            tok_sorted, pos, visits = _route(rt2, n_experts, MOE_ROWS)
            xs = hin2.reshape(n_tok, d)[tok_sorted]
            ys = _moe_grouped(visits, xs, *moe_w, tm=MOE_ROWS)
            yg = ys[pos.T.reshape(-1)].reshape(TOP_K, bsz, t, d)
            x = _final(x1, yg, rt, g2, p['ln_g'][l, 1][None], p['ln_b'][l, 1][None], alpha=alpha, combine=True, tt=tt)
    return x, jnp.stack(conv_new), jnp.stack(h_new), kv_state


def _qg_weight_t(w_in, n_kv, group, dh):
    d = w_in.shape[0]
    hd = n_kv * group * dh
    wg = w_in[:, hd:].reshape(d, n_kv, 3 * group)
    wg = jnp.pad(wg, ((0, 0), (0, 0), (0, 16 - 3 * group))).reshape(d, n_kv * 16)
    return jnp.concatenate([w_in[:, :hd], wg], axis=1).T.astype(BF16)


def _toeplitz_bias(rel_table, tq):
    n_buckets = rel_table.shape[0]
    thr = np.asarray(_rel_thresholds(n_buckets))
    kk = np.arange(tq)[:, None]
    ii = np.arange(tq)[None, :]
    dist = np.maximum(np.stack([ii - kk, tq + ii - kk]), 0)
    bucket = np.searchsorted(thr, dist, side='right') - 1
    return rel_table.astype(F32).T[:, bucket]


def kernel(x_prompt, x_sample, c_prompt, c_sample, state_conv, state_h, cache_kv, cache_win, page_table, w_ada, b_ada, ln_g, ln_b, w_in_a, w_conv, b_conv, w_gate_a, b_gate_a, lru_lambda, w_out_a, w_ada_kv, b_ada_kv, w_kv, cmp_w1, cmp_b1, cmp_w2, cmp_b2, cmp_pe, w_in_b, w_out_b, rel_table, w_router, b_router, w_gu, b_gu, w_down, b_down):
    p = dict(w_ada=w_ada, b_ada=b_ada, ln_g=ln_g, ln_b=ln_b, w_in_a=w_in_a, w_conv=w_conv, b_conv=b_conv,
             w_gate_a=w_gate_a, b_gate_a=b_gate_a, lru_lambda=lru_lambda, w_out_a=w_out_a, w_kv=w_kv,
             w_in_b=w_in_b, w_out_b=w_out_b, w_router=w_router, b_router=b_router, w_gu=w_gu, b_gu=b_gu,
             w_down=w_down, b_down=b_down)
    bsz, seq, d = x_prompt.shape
    dec_b = x_sample.shape[0]
    n_a = w_in_a.shape[0]
    cw = w_conv.shape[1]
    lru_w = w_conv.shape[2]
    n_kv, dh = cache_kv.shape[3], cache_kv.shape[4]
    n_heads = rel_table.shape[1]
    group = n_heads // n_kv
    hd = n_heads * dh
    gd = n_kv * dh
    n_parts = w_kv.shape[1] // gd
    assert n_parts == 6 and 3 * group <= 16
    cmp_block = cmp_pe.shape[1]
    scale = dh ** -0.5
    assert math.log2(scale) == round(math.log2(scale))

    c_all = jnp.concatenate([c_prompt, c_sample], axis=0)
    mods = _linear(c_all, w_ada, b_ada[:, None, :], silu_in=True)
    kv_mod = _linear(c_all, w_ada_kv[None], b_ada_kv[None, None, :], silu_in=True)[0]
    tab = rel_table.astype(F32).reshape(-1)

    def cmp_kv(k, v):
        kc = _compress(k, cmp_w1[0], cmp_b1[0], cmp_w2[0], cmp_b2[0], cmp_pe[0])
        vc = _compress(v, cmp_w1[1], cmp_b1[1], cmp_w2[1], cmp_b2[1], cmp_pe[1])
        return kc, vc

    def make_prompt_attend(x, ksc, ksh):
        tq = min(NSA_TILE, seq)
        wvt = jnp.concatenate([w_kv[:, 3 * gd:4 * gd], w_kv[:, 5 * gd:6 * gd]], axis=1).T.astype(BF16)
        kv, kk, vt = _kvproj(x, ksc, ksh, w_kv.astype(BF16), wvt, n_kv=n_kv, dh=dh, k_parts=(2, 4), tt=tq)
        kv6 = kv.reshape(bsz, seq, n_parts, n_kv, dh)
        kc, vc = cmp_kv(kv6[:, :, 0], kv6[:, :, 1])
        n_cmp = kc.shape[1]
        ncp = -(-n_cmp // SUBLANES_BF16) * SUBLANES_BF16
        padc = lambda a: jnp.pad(a, ((0, 0), (0, ncp - n_cmp), (0, 0), (0, 0))).astype(BF16)
        kc_g = padc(kc).transpose(0, 2, 1, 3)
        vc_t = padc(vc).transpose(0, 2, 3, 1)
        toe = _toeplitz_bias(rel_table, tq)

        def attend(x, sc1, sh1, w_in):
            qt, gt = _qproj(x, sc1, sh1, _qg_weight_t(w_in, n_kv, group, dh), hd=hd, scale=scale, tt=tq)
            return _nsa_seq(tab, qt, gt, kc_g, vc_t, kk, vt, toe, group=group, dh=dh, cmp_block=cmp_block,
                            n_cmp_real=n_cmp, tq=tq)

        n_win = min(WINDOW, seq)
        return attend, (kv6[:, :, :4], kv6[:, seq - n_win:, 4:])

    def make_sample_attend(x, ksc, ksh):
        kv = _modlinear(x, ksc, ksh, w_kv.astype(BF16), tt=dec_b)
        kv6 = kv.reshape(dec_b, 1, n_parts, n_kv, dh)
        past_len = page_table.shape[1] * cache_kv.shape[1]

        def full(u):
            past = cache_kv[page_table, :, u].reshape(dec_b, past_len, n_kv, dh)
            return jnp.concatenate([past, kv6[:, :, u]], axis=1)

        kc, vc = cmp_kv(full(0), full(1))
        ks, vs = full(2), full(3)
        n_buf = cache_win.shape[1]
        win = jnp.concatenate([cache_win, kv6[:, :, 4:]], axis=1)
        wpos = past_len - n_buf + jnp.arange(n_buf + 1)

        def attend(x, sc1, sh1, w_in):
            qt, gt = _qproj(x, sc1, sh1, _qg_weight_t(w_in, n_kv, group, dh), hd=hd, scale=scale, tt=dec_b)
            q = qt[0].T.astype(F32).reshape(dec_b, n_heads, dh)
            gates = gt[0].reshape(n_kv, 16, dec_b)[:, :3 * group].transpose(2, 0, 1).reshape(dec_b, n_heads, 3)
            o = _nsa_attend_single(q, gates, past_len, kc, vc, ks, vs, win[:, :, 0], win[:, :, 1], wpos, rel_table, n_kv,
                                   cmp_block)
            return o[None]

        return attend, (kv6[:, :, :4], win[:, -n_buf:])

    conv0 = jnp.zeros((n_a, bsz, cw - 1, lru_w), F32)
    h0 = jnp.zeros((n_a, bsz, lru_w), F32)
    y_prompt, prompt_conv, prompt_h, (prompt_kv, prompt_win) = _trunk(
        x_prompt, mods[:, :bsz], kv_mod[:bsz], conv0, h0, make_prompt_attend, p, per_row=False)
    y_sample, sample_conv, sample_h, (sample_kv, sample_win) = _trunk(
        x_sample.reshape(1, dec_b, d), mods[:, bsz:], kv_mod[bsz:], state_conv, state_h, make_sample_attend, p, per_row=True)
    return (y_prompt, y_sample.reshape(dec_b, 1, d), prompt_conv, prompt_h, prompt_kv, prompt_win,
            sample_conv, sample_h, sample_kv, sample_win)
```

```python
import functools
import math

import numpy as np
import jax
import jax.numpy as jnp
from jax import lax
from jax.experimental import pallas as pl
from jax.experimental.pallas import tpu as pltpu

F32 = jnp.float32
BF16 = jnp.bfloat16
HIGHEST = lax.Precision.HIGHEST
NT_DIMS = (((1,), (1,)), ((), ()))

CMP_STRIDE = 16
SEL_BLOCK = 64
N_SELECT = 16
WINDOW = 512
REL_MAX_DIST = 128
TOP_K = 4
RGLRU_C = 8.0
SWIGLU_LIMIT = 7.0
SWIGLU_ALPHA = 1.702
MOE_ROWS = 256
LN_EPS = 1e-5
SEL_BIG = 1e9
NEG_INF = -1e30
M_INIT = -1e29

LANES = 128
SUBLANES_BF16 = 16
VMEM_LIMIT = 48 * 1024 * 1024
NSA_TILE = 256


def _cp(sem, vmem=VMEM_LIMIT):
    return pltpu.CompilerParams(dimension_semantics=sem, vmem_limit_bytes=vmem)


def _rel_thresholds(n_buckets):
    exact = n_buckets // 2
    d_max = REL_MAX_DIST + 1
    buckets = []
    for d in range(d_max + 1):
        if d < exact:
            buckets.append(d)
        else:
            large = exact + int(math.log(max(d, 1) / exact) / math.log(REL_MAX_DIST / exact) * (n_buckets - exact))
            buckets.append(min(large, n_buckets - 1))
    return [next(d for d in range(d_max + 1) if buckets[d] >= b) for b in range(n_buckets)]


def _linear_kernel(x_ref, w_ref, b_ref, o_ref, *, act_in):
    x = x_ref[...]
    if act_in == 'silu':
        x = x * jax.nn.sigmoid(x)
    elif act_in == 'gelu':
        x = jax.nn.gelu(x)
    y = jnp.dot(x.astype(BF16), w_ref[0].astype(BF16), preferred_element_type=F32)
    o_ref[0] = y + b_ref[0]


def _linear(x, w, b, *, act_in=None, tn=512):
    m, k = x.shape
    nl, _, n = w.shape
    tn = min(tn, n)
    return pl.pallas_call(
        functools.partial(_linear_kernel, act_in=act_in),
        grid=(nl, n // tn),
        in_specs=[
            pl.BlockSpec((m, k), lambda l, j: (0, 0)),
            pl.BlockSpec((1, k, tn), lambda l, j: (l, 0, j)),
            pl.BlockSpec((1, 1, tn), lambda l, j: (l, 0, j)),
        ],
        out_specs=pl.BlockSpec((1, m, tn), lambda l, j: (l, 0, j)),
        out_shape=jax.ShapeDtypeStruct((nl, m, n), F32),
        compiler_params=_cp(("parallel", "parallel")),
        name="linear",
    )(x, w, b)


def _mod_spec(mod, tt):
    d = mod.shape[-1]
    if mod.shape[1] == 1:
        return pl.BlockSpec((1, 1, d), lambda b, i: (b, 0, 0))
    return pl.BlockSpec((1, tt, d), lambda b, i: (b, i, 0))


def _modlinear_kernel(x_ref, sc_ref, sh_ref, w_ref, o_ref):
    hin = x_ref[0] * (1.0 + sc_ref[0]) + sh_ref[0]
    o_ref[0] = jnp.dot(hin.astype(BF16), w_ref[...], preferred_element_type=F32)


def _modlinear(x, sc, sh, w_bf, *, tt):
    bsz, t, d = x.shape
    n = w_bf.shape[1]
    return pl.pallas_call(
        _modlinear_kernel,
        grid=(bsz, t // tt),
        in_specs=[
            pl.BlockSpec((1, tt, d), lambda b, i: (b, i, 0)),
            _mod_spec(sc, tt),
            _mod_spec(sh, tt),
            pl.BlockSpec((d, n), lambda b, i: (0, 0)),
        ],
        out_specs=pl.BlockSpec((1, tt, n), lambda b, i: (b, i, 0)),
        out_shape=jax.ShapeDtypeStruct((bsz, t, n), F32),
        compiler_params=_cp(("parallel", "parallel")),
        name="modlinear",
    )(x, sc, sh, w_bf)


def _kvproj_kernel(x_ref, sc_ref, sh_ref, w_ref, wvt_ref, kv_ref, kk_ref, vt_ref, *, n_kv, dh, k_parts):
    hin = (x_ref[0] * (1.0 + sc_ref[0]) + sh_ref[0]).astype(BF16)
    y = jnp.dot(hin, w_ref[...], preferred_element_type=F32)
    kv_ref[0] = y
    gd = n_kv * dh
    for j, part in enumerate(k_parts):
        for g in range(n_kv):
            kk_ref[j, 0, g] = y[:, part * gd + g * dh:part * gd + (g + 1) * dh].astype(BF16)
    yt = lax.dot_general(wvt_ref[...], hin, NT_DIMS, preferred_element_type=F32)
    for j in range(vt_ref.shape[1]):
        vt_ref[0, j, 0] = yt[j * dh:(j + 1) * dh].astype(BF16)


def _kvproj(x, sc, sh, w_bf, wvt_bf, *, n_kv, dh, k_parts, tt):
    bsz, t, d = x.shape
    n = w_bf.shape[1]
    nv = wvt_bf.shape[0] // dh
    return pl.pallas_call(
        functools.partial(_kvproj_kernel, n_kv=n_kv, dh=dh, k_parts=k_parts),
        grid=(bsz, t // tt),
        in_specs=[
            pl.BlockSpec((1, tt, d), lambda b, i: (b, i, 0)),
            _mod_spec(sc, tt),
            _mod_spec(sh, tt),
            pl.BlockSpec((d, n), lambda b, i: (0, 0)),
            pl.BlockSpec((nv * dh, d), lambda b, i: (0, 0)),
        ],
        out_specs=[
            pl.BlockSpec((1, tt, n), lambda b, i: (b, i, 0)),
            pl.BlockSpec((len(k_parts), 1, n_kv, tt, dh), lambda b, i: (0, b, 0, i, 0)),
            pl.BlockSpec((1, nv, 1, dh, tt), lambda b, i: (b, 0, i, 0, 0)),
        ],
        out_shape=[
            jax.ShapeDtypeStruct((bsz, t, n), F32),
            jax.ShapeDtypeStruct((len(k_parts), bsz, n_kv, t, dh), BF16),
            jax.ShapeDtypeStruct((bsz, nv, t // tt, dh, tt), BF16),
        ],
        compiler_params=_cp(("parallel", "parallel")),
        name="kvproj",
    )(x, sc, sh, w_bf, wvt_bf)


def _qproj_kernel(x_ref, sc_ref, sh_ref, wt_ref, qt_ref, gt_ref, *, hd, scale):
    hin = (x_ref[0] * (1.0 + sc_ref[0]) + sh_ref[0]).astype(BF16)
    yt = lax.dot_general(wt_ref[...], hin, NT_DIMS, preferred_element_type=F32)
    qt_ref[0] = (yt[:hd] * scale).astype(BF16)
    gt_ref[0] = jax.nn.sigmoid(yt[hd:])


def _qproj(x, sc, sh, wt_bf, *, hd, scale, tt):
    bsz, t, d = x.shape
    n = wt_bf.shape[0]
    return pl.pallas_call(
        functools.partial(_qproj_kernel, hd=hd, scale=scale),
        grid=(bsz, t // tt),
        in_specs=[
            pl.BlockSpec((1, tt, d), lambda b, i: (b, i, 0)),
            _mod_spec(sc, tt),
            _mod_spec(sh, tt),
            pl.BlockSpec((n, d), lambda b, i: (0, 0)),
        ],
        out_specs=[
            pl.BlockSpec((1, hd, tt), lambda b, i: (b, 0, i)),
            pl.BlockSpec((1, n - hd, tt), lambda b, i: (b, 0, i)),
        ],
        out_shape=[
            jax.ShapeDtypeStruct((bsz, hd, t), BF16),
            jax.ShapeDtypeStruct((bsz, n - hd, t), F32),
        ],
        compiler_params=_cp(("parallel", "parallel")),
        name="qproj",
    )(x, sc, sh, wt_bf)


def _log1p(x):
    u = 1.0 + x
    return jnp.where(u == 1.0, x, jnp.log(u) * x / jnp.where(u == 1.0, 1.0, u - 1.0))


def _expm1(x):
    u = jnp.exp(x)
    safe = (u != 1.0) & (u > 0.0)
    return jnp.where(u == 1.0, x, jnp.where(u > 0.0, (u - 1.0) * x / jnp.where(safe, jnp.log(u), 1.0), -1.0))


def _softplus(z):
    return jnp.maximum(z, 0.0) + _log1p(jnp.exp(-jnp.abs(z)))


def _rglru_gates(xc, wg_ref, bg_ref, lam_ref, n_heads):
    w = xc.shape[1]
    bw = w // n_heads
    xcb = xc.astype(BF16)
    gates = []
    for g in range(2):
        cols = [jnp.dot(xcb[:, i * bw:(i + 1) * bw], wg_ref[g, i], preferred_element_type=F32) for i in range(n_heads)]
        gates.append(jnp.concatenate(cols, axis=1) + bg_ref[g:g + 1, :])
    r = jax.nn.sigmoid(gates[0])
    i_g = jax.nn.sigmoid(gates[1])
    log_a = (-RGLRU_C * _softplus(-lam_ref[...])) * r
    a = jnp.exp(log_a)
    b_in = jnp.sqrt(-_expm1(2.0 * log_a)) * (i_g * xc)
    return a, b_in


def _rglru_seq_kernel(x_ref, sc_ref, sh_ref, conv0_ref, h0_ref, win_ref, wconv_ref, bconv_ref, wg_ref, bg_ref, lam_ref,
                      gated_ref, convout_ref, hout_ref, ext_ref, a_ref, b_ref, yb_ref, h_ref, *, tt, n_heads, cw):
    t = pl.program_id(1)
    w = a_ref.shape[1]
    pad = 8

    @pl.when(t == 0)
    def _():
        ext_ref[0:pad - (cw - 1), :] = jnp.zeros((pad - (cw - 1), w), F32)
        ext_ref[pad - (cw - 1):pad, :] = conv0_ref[0]
        h_ref[...] = h0_ref[0]

    hin = x_ref[0] * (1.0 + sc_ref[0]) + sh_ref[0]
    u = jnp.dot(hin.astype(BF16), win_ref[...], preferred_element_type=F32)
    yb_ref[...] = jax.nn.gelu(u[:, :w])
    ext_ref[pad:pad + tt, :] = u[:, w:]
    xc = bconv_ref[...] + sum(ext_ref[pad - (cw - 1) + k:pad - (cw - 1) + k + tt, :] * wconv_ref[k:k + 1, :] for k in range(cw))
    tail = ext_ref[tt:tt + pad, :]
    ext_ref[0:pad, :] = tail

    a, b_in = _rglru_gates(xc, wg_ref, bg_ref, lam_ref, n_heads)
    a_ref[...] = a
    b_ref[...] = b_in

    row = lax.broadcasted_iota(jnp.int32, (tt, LANES), 0)
    for c in range(w // LANES):
        cs = slice(c * LANES, (c + 1) * LANES)
        av = a_ref[:, cs]
        bv = b_ref[:, cs]
        s = 1
        while s < tt:
            keep = row >= s
            b_sh = jnp.where(keep, pltpu.roll(bv, s, 0), 0.0)
            a_sh = jnp.where(keep, pltpu.roll(av, s, 0), 1.0)
            bv = av * b_sh + bv
            av = av * a_sh
            s *= 2
        hs = av * h_ref[:, cs] + bv
        h_ref[:, cs] = hs[tt - 1:tt, :]
        gated_ref[0, :, cs] = (hs * yb_ref[:, cs]).astype(gated_ref.dtype)

    @pl.when(t == pl.num_programs(1) - 1)
    def _():
        convout_ref[0] = tail[pad - (cw - 1):pad, :]
        hout_ref[0] = h_ref[...]


def _rglru_seq(x, sc, sh, conv0, h0, win_bf, wconv, bconv, wg_bf, bg, lam, *, tt):
    bsz, t, d = x.shape
    cw, w = wconv.shape
    n_heads = wg_bf.shape[1]
    full = lambda shape: pl.BlockSpec(shape, lambda b, i: (0,) * len(shape))
    return pl.pallas_call(
        functools.partial(_rglru_seq_kernel, tt=tt, n_heads=n_heads, cw=cw),
        grid=(bsz, t // tt),
        in_specs=[
            pl.BlockSpec((1, tt, d), lambda b, i: (b, i, 0)),
            _mod_spec(sc, tt),
            _mod_spec(sh, tt),
            pl.BlockSpec((1, cw - 1, w), lambda b, i: (b, 0, 0)),
            pl.BlockSpec((1, 1, w), lambda b, i: (b, 0, 0)),
            full((d, 2 * w)),
            full((cw, w)),
            full((1, w)),
            full(wg_bf.shape),
            full((2, w)),
            full((1, w)),
        ],
        out_specs=[
            pl.BlockSpec((1, tt, w), lambda b, i: (b, i, 0)),
            pl.BlockSpec((1, cw - 1, w), lambda b, i: (b, 0, 0)),
            pl.BlockSpec((1, 1, w), lambda b, i: (b, 0, 0)),
        ],
        out_shape=[
            jax.ShapeDtypeStruct((bsz, t, w), BF16),
            jax.ShapeDtypeStruct((bsz, cw - 1, w), F32),
            jax.ShapeDtypeStruct((bsz, 1, w), F32),
        ],
        scratch_shapes=[
            pltpu.VMEM((tt + 8, w), F32),
            pltpu.VMEM((tt, w), F32),
            pltpu.VMEM((tt, w), F32),
            pltpu.VMEM((tt, w), F32),
            pltpu.VMEM((1, w), F32),
        ],
        compiler_params=_cp(("parallel", "arbitrary")),
        name="rglru_seq",
    )(x, sc, sh, conv0, h0, win_bf, wconv, bconv, wg_bf, bg, lam)


def _rglru_step_kernel(x_ref, sc_ref, sh_ref, conv_ref, h0_ref, win_ref, wconv_ref, bconv_ref, wg_ref, bg_ref, lam_ref,
                       gated_ref, convout_ref, hout_ref, *, n_heads, cw):
    w = h0_ref.shape[1]
    hin = x_ref[...] * (1.0 + sc_ref[...]) + sh_ref[...]
    u = jnp.dot(hin.astype(BF16), win_ref[...], preferred_element_type=F32)
    yb = jax.nn.gelu(u[:, :w])
    xb = u[:, w:]
    taps = [conv_ref[k] for k in range(cw - 1)] + [xb]
    xc = bconv_ref[...] + sum(taps[k] * wconv_ref[k:k + 1, :] for k in range(cw))
    a, b_in = _rglru_gates(xc, wg_ref, bg_ref, lam_ref, n_heads)
    h = a * h0_ref[...] + b_in
    gated_ref[...] = (h * yb).astype(gated_ref.dtype)
    for k in range(cw - 1):
        convout_ref[k] = taps[k + 1]
    hout_ref[...] = h


def _rglru_step(x, sc, sh, conv, h0, win_bf, wconv, bconv, wg_bf, bg, lam):
    r, _ = x.shape
    cw, w = wconv.shape
    n_heads = wg_bf.shape[1]
    return pl.pallas_call(
        functools.partial(_rglru_step_kernel, n_heads=n_heads, cw=cw),
        out_shape=[
            jax.ShapeDtypeStruct((r, w), BF16),
            jax.ShapeDtypeStruct((cw - 1, r, w), F32),
            jax.ShapeDtypeStruct((r, w), F32),
        ],
        compiler_params=pltpu.CompilerParams(vmem_limit_bytes=VMEM_LIMIT),
        name="rglru_step",
    )(x, sc, sh, conv, h0, win_bf, wconv, bconv, wg_bf, bg, lam)


def _layer_norm(v, g, b):
    mu = jnp.mean(v, axis=-1, keepdims=True)
    dv = v - mu
    var = jnp.mean(dv * dv, axis=-1, keepdims=True)
    return dv * lax.rsqrt(var + LN_EPS) * g + b


def _post_kernel(a_ref, x_ref, g1_ref, sc2_ref, sh2_ref, wout_ref, lng_ref, lnb_ref, wr_ref, br_ref,
                 x1_ref, hin2_ref, rt_ref, *, alpha, n_experts):
    y = jnp.dot(a_ref[0].astype(BF16), wout_ref[...], preferred_element_type=F32)
    x1 = _layer_norm(alpha * x_ref[0] + (1.0 + g1_ref[0]) * y, lng_ref[...], lnb_ref[...])
    x1_ref[0] = x1
    hin2 = x1 * (1.0 + sc2_ref[0]) + sh2_ref[0]
    hin2_ref[0] = hin2.astype(hin2_ref.dtype)
    logits = jnp.dot(hin2, wr_ref[...], preferred_element_type=F32, precision=HIGHEST) + br_ref[...]
    lane = lax.broadcasted_iota(jnp.int32, logits.shape, 1)
    work = jnp.where(lane < n_experts, logits, -jnp.inf)
    vals, idxs = [], []
    for _ in range(TOP_K):
        m = jnp.max(work, axis=-1, keepdims=True)
        idx = jnp.min(jnp.where(work == m, lane, LANES), axis=-1, keepdims=True)
        vals.append(m)
        idxs.append(idx)
        work = jnp.where(lane == idx, -jnp.inf, work)
    es = [jnp.exp(v - vals[0]) for v in vals]
    z = sum(es)
    rt = jnp.zeros(logits.shape, F32)
    for k in range(TOP_K):
        rt = jnp.where(lane == k, es[k] / z, rt)
        rt = jnp.where(lane == TOP_K + k, idxs[k].astype(F32), rt)
    rt_ref[0] = rt


def _post(a, x, g1, sc2, sh2, wout_bf, lng, lnb, wr_pad, br_pad, *, alpha, n_experts, tt):
    bsz, t, d = x.shape
    dk = a.shape[-1]
    full = lambda shape: pl.BlockSpec(shape, lambda b, i: (0,) * len(shape))
    row = pl.BlockSpec((1, tt, d), lambda b, i: (b, i, 0))
    return pl.pallas_call(
        functools.partial(_post_kernel, alpha=alpha, n_experts=n_experts),
        grid=(bsz, t // tt),
        in_specs=[
            pl.BlockSpec((1, tt, dk), lambda b, i: (b, i, 0)),
            row,
            _mod_spec(g1, tt),
            _mod_spec(sc2, tt),
            _mod_spec(sh2, tt),
            full((dk, d)),
            full((1, d)),
            full((1, d)),
            full((d, LANES)),
            full((1, LANES)),
        ],
        out_specs=[row, row, pl.BlockSpec((1, tt, LANES), lambda b, i: (b, i, 0))],
        out_shape=[
            jax.ShapeDtypeStruct((bsz, t, d), F32),
            jax.ShapeDtypeStruct((bsz, t, d), BF16),
            jax.ShapeDtypeStruct((bsz, t, LANES), F32),
        ],
        compiler_params=_cp(("parallel", "parallel")),
        name="post",
    )(a, x, g1, sc2, sh2, wout_bf, lng, lnb, wr_pad, br_pad)


def _swiglu(g, u):
    g = jnp.minimum(g, SWIGLU_LIMIT)
    u = jnp.clip(u, -SWIGLU_LIMIT, SWIGLU_LIMIT)
    return g * jax.nn.sigmoid(SWIGLU_ALPHA * g) * (u + 1.0)


def _moe_grouped_kernel(vb_ref, ve_ref, lo_ref, hi_ref, dst_ref, xs_ref, wgu_ref, bgu_ref, wdn_ref, bdn_ref, yg_ref,
                        wgu_bf, wdn_bf, stage_ref, sem, *, chunk, tm, trash0):
    v = pl.program_id(0)
    last = pl.num_programs(0) - 1
    d, de2 = wgu_bf.shape
    de = de2 // 2
    e = ve_ref[v]
    pv = jnp.maximum(v - 1, 0)
    slot = v % 2

    def live(u):
        return hi_ref[u] > lo_ref[u]

    def wait_rows(s):
        pltpu.make_async_copy(stage_ref.at[s], yg_ref.at[pl.ds(0, tm)], sem.at[s]).wait()

    @pl.when((v == 0) | (e != ve_ref[pv]))
    def _():
        for c in range(de2 // chunk):
            wgu_bf[:, c * chunk:(c + 1) * chunk] = wgu_ref[0, :, c * chunk:(c + 1) * chunk].astype(BF16)
        for c in range(de // chunk):
            wdn_bf[c * chunk:(c + 1) * chunk, :] = wdn_ref[0, c * chunk:(c + 1) * chunk, :].astype(BF16)

    @pl.when(v == 0)
    def _():
        stage_ref[...] = jnp.zeros(stage_ref.shape, F32)
        for s in range(2):
            fill = pltpu.make_async_copy(stage_ref.at[s], yg_ref.at[pl.ds(trash0 + s * tm, tm)], sem.at[s])
            fill.start()
            fill.wait()

    @pl.when((v >= 2) & live(jnp.maximum(v - 2, 0)))
    def _():
        wait_rows(slot)

    lo = lo_ref[v]
    hi = hi_ref[v]

    @pl.when(hi > lo)
    def _():
        x = xs_ref[...]
        acc = jnp.zeros((tm, d), F32) + bdn_ref[0]
        for c in range(de // chunk):
            g = jnp.dot(x, wgu_bf[:, c * chunk:(c + 1) * chunk], preferred_element_type=F32) + bgu_ref[0, :, c * chunk:(c + 1) * chunk]
            u = jnp.dot(x, wgu_bf[:, de + c * chunk:de + (c + 1) * chunk], preferred_element_type=F32) + bgu_ref[0, :, de + c * chunk:de + (c + 1) * chunk]
            h = _swiglu(g, u).astype(BF16)
            acc = acc + jnp.dot(h, wdn_bf[c * chunk:(c + 1) * chunk, :], preferred_element_type=F32)
        for s in range(2):
            @pl.when(slot == s)
            def _(s=s):
                stage_ref[s] = acc

                for i in range(tm):
                    pltpu.make_async_copy(stage_ref.at[s, pl.ds(i, 1)], yg_ref.at[pl.ds(dst_ref[0, 0, i], 1)],
                                          sem.at[s]).start()

    @pl.when(v == last)
    def _():
        @pl.when(hi > lo)
        def _():
            wait_rows(slot)

        @pl.when((v >= 1) & live(pv))
        def _():
            wait_rows(1 - slot)


def _moe_grouped(visits, dst, xs, w_gu, b_gu, w_dn, b_dn, *, tm, trash0):
    n_rows, d = xs.shape
    n_e, _, de2 = w_gu.shape
    de = de2 // 2
    n_vis = visits[0].shape[0]
    grid_spec = pltpu.PrefetchScalarGridSpec(
        num_scalar_prefetch=4,
        grid=(n_vis,),
        in_specs=[
            pl.BlockSpec((1, 1, tm), lambda v, vb, ve, lo, hi: (v, 0, 0), memory_space=pltpu.SMEM),
            pl.BlockSpec((tm, d), lambda v, vb, ve, lo, hi: (vb[v], 0)),
            pl.BlockSpec((1, d, de2), lambda v, vb, ve, lo, hi: (ve[v], 0, 0)),
            pl.BlockSpec((1, 1, de2), lambda v, vb, ve, lo, hi: (ve[v], 0, 0)),
            pl.BlockSpec((1, de, d), lambda v, vb, ve, lo, hi: (ve[v], 0, 0)),
            pl.BlockSpec((1, 1, d), lambda v, vb, ve, lo, hi: (ve[v], 0, 0)),
        ],
        out_specs=pl.BlockSpec(memory_space=pl.ANY),
        scratch_shapes=[
            pltpu.VMEM((d, de2), BF16),
            pltpu.VMEM((de, d), BF16),
            pltpu.VMEM((2, tm, d), F32),
            pltpu.SemaphoreType.DMA((2,)),
        ],
    )
    return pl.pallas_call(
        functools.partial(_moe_grouped_kernel, chunk=min(512, de), tm=tm, trash0=trash0),
        grid_spec=grid_spec,
        out_shape=jax.ShapeDtypeStruct((trash0 + 2 * tm, d), F32),
        compiler_params=_cp(("arbitrary",), 56 * 1024 * 1024),
        name="moe_grouped",
    )(*visits, dst, xs, w_gu, b_gu.reshape(n_e, 1, de2), w_dn, b_dn.reshape(n_e, 1, d))


def _moe_dense_kernel(x_ref, rt_ref, wg_ref, wu_ref, bg_ref, bu_ref, wdn_ref, bdn_ref, o_ref):
    e = pl.program_id(0)
    c = pl.program_id(1)

    @pl.when((e == 0) & (c == 0))
    def _():
        o_ref[...] = jnp.zeros(o_ref.shape, F32)

    rt = rt_ref[...]
    ef = e.astype(F32)
    gate = sum(jnp.where(rt[:, TOP_K + k:TOP_K + k + 1] == ef, rt[:, k:k + 1], 0.0) for k in range(TOP_K))
    x = x_ref[...]
    g = jnp.dot(x, wg_ref[0].astype(BF16), preferred_element_type=F32) + bg_ref[0]
    u = jnp.dot(x, wu_ref[0].astype(BF16), preferred_element_type=F32) + bu_ref[0]
    h = _swiglu(g, u).astype(BF16)
    y = jnp.dot(h, wdn_ref[0].astype(BF16), preferred_element_type=F32)
    y = y + jnp.where(c == 0, 1.0, 0.0) * bdn_ref[0]
    o_ref[...] += gate * y


def _moe_dense(x_bf, rt, w_gu, b_gu, w_dn, b_dn, *, chunk=256):
    r, d = x_bf.shape
    n_e, _, de2 = w_gu.shape
    de = de2 // 2
    nc = de // chunk
    b_gu3 = b_gu.reshape(n_e, 1, de2)
    return pl.pallas_call(
        _moe_dense_kernel,
        grid=(n_e, nc),
        in_specs=[
            pl.BlockSpec((r, d), lambda e, c: (0, 0)),
            pl.BlockSpec((r, LANES), lambda e, c: (0, 0)),
            pl.BlockSpec((1, d, chunk), lambda e, c: (e, 0, c)),
            pl.BlockSpec((1, d, chunk), lambda e, c: (e, 0, nc + c)),
            pl.BlockSpec((1, 1, chunk), lambda e, c: (e, 0, c)),
            pl.BlockSpec((1, 1, chunk), lambda e, c: (e, 0, nc + c)),
            pl.BlockSpec((1, chunk, d), lambda e, c: (e, c, 0)),
            pl.BlockSpec((1, 1, d), lambda e, c: (e, 0, 0)),
        ],
        out_specs=pl.BlockSpec((r, d), lambda e, c: (0, 0)),
        out_shape=jax.ShapeDtypeStruct((r, d), F32),
        compiler_params=_cp(("arbitrary", "arbitrary")),
        name="moe_dense",
    )(x_bf, rt, w_gu, w_gu, b_gu3, b_gu3, w_dn, b_dn.reshape(n_e, 1, d))


def _final_kernel(x_ref, *refs, alpha, combine):
    if combine:
        ff_refs, (rt_ref, g2_ref, lng_ref, lnb_ref, o_ref) = refs[:TOP_K], refs[TOP_K:]
        rt = rt_ref[0]
        ff = sum(rt[:, k:k + 1] * ff_refs[k][...] for k in range(TOP_K))
    else:
        ff_ref, rt_ref, g2_ref, lng_ref, lnb_ref, o_ref = refs
        ff = ff_ref[0]
    o_ref[0] = _layer_norm(alpha * x_ref[0] + (1.0 + g2_ref[0]) * ff, lng_ref[...], lnb_ref[...])


def _final(x1, ff, rt, g2, lng, lnb, *, alpha, combine, tt):
    bsz, t, d = x1.shape
    row = pl.BlockSpec((1, tt, d), lambda b, i: (b, i, 0))
    nblk = bsz * t // tt
    if combine:
        ff_specs = [pl.BlockSpec((tt, d), functools.partial(lambda b, i, k: (k * nblk + b * (t // tt) + i, 0), k=k))
                    for k in range(TOP_K)]
        ffs = [ff] * TOP_K
    else:
        ff_specs, ffs = [row], [ff]
    full = lambda shape: pl.BlockSpec(shape, lambda b, i: (0,) * len(shape))
    return pl.pallas_call(
        functools.partial(_final_kernel, alpha=alpha, combine=combine),
        grid=(bsz, t // tt),
        in_specs=[row, *ff_specs, pl.BlockSpec((1, tt, LANES), lambda b, i: (b, i, 0)), _mod_spec(g2, tt), full((1, d)), full((1, d))],
        out_specs=row,
        out_shape=jax.ShapeDtypeStruct((bsz, t, d), F32),
        compiler_params=_cp(("parallel", "parallel")),
        name="final",
    )(x1, *ffs, rt, g2, lng, lnb)


def _bias_chain(dist, tab_ref, h0, n_heads_tab, group, thr):
    out = [jnp.full(dist.shape, tab_ref[h0 + r], F32) for r in range(group)]
    for b in range(1, len(thr)):
        ind = dist >= thr[b]
        out = [jnp.where(ind, tab_ref[b * n_heads_tab + h0 + r], out[r]) for r in range(group)]
    return out


def _nsa_seq_kernel(tab_ref, qt_ref, gt_ref, kc_ref, vct_ref, ks_ref, vst_ref, kw_ref, vwt_ref, toe_ref, o_ref,
                    sel_ref, flag_ref, ms_ref, accs_ref, mw_ref, accw_ref,
                    *, tq, group, dh, n_heads, thr, cmp_block, n_cmp_real, n_blk):
    g = pl.program_id(1)
    it = pl.program_id(2)
    q0 = it * tq
    h0 = g * group
    n_bkt = len(thr)
    far = thr[-1]
    bpt = tq // SEL_BLOCK
    n_tiles = ks_ref.shape[0] // tq
    assert far <= tq + 1 and WINDOW % tq == 0 and WINDOW >= 2 * tq - 1

    def lanes(parts):
        return jnp.concatenate(parts, axis=1)

    q4t = lanes([qt_ref[0, r * dh:(r + 1) * dh, :] for r in range(group)])
    far_bias = lanes([jnp.full((1, tq), tab_ref[(n_bkt - 1) * n_heads + h0 + r], F32) for r in range(group)])
    qi = lax.broadcasted_iota(jnp.int32, (1, tq), 1)
    ki = lax.broadcasted_iota(jnp.int32, (tq, 1), 0)
    qpos = q0 + qi

    ncp = kc_ref.shape[0]
    cidx = lax.broadcasted_iota(jnp.int32, (ncp, 1), 0)
    dist_c = qpos - (cidx * CMP_STRIDE + (cmp_block - 1))
    madd_c = jnp.where((dist_c >= 0) & (cidx < n_cmp_real), 0.0, NEG_INF)
    bias_c = _bias_chain(dist_c, tab_ref, h0, n_heads, group, thr)
    s_c = jnp.dot(kc_ref[...], q4t, preferred_element_type=F32) + lanes([b + madd_c for b in bias_c])
    m_c = jnp.maximum(jnp.max(s_c, axis=0, keepdims=True), M_INIT)
    e_c = jnp.exp(s_c - m_c)
    z_c = jnp.sum(e_c, axis=0, keepdims=True)
    p_c = e_c / jnp.where(z_c > 0, z_c, 1.0)
    o_c = jnp.dot(vct_ref[...], p_c.astype(BF16), preferred_element_type=F32)

    psum = sum(p_c[:, r * tq:(r + 1) * tq] for r in range(group))
    bj = lax.broadcasted_iota(jnp.int32, (n_blk, ncp), 0) * SEL_BLOCK
    ci = lax.broadcasted_iota(jnp.int32, (n_blk, ncp), 1) * CMP_STRIDE
    overlap_t = jnp.where((ci < bj + SEL_BLOCK) & (ci + cmp_block > bj), 1.0, 0.0)
    imp = jnp.dot(overlap_t, psum, preferred_element_type=F32, precision=HIGHEST)
    blk = lax.broadcasted_iota(jnp.int32, (n_blk, tq), 0)
    cur = qpos // SEL_BLOCK
    forced = (blk == 0) | ((blk >= cur - 1) & (blk <= cur))
    valid = blk * SEL_BLOCK <= qpos
    score = jnp.where(forced, SEL_BIG, jnp.where(valid, imp, -SEL_BIG))
    sel = jnp.full((n_blk, tq), NEG_INF, F32)
    for _ in range(min(N_SELECT, n_blk)):
        mx = jnp.max(score, axis=0, keepdims=True)
        idx = jnp.min(jnp.where(score == mx, blk, n_blk), axis=0, keepdims=True)
        hit = blk == idx
        sel = jnp.where(hit, 0.0, sel)
        score = jnp.where(hit, -jnp.inf, score)
    sel_ref[...] = sel
    for t in range(n_tiles):
        flag_ref[t] = jnp.max(sel[t * bpt:(t + 1) * bpt, :])

    ms_ref[...] = jnp.full(ms_ref.shape, M_INIT, F32)
    mw_ref[...] = jnp.full(mw_ref.shape, M_INIT, F32)
    accs_ref[...] = jnp.zeros(accs_ref.shape, F32)
    accw_ref[...] = jnp.zeros(accw_ref.shape, F32)
    ones = jnp.ones((SUBLANES_BF16, tq), BF16)

    def sel_madd(t):
        b0 = t * bpt
        return jnp.concatenate([jnp.broadcast_to(sel_ref[pl.ds(b0 + j, 1), :], (SEL_BLOCK, tq)) for j in range(bpt)], axis=0)

    def toeplitz(kind):
        return lanes([toe_ref[r, kind] for r in range(group)])

    def attend(k_ref, vt_ref, t, bias, madd, m_ref, acc_ref):
        kt = k_ref[pl.ds(pl.multiple_of(t * tq, tq), tq), :]
        s = jnp.dot(kt, q4t, preferred_element_type=F32) + bias
        if madd is not None:
            s = s + lanes([madd] * group)
        m_old = m_ref[...]
        m_new = jnp.maximum(m_old, jnp.max(s, axis=0, keepdims=True))
        e = jnp.exp(s - m_new).astype(BF16)
        vt = jnp.concatenate([vt_ref[t], ones], axis=0)
        acc_ref[...] = jnp.exp(m_old - m_new) * acc_ref[...] + jnp.dot(vt, e, preferred_element_type=F32)
        m_ref[...] = m_new

    causal = jnp.where(qi >= ki, 0.0, NEG_INF)

    def far_body(t, carry):
        @pl.when(flag_ref[t] > -1.0)
        def _():
            attend(ks_ref, vst_ref, t, far_bias, sel_madd(t), ms_ref, accs_ref)
        return carry

    lax.fori_loop(0, jnp.maximum(it - 1, 0), far_body, 0)

    t_prev = jnp.maximum(it - 1, 0)
    kill_prev = jnp.where(it >= 1, 0.0, NEG_INF)
    attend(ks_ref, vst_ref, t_prev, toeplitz(1), sel_madd(t_prev) + kill_prev, ms_ref, accs_ref)
    attend(ks_ref, vst_ref, it, toeplitz(0), sel_madd(it) + causal, ms_ref, accs_ref)

    for j in range(WINDOW // tq + 1):
        off = WINDOW - j * tq
        t_w = jnp.maximum(it - off // tq, 0)
        madd = None
        if off == 0:
            bias, madd = toeplitz(0), causal
        elif off == tq:
            bias = toeplitz(1)
        else:
            bias = far_bias
            if off + tq - 1 > WINDOW:
                madd = jnp.where(off + qi - ki <= WINDOW, 0.0, NEG_INF)
        if off > 0:
            kill = jnp.where(it >= off // tq, 0.0, NEG_INF)
            madd = kill if madd is None else madd + kill
            if madd.shape != (tq, tq):
                madd = jnp.broadcast_to(madd, (tq, tq))
        attend(kw_ref, vwt_ref, t_w, bias, madd, mw_ref, accw_ref)

    def finish(acc_ref):
        acc = acc_ref[...]
        l = acc[dh:dh + 1, :]
        return acc[:dh, :] / jnp.where(l > 0, l, 1.0)

    def gate(j):
        return lanes([gt_ref[0, r * 3 + j:r * 3 + j + 1, :] for r in range(group)])

    ot = gate(0) * o_c + gate(1) * finish(accs_ref) + gate(2) * finish(accw_ref)
    heads_per_store = LANES // dh
    for r in range(0, group, heads_per_store):
        stacked = jnp.concatenate([ot[:, (r + u) * tq:(r + u + 1) * tq] for u in range(heads_per_store)], axis=0)
        o_ref[0, :, r * dh:(r + heads_per_store) * dh] = stacked.T


def _nsa_seq(tab, qt, gt, kc, vct, kk, vt, toe, *, group, dh, cmp_block, n_cmp_real, tq):
    bsz, hd, s = qt.shape
    n_kv = hd // (group * dh)
    n_heads = n_kv * group
    ncp = kc.shape[2]
    n_tiles = s // tq
    n_blk = -(-s // SEL_BLOCK // 8) * 8
    thr = tuple(_rel_thresholds(tab.shape[0] // n_heads))
    rows = group * tq
    gpad = gt.shape[1] // n_kv
    assert s % tq == 0 and tq % SEL_BLOCK == 0 and dh * (LANES // dh) == LANES and group % (LANES // dh) == 0
    sq = lambda *dims: pl.BlockSpec(tuple(None if d == 0 else d for d in dims[:-1]), dims[-1])
    kern = functools.partial(_nsa_seq_kernel, tq=tq, group=group, dh=dh, n_heads=n_heads, thr=thr, cmp_block=cmp_block,
                             n_cmp_real=n_cmp_real, n_blk=n_blk)
    return pl.pallas_call(
        kern,
        grid=(bsz, n_kv, n_tiles),
        in_specs=[
            pl.BlockSpec(memory_space=pltpu.SMEM),
            pl.BlockSpec((1, group * dh, tq), lambda b, g, i: (b, g, i)),
            pl.BlockSpec((1, gpad, tq), lambda b, g, i: (b, g, i)),
            pl.BlockSpec((None, None, ncp, dh), lambda b, g, i: (b, g, 0, 0)),
            pl.BlockSpec((None, None, dh, ncp), lambda b, g, i: (b, g, 0, 0)),
            pl.BlockSpec((None, None, None, s, dh), lambda b, g, i: (0, b, g, 0, 0)),
            pl.BlockSpec((None, None, n_tiles, dh, tq), lambda b, g, i: (b, g, 0, 0, 0)),
            pl.BlockSpec((None, None, None, s, dh), lambda b, g, i: (1, b, g, 0, 0)),
            pl.BlockSpec((None, None, n_tiles, dh, tq), lambda b, g, i: (b, n_kv + g, 0, 0, 0)),
            pl.BlockSpec((group, 2, tq, tq), lambda b, g, i: (g, 0, 0, 0)),
        ],
        out_specs=pl.BlockSpec((1, tq, group * dh), lambda b, g, i: (b, i, g)),
        out_shape=jax.ShapeDtypeStruct((bsz, s, hd), F32),
        scratch_shapes=[
            pltpu.VMEM((n_blk, tq), F32),
            pltpu.SMEM((n_tiles,), F32),
            pltpu.VMEM((1, rows), F32),
            pltpu.VMEM((dh + SUBLANES_BF16, rows), F32),
            pltpu.VMEM((1, rows), F32),
            pltpu.VMEM((dh + SUBLANES_BF16, rows), F32),
        ],
        compiler_params=_cp(("parallel", "parallel", "arbitrary")),
        name="nsa_seq",
    )(tab, qt, gt, kc, vct, kk, vt, kk, vt, toe)


def _compress(k_head, length, w1, b1, w2, b2, pe):
    bsz, _, n_kv, dh = k_head.shape
    cmp_block = pe.shape[0]
    ratio = cmp_block // CMP_STRIDE
    hidden = w1.shape[-1]
    nc = (length - cmp_block) // CMP_STRIDE + 1
    r = bsz * n_kv
    x = k_head[:, :cmp_block].reshape(bsz, ratio, CMP_STRIDE, n_kv, dh).transpose(1, 0, 3, 2, 4).reshape(ratio, r, CMP_STRIDE * dh)
    w1r = w1.reshape(ratio, CMP_STRIDE * dh, hidden)
    part = [_linear(x[n], w1r, jnp.zeros((ratio, 1, hidden), F32)) for n in range(ratio)]
    const = _linear(pe.reshape(1, -1), w1.reshape(1, cmp_block * dh, hidden), b1.reshape(1, 1, hidden))[0]
    pre = sum(part[u][u:u + nc] for u in range(ratio)) + const
    n_cmp = pre.shape[0]
    out = _linear(pre.reshape(n_cmp * r, hidden), w2[None], b2.reshape(1, 1, dh), act_in='gelu')[0]
    return out.reshape(n_cmp, bsz, n_kv, dh).transpose(1, 0, 2, 3)


def _nsa_dec_kernel(pt_ref, qbd_ref, gt_ref, kct_ref, vcf_ref, bc_ref, bs_ref, bw_ref, bn_ref, win_ref, new_ref, *rest,
                    n_pages, group, n_kv, dh, cmp_block, qpos):
    page_refs, o_ref = rest[:n_pages], rest[n_pages]
    n_heads = group * n_kv
    gd = n_kv * dh
    page = page_refs[0].shape[1]
    qbd = qbd_ref[0]
    new = new_ref[0]

    def update(s, v, st):
        m, l, acc = st
        m_new = jnp.maximum(m, jnp.max(s, axis=-1, keepdims=True))
        e = jnp.exp(s - m_new)
        a = jnp.exp(m - m_new)
        return m_new, a * l + jnp.sum(e, axis=-1, keepdims=True), a * acc + jnp.dot(e.astype(BF16), v, preferred_element_type=F32)

    def update_row(s, k_row, v_row, st):
        m, l, acc = st
        s = s + jnp.sum(qbd.astype(F32) * k_row.astype(BF16).astype(F32), axis=-1, keepdims=True)
        m_new = jnp.maximum(m, s)
        e = jnp.exp(s - m_new)
        a = jnp.exp(m - m_new)
        return m_new, a * l + e, a * acc + e.astype(BF16).astype(F32) * v_row.astype(BF16).astype(F32)

    def finish(st):
        _, l, acc = st
        return acc / jnp.where(l > 0, l, 1.0)

    init = (jnp.full((n_heads, 1), M_INIT, F32), jnp.zeros((n_heads, 1), F32), jnp.zeros((n_heads, gd), F32))

    s_c = jnp.dot(qbd, kct_ref[0], preferred_element_type=F32) + bc_ref[...]
    m_c = jnp.maximum(jnp.max(s_c, axis=-1, keepdims=True), M_INIT)
    e_c = jnp.exp(s_c - m_c)
    z_c = jnp.sum(e_c, axis=-1, keepdims=True)
    p_c = e_c / jnp.where(z_c > 0, z_c, 1.0)
    o_c = jnp.dot(p_c.astype(BF16), vcf_ref[0], preferred_element_type=F32)

    ncp = p_c.shape[1]
    g_of_h = lax.broadcasted_iota(jnp.int32, (n_kv, n_heads), 1) // group
    gsel = jnp.where(g_of_h == lax.broadcasted_iota(jnp.int32, (n_kv, n_heads), 0), 1.0, 0.0)
    psum = jnp.dot(gsel, p_c, preferred_element_type=F32, precision=HIGHEST)
    ci = lax.broadcasted_iota(jnp.int32, (ncp, LANES), 0) * CMP_STRIDE
    bj = lax.broadcasted_iota(jnp.int32, (ncp, LANES), 1) * SEL_BLOCK
    overlap = jnp.where((ci < bj + SEL_BLOCK) & (ci + cmp_block > bj), 1.0, 0.0)
    imp = jnp.dot(psum, overlap, preferred_element_type=F32, precision=HIGHEST)
    n_blk = qpos // SEL_BLOCK + 1
    assert n_blk <= LANES
    cur = qpos // SEL_BLOCK
    blk = lax.broadcasted_iota(jnp.int32, (n_kv, LANES), 1)
    forced = (blk == 0) | ((blk >= cur - 1) & (blk <= cur))
    valid = blk * SEL_BLOCK <= qpos
    score = jnp.where(forced, SEL_BIG, jnp.where(valid, imp, -SEL_BIG))
    score = jnp.where(blk < n_blk, score, -jnp.inf)
    sel = jnp.zeros((n_kv, LANES), F32)
    for _ in range(min(N_SELECT, n_blk)):
        mx = jnp.max(score, axis=-1, keepdims=True)
        idx = jnp.min(jnp.where(score == mx, blk, LANES), axis=-1, keepdims=True)
        hit = blk == idx
        sel = jnp.where(hit, 1.0, sel)
        score = jnp.where(hit, -jnp.inf, score)
    h_of_g = lax.broadcasted_iota(jnp.int32, (n_heads, n_kv), 0) // group
    gsel_t = jnp.where(h_of_g == lax.broadcasted_iota(jnp.int32, (n_heads, n_kv), 1), 1.0, 0.0)
    madd_blk = (jnp.dot(gsel_t, sel, preferred_element_type=F32) - 1.0) * -NEG_INF

    bpp = page // SEL_BLOCK
    lane = lax.broadcasted_iota(jnp.int32, (1, page), 1)
    st = init
    for p in range(n_pages):
        kv = page_refs[p][0]
        madd = madd_blk[:, p * bpp + bpp - 1:p * bpp + bpp]
        for j in range(bpp - 2, -1, -1):
            madd = jnp.where(lane < (j + 1) * SEL_BLOCK, madd_blk[:, p * bpp + j:p * bpp + j + 1], madd)
        s = lax.dot_general(qbd, kv[:, :gd].astype(BF16), NT_DIMS, preferred_element_type=F32)
        st = update(s + bs_ref[:, p * page:(p + 1) * page] + madd, kv[:, gd:].astype(BF16), st)
    st = update_row(bn_ref[...] + madd_blk[:, cur:cur + 1], new[:, 2 * gd:3 * gd], new[:, 3 * gd:4 * gd], st)

    stw = init
    n_buf = win_ref.shape[1]
    step = min(n_buf, 4 * LANES)
    for c in range(n_buf // step):
        w = win_ref[0, c * step:(c + 1) * step, :]
        s = lax.dot_general(qbd, w[:, :gd].astype(BF16), NT_DIMS, preferred_element_type=F32)
        stw = update(s + bw_ref[:, c * step:(c + 1) * step], w[:, gd:].astype(BF16), stw)
    stw = update_row(bn_ref[...], new[:, 4 * gd:5 * gd], new[:, 5 * gd:6 * gd], stw)

    gt = gt_ref[0]
    o = gt[:, 0:1] * o_c + gt[:, 1:2] * finish(st) + gt[:, 2:3] * finish(stw)
    hg = lax.broadcasted_iota(jnp.int32, (n_heads, 1), 0) // group
    o_ref[0] = sum(jnp.where(hg == g, o[:, g * dh:(g + 1) * dh], 0.0) for g in range(n_kv))


def _nsa_dec(page_table, qbd, gates, kct, vcf, bias_c, bias_s, bias_w, bias_n, cache_win2, kv_new, cache2,
             *, group, n_kv, dh, cmp_block, qpos):
    bsz, n_heads, gd = qbd.shape
    n_pages = page_table.shape[1]
    page = cache2.shape[1]
    n_buf = cache_win2.shape[1]
    assert page % SEL_BLOCK == 0 and n_pages * page == qpos and n_buf % min(n_buf, 4 * LANES) == 0
    full = lambda a: pl.BlockSpec(a.shape, lambda b, pt: (0,) * a.ndim)
    per_b = lambda a: pl.BlockSpec((1,) + a.shape[1:], lambda b, pt: (b,) + (0,) * (a.ndim - 1))
    page_specs = [pl.BlockSpec((1, page, 2 * gd), functools.partial(lambda b, pt, p: (pt[b, p], 0, 1), p=p))
                  for p in range(n_pages)]
    grid_spec = pltpu.PrefetchScalarGridSpec(
        num_scalar_prefetch=1,
        grid=(bsz,),
        in_specs=[per_b(qbd), per_b(gates), per_b(kct), per_b(vcf), full(bias_c), full(bias_s), full(bias_w), full(bias_n),
                  per_b(cache_win2), per_b(kv_new), *page_specs],
        out_specs=pl.BlockSpec((1, n_heads, dh), lambda b, pt: (b, 0, 0)),
    )
    kern = functools.partial(_nsa_dec_kernel, n_pages=n_pages, group=group, n_kv=n_kv, dh=dh, cmp_block=cmp_block, qpos=qpos)
    return pl.pallas_call(
        kern,
        grid_spec=grid_spec,
        out_shape=jax.ShapeDtypeStruct((bsz, n_heads, dh), F32),
        compiler_params=_cp(("parallel",)),
        name="nsa_dec",
    )(page_table, qbd, gates, kct, vcf, bias_c, bias_s, bias_w, bias_n, cache_win2, kv_new, *([cache2] * n_pages))


def _static_bias(rel_table, dist, valid):
    thr = np.asarray(_rel_thresholds(rel_table.shape[0]))
    bucket = np.searchsorted(thr, np.maximum(dist, 0), side='right') - 1
    return jnp.where(jnp.asarray(valid)[None, :], rel_table.astype(F32).T[:, bucket], NEG_INF)


def _route(rt, n_experts, tm):
    n_tok = rt.shape[0]
    n_as = n_tok * TOP_K
    assert n_as % tm == 0
    n_blocks = n_as // tm
    flat_e = rt[:, TOP_K:2 * TOP_K].astype(jnp.int32).reshape(-1)
    order = jnp.argsort(flat_e)
    se = flat_e[order]
    tok_sorted = (order // TOP_K).astype(jnp.int32)
    dst_sorted = ((order % TOP_K) * n_tok + order // TOP_K).astype(jnp.int32).reshape(n_blocks, tm)
    experts = jnp.arange(n_experts)
    starts = jnp.searchsorted(se, experts, side='left').astype(jnp.int32)
    ends = jnp.searchsorted(se, experts, side='right').astype(jnp.int32)
    first_blk = starts // tm
    n_vis_e = jnp.where(ends > starts, (ends - 1) // tm - first_blk + 1, 0)
    cum = jnp.cumsum(n_vis_e)
    v = jnp.arange(n_blocks + n_experts - 1)
    ve = jnp.minimum(jnp.searchsorted(cum, v, side='right'), n_experts - 1)
    live = v < cum[-1]
    vb = first_blk[ve] + v - (cum[ve] - n_vis_e[ve])
    e_last = jnp.max(jnp.where(ends > starts, experts, 0))
    ve = jnp.where(live, ve, e_last).astype(jnp.int32)
    vb = jnp.where(live, vb, n_blocks - 1).astype(jnp.int32)
    lo = jnp.where(live, starts[ve], 0).astype(jnp.int32)
    hi = jnp.where(live, ends[ve], 0).astype(jnp.int32)
    i = jnp.arange(tm)[None, :]
    row = vb[:, None] * tm + i
    trash = n_as + (v % 2)[:, None] * tm + i
    dst = jnp.where((row >= lo[:, None]) & (row < hi[:, None]), dst_sorted[vb], trash).astype(jnp.int32)
    return tok_sorted, dst[:, None, :], (vb, ve, lo, hi)


def _split_mod(mod, bsz, per_row):
    parts = jnp.split(mod, 6, axis=-1)
    if per_row:
        return [p[None] for p in parts]
    return [p[:, None] for p in parts]


def _trunk(x, mods, kv_mod, conv0, h0, make_attend, p, *, per_row):
    depth = p['w_ada'].shape[0]
    n_a = p['w_in_a'].shape[0]
    bsz, t, d = x.shape
    n_experts = p['w_router'].shape[-1]
    alpha = (2 * depth) ** 0.25
    tt = min(256, t)
    conv_new, h_new = [], []
    attend, kv_state = None, None
    for l in range(depth):
        sh1, sc1, g1, sh2, sc2, g2 = _split_mod(mods[l], bsz, per_row)
        if l < n_a:
            win_bf = p['w_in_a'][l].astype(BF16)
            wg_bf = p['w_gate_a'][l].astype(BF16)
            args = (win_bf, p['w_conv'][l], p['b_conv'][l][None], wg_bf, p['b_gate_a'][l], p['lru_lambda'][l][None])
            if per_row:
                gated, cb, hl = _rglru_step(x[0], sc1[0], sh1[0], conv0[l].swapaxes(0, 1), h0[l], *args)
                gated, cb = gated[None], cb.swapaxes(0, 1)
            else:
                gated, cb, hl = _rglru_seq(x, sc1, sh1, conv0[l], h0[l][:, None], *args, tt=tt)
                hl = hl[:, 0]
            conv_new.append(cb)
            h_new.append(hl)
            mix, wout = gated, p['w_out_a'][l]
        else:
            if l == n_a:
                ksh, ksc = jnp.split(kv_mod, 2, axis=-1)
                ksh, ksc = (ksh[None], ksc[None]) if per_row else (ksh[:, None], ksc[:, None])
                attend, kv_state = make_attend(x, ksc, ksh)
            lb = l - n_a
            mix, wout = attend(x, sc1, sh1, p['w_in_b'][lb]), p['w_out_b'][lb]
        wr_pad = jnp.zeros((d, LANES), F32).at[:, :n_experts].set(p['w_router'][l])
        br_pad = jnp.zeros((1, LANES), F32).at[0, :n_experts].set(p['b_router'][l])
        x1, hin2, rt = _post(mix, x, g1, sc2, sh2, wout.astype(BF16), p['ln_g'][l, 0][None], p['ln_b'][l, 0][None],
                             wr_pad, br_pad, alpha=alpha, n_experts=n_experts, tt=tt)
        n_tok = bsz * t
        rt2 = rt.reshape(n_tok, LANES)
        moe_w = (p['w_gu'][l], p['b_gu'][l], p['w_down'][l], p['b_down'][l])
        if per_row:
            ff = _moe_dense(hin2.reshape(n_tok, d), rt2, *moe_w).reshape(bsz, t, d)
            x = _final(x1, ff, rt, g2, p['ln_g'][l, 1][None], p['ln_b'][l, 1][None], alpha=alpha, combine=False, tt=tt)
        else:
            tok_sorted, dst, visits = _route(rt2, n_experts, MOE_ROWS)
            xs = hin2.reshape(n_tok, d)[tok_sorted]
            yg = _moe_grouped(visits, dst, xs, *moe_w, tm=MOE_ROWS, trash0=TOP_K * n_tok)
            x = _final(x1, yg, rt, g2, p['ln_g'][l, 1][None], p['ln_b'][l, 1][None], alpha=alpha, combine=True, tt=tt)
    return x, jnp.stack(conv_new), jnp.stack(h_new), kv_state


def _qg_weight_t(w_in, n_kv, group, dh):
    d = w_in.shape[0]
    hd = n_kv * group * dh
    wg = w_in[:, hd:].reshape(d, n_kv, 3 * group)
    wg = jnp.pad(wg, ((0, 0), (0, 0), (0, 16 - 3 * group))).reshape(d, n_kv * 16)
    return jnp.concatenate([w_in[:, :hd], wg], axis=1).T.astype(BF16)


def _toeplitz_bias(rel_table, tq):
    n_buckets = rel_table.shape[0]
    thr = np.asarray(_rel_thresholds(n_buckets))
    kk = np.arange(tq)[:, None]
    ii = np.arange(tq)[None, :]
    dist = np.maximum(np.stack([ii - kk, tq + ii - kk]), 0)
    bucket = np.searchsorted(thr, dist, side='right') - 1
    return rel_table.astype(F32).T[:, bucket]


def kernel(x_prompt, x_sample, c_prompt, c_sample, state_conv, state_h, cache_kv, cache_win, page_table, w_ada, b_ada, ln_g, ln_b, w_in_a, w_conv, b_conv, w_gate_a, b_gate_a, lru_lambda, w_out_a, w_ada_kv, b_ada_kv, w_kv, cmp_w1, cmp_b1, cmp_w2, cmp_b2, cmp_pe, w_in_b, w_out_b, rel_table, w_router, b_router, w_gu, b_gu, w_down, b_down):
    p = dict(w_ada=w_ada, b_ada=b_ada, ln_g=ln_g, ln_b=ln_b, w_in_a=w_in_a, w_conv=w_conv, b_conv=b_conv,
             w_gate_a=w_gate_a, b_gate_a=b_gate_a, lru_lambda=lru_lambda, w_out_a=w_out_a, w_kv=w_kv,
             w_in_b=w_in_b, w_out_b=w_out_b, w_router=w_router, b_router=b_router, w_gu=w_gu, b_gu=b_gu,
             w_down=w_down, b_down=b_down)
    bsz, seq, d = x_prompt.shape
    dec_b = x_sample.shape[0]
    n_a = w_in_a.shape[0]
    cw = w_conv.shape[1]
    lru_w = w_conv.shape[2]
    n_kv, dh = cache_kv.shape[3], cache_kv.shape[4]
    n_heads = rel_table.shape[1]
    group = n_heads // n_kv
    hd = n_heads * dh
    gd = n_kv * dh
    n_parts = w_kv.shape[1] // gd
    assert n_parts == 6 and 3 * group <= 16
    cmp_block = cmp_pe.shape[1]
    scale = dh ** -0.5
    assert math.log2(scale) == round(math.log2(scale))

    c_all = jnp.concatenate([c_prompt, c_sample], axis=0)
    mods = _linear(c_all, w_ada, b_ada[:, None, :], act_in='silu')
    kv_mod = _linear(c_all, w_ada_kv[None], b_ada_kv[None, None, :], act_in='silu')[0]
    tab = rel_table.astype(F32).reshape(-1)

    def cmp_kv(k_head, v_head, length):
        kc = _compress(k_head, length, cmp_w1[0], cmp_b1[0], cmp_w2[0], cmp_b2[0], cmp_pe[0])
        vc = _compress(v_head, length, cmp_w1[1], cmp_b1[1], cmp_w2[1], cmp_b2[1], cmp_pe[1])
        return kc, vc

    def make_prompt_attend(x, ksc, ksh):
        tq = min(NSA_TILE, seq)
        wvt = jnp.concatenate([w_kv[:, 3 * gd:4 * gd], w_kv[:, 5 * gd:6 * gd]], axis=1).T.astype(BF16)
        kv, kk, vt = _kvproj(x, ksc, ksh, w_kv.astype(BF16), wvt, n_kv=n_kv, dh=dh, k_parts=(2, 4), tt=tq)
        kv6 = kv.reshape(bsz, seq, n_parts, n_kv, dh)
        kc, vc = cmp_kv(kv6[:, :cmp_block, 0], kv6[:, :cmp_block, 1], seq)
        n_cmp = kc.shape[1]
        ncp = -(-n_cmp // SUBLANES_BF16) * SUBLANES_BF16
        padc = lambda a: jnp.pad(a, ((0, 0), (0, ncp - n_cmp), (0, 0), (0, 0))).astype(BF16)
        kc_g = padc(kc).transpose(0, 2, 1, 3)
        vc_t = padc(vc).transpose(0, 2, 3, 1)
        toe = _toeplitz_bias(rel_table, tq)

        def attend(x, sc1, sh1, w_in):
            qt, gt = _qproj(x, sc1, sh1, _qg_weight_t(w_in, n_kv, group, dh), hd=hd, scale=scale, tt=tq)
            return _nsa_seq(tab, qt, gt, kc_g, vc_t, kk, vt, toe, group=group, dh=dh, cmp_block=cmp_block,
                            n_cmp_real=n_cmp, tq=tq)

        n_win = min(WINDOW, seq)
        return attend, (kv6[:, :, :4], kv6[:, seq - n_win:, 4:])

    def make_sample_attend(x, ksc, ksh):
        kv = _modlinear(x, ksc, ksh, w_kv.astype(BF16), tt=dec_b)
        kv6 = kv.reshape(dec_b, 1, n_parts, n_kv, dh)
        page = cache_kv.shape[1]
        past_len = page_table.shape[1] * page
        n_buf = cache_win.shape[1]
        assert page >= cmp_block and past_len >= cmp_block
        head = cache_kv[page_table[:, 0], :cmp_block]
        kc, vc = cmp_kv(head[:, :, 0], head[:, :, 1], past_len + 1)
        n_cmp = kc.shape[1]
        ncp = -(-n_cmp // SUBLANES_BF16) * SUBLANES_BF16
        padc = lambda a: jnp.pad(a, ((0, 0), (0, ncp - n_cmp), (0, 0), (0, 0))).astype(BF16).reshape(dec_b, ncp, gd)
        kct, vcf = padc(kc).transpose(0, 2, 1), padc(vc)
        c = np.arange(ncp)
        dist_c = past_len - (c * CMP_STRIDE + cmp_block - 1)
        bias_c = _static_bias(rel_table, dist_c, (dist_c >= 0) & (c < n_cmp))
        bias_s = _static_bias(rel_table, past_len - np.arange(past_len), np.ones(past_len, bool))
        dist_w = n_buf - np.arange(n_buf)
        bias_w = _static_bias(rel_table, dist_w, (dist_w <= WINDOW) & (past_len - dist_w >= 0))
        bias_n = rel_table.astype(F32)[0][:, None]
        cache2 = cache_kv.reshape(cache_kv.shape[0], page, 4 * gd)
        cache_win2 = cache_win.reshape(dec_b, n_buf, 2 * gd)
        win = jnp.concatenate([cache_win, kv6[:, :, 4:]], axis=1)
        own_group = (jnp.arange(n_heads)[:, None] // group == jnp.arange(n_kv)[None, :]).astype(BF16)

        def attend(x, sc1, sh1, w_in):
            qt, gt = _qproj(x, sc1, sh1, _qg_weight_t(w_in, n_kv, group, dh), hd=hd, scale=scale, tt=dec_b)
            q = qt[0].T.reshape(dec_b, n_heads, dh)
            qbd = (q[:, :, None, :] * own_group[None, :, :, None]).reshape(dec_b, n_heads, gd)
            gates = gt[0].reshape(n_kv, 16, dec_b)[:, :3 * group].transpose(2, 0, 1).reshape(dec_b, n_heads, 3)
            gates = jnp.pad(gates, ((0, 0), (0, 0), (0, LANES - 3)))
            o = _nsa_dec(page_table, qbd, gates, kct, vcf, bias_c, bias_s, bias_w, bias_n, cache_win2,
                         kv.reshape(dec_b, 1, n_parts * gd), cache2, group=group, n_kv=n_kv, dh=dh, cmp_block=cmp_block, qpos=past_len)
            return o.reshape(1, dec_b, hd)

        return attend, (kv6[:, :, :4], win[:, -n_buf:])

    conv0 = jnp.zeros((n_a, bsz, cw - 1, lru_w), F32)
    h0 = jnp.zeros((n_a, bsz, lru_w), F32)
    y_prompt, prompt_conv, prompt_h, (prompt_kv, prompt_win) = _trunk(
        x_prompt, mods[:, :bsz], kv_mod[:bsz], conv0, h0, make_prompt_attend, p, per_row=False)
    y_sample, sample_conv, sample_h, (sample_kv, sample_win) = _trunk(
        x_sample.reshape(1, dec_b, d), mods[:, bsz:], kv_mod[bsz:], state_conv, state_h, make_sample_attend, p, per_row=True)
    return (y_prompt, y_sample.reshape(dec_b, 1, d), prompt_conv, prompt_h, prompt_kv, prompt_win,
            sample_conv, sample_h, sample_kv, sample_win)
```

```python
import functools
import math

import numpy as np
import jax
import jax.numpy as jnp
from jax import lax
from jax.experimental import pallas as pl
from jax.experimental.pallas import tpu as pltpu

F32 = jnp.float32
BF16 = jnp.bfloat16
HIGHEST = lax.Precision.HIGHEST
NT_DIMS = (((1,), (1,)), ((), ()))

CMP_STRIDE = 16
SEL_BLOCK = 64
N_SELECT = 16
WINDOW = 512
REL_MAX_DIST = 128
TOP_K = 4
RGLRU_C = 8.0
SWIGLU_LIMIT = 7.0
SWIGLU_ALPHA = 1.702
MOE_ROWS = 256
LN_EPS = 1e-5
SEL_BIG = 1e9
NEG_INF = -1e30
M_INIT = -1e29

LANES = 128
SUBLANES_BF16 = 16
VMEM_LIMIT = 48 * 1024 * 1024
NSA_TILE = 256


def _cp(sem, vmem=VMEM_LIMIT):
    return pltpu.CompilerParams(dimension_semantics=sem, vmem_limit_bytes=vmem)


def _rel_thresholds(n_buckets):
    exact = n_buckets // 2
    d_max = REL_MAX_DIST + 1
    buckets = []
    for d in range(d_max + 1):
        if d < exact:
            buckets.append(d)
        else:
            large = exact + int(math.log(max(d, 1) / exact) / math.log(REL_MAX_DIST / exact) * (n_buckets - exact))
            buckets.append(min(large, n_buckets - 1))
    return [next(d for d in range(d_max + 1) if buckets[d] >= b) for b in range(n_buckets)]


def _linear_kernel(x_ref, w_ref, b_ref, o_ref, *, act_in):
    x = x_ref[...]
    if act_in == 'silu':
        x = x * jax.nn.sigmoid(x)
    elif act_in == 'gelu':
        x = jax.nn.gelu(x)
    y = jnp.dot(x.astype(BF16), w_ref[0].astype(BF16), preferred_element_type=F32)
    o_ref[0] = y + b_ref[0]


def _linear(x, w, b, *, act_in=None, tn=512):
    m, k = x.shape
    nl, _, n = w.shape
    tn = min(tn, n)
    return pl.pallas_call(
        functools.partial(_linear_kernel, act_in=act_in),
        grid=(nl, n // tn),
        in_specs=[
            pl.BlockSpec((m, k), lambda l, j: (0, 0)),
            pl.BlockSpec((1, k, tn), lambda l, j: (l, 0, j)),
            pl.BlockSpec((1, 1, tn), lambda l, j: (l, 0, j)),
        ],
        out_specs=pl.BlockSpec((1, m, tn), lambda l, j: (l, 0, j)),
        out_shape=jax.ShapeDtypeStruct((nl, m, n), F32),
        compiler_params=_cp(("parallel", "parallel")),
        name="linear",
    )(x, w, b)


def _mod_spec(mod, tt):
    d = mod.shape[-1]
    if mod.shape[1] == 1:
        return pl.BlockSpec((1, 1, d), lambda b, i: (b, 0, 0))
    return pl.BlockSpec((1, tt, d), lambda b, i: (b, i, 0))


def _modlinear_kernel(x_ref, sc_ref, sh_ref, w_ref, o_ref):
    hin = x_ref[0] * (1.0 + sc_ref[0]) + sh_ref[0]
    o_ref[0] = jnp.dot(hin.astype(BF16), w_ref[...], preferred_element_type=F32)


def _modlinear(x, sc, sh, w_bf, *, tt):
    bsz, t, d = x.shape
    n = w_bf.shape[1]
    return pl.pallas_call(
        _modlinear_kernel,
        grid=(bsz, t // tt),
        in_specs=[
            pl.BlockSpec((1, tt, d), lambda b, i: (b, i, 0)),
            _mod_spec(sc, tt),
            _mod_spec(sh, tt),
            pl.BlockSpec((d, n), lambda b, i: (0, 0)),
        ],
        out_specs=pl.BlockSpec((1, tt, n), lambda b, i: (b, i, 0)),
        out_shape=jax.ShapeDtypeStruct((bsz, t, n), F32),
        compiler_params=_cp(("parallel", "parallel")),
        name="modlinear",
    )(x, sc, sh, w_bf)


def _kvproj_kernel(x_ref, sc_ref, sh_ref, w_ref, wvt_ref, kv_ref, kk_ref, vt_ref, *, n_kv, dh, k_parts):
    hin = (x_ref[0] * (1.0 + sc_ref[0]) + sh_ref[0]).astype(BF16)
    y = jnp.dot(hin, w_ref[...], preferred_element_type=F32)
    kv_ref[0] = y
    gd = n_kv * dh
    for j, part in enumerate(k_parts):
        for g in range(n_kv):
            kk_ref[j, 0, g] = y[:, part * gd + g * dh:part * gd + (g + 1) * dh].astype(BF16)
    yt = lax.dot_general(wvt_ref[...], hin, NT_DIMS, preferred_element_type=F32)
    for j in range(vt_ref.shape[1]):
        vt_ref[0, j, 0] = yt[j * dh:(j + 1) * dh].astype(BF16)


def _kvproj(x, sc, sh, w_bf, wvt_bf, *, n_kv, dh, k_parts, tt):
    bsz, t, d = x.shape
    n = w_bf.shape[1]
    nv = wvt_bf.shape[0] // dh
    return pl.pallas_call(
        functools.partial(_kvproj_kernel, n_kv=n_kv, dh=dh, k_parts=k_parts),
        grid=(bsz, t // tt),
        in_specs=[
            pl.BlockSpec((1, tt, d), lambda b, i: (b, i, 0)),
            _mod_spec(sc, tt),
            _mod_spec(sh, tt),
            pl.BlockSpec((d, n), lambda b, i: (0, 0)),
            pl.BlockSpec((nv * dh, d), lambda b, i: (0, 0)),
        ],
        out_specs=[
            pl.BlockSpec((1, tt, n), lambda b, i: (b, i, 0)),
            pl.BlockSpec((len(k_parts), 1, n_kv, tt, dh), lambda b, i: (0, b, 0, i, 0)),
            pl.BlockSpec((1, nv, 1, dh, tt), lambda b, i: (b, 0, i, 0, 0)),
        ],
        out_shape=[
            jax.ShapeDtypeStruct((bsz, t, n), F32),
            jax.ShapeDtypeStruct((len(k_parts), bsz, n_kv, t, dh), BF16),
            jax.ShapeDtypeStruct((bsz, nv, t // tt, dh, tt), BF16),
        ],
        compiler_params=_cp(("parallel", "parallel")),
        name="kvproj",
    )(x, sc, sh, w_bf, wvt_bf)


def _qproj_kernel(x_ref, sc_ref, sh_ref, wt_ref, qt_ref, gt_ref, *, hd, scale):
    hin = (x_ref[0] * (1.0 + sc_ref[0]) + sh_ref[0]).astype(BF16)
    yt = lax.dot_general(wt_ref[...], hin, NT_DIMS, preferred_element_type=F32)
    qt_ref[0] = (yt[:hd] * scale).astype(BF16)
    gt_ref[0] = jax.nn.sigmoid(yt[hd:])


def _qproj(x, sc, sh, wt_bf, *, hd, scale, tt):
    bsz, t, d = x.shape
    n = wt_bf.shape[0]
    return pl.pallas_call(
        functools.partial(_qproj_kernel, hd=hd, scale=scale),
        grid=(bsz, t // tt),
        in_specs=[
            pl.BlockSpec((1, tt, d), lambda b, i: (b, i, 0)),
            _mod_spec(sc, tt),
            _mod_spec(sh, tt),
            pl.BlockSpec((n, d), lambda b, i: (0, 0)),
        ],
        out_specs=[
            pl.BlockSpec((1, hd, tt), lambda b, i: (b, 0, i)),
            pl.BlockSpec((1, n - hd, tt), lambda b, i: (b, 0, i)),
        ],
        out_shape=[
            jax.ShapeDtypeStruct((bsz, hd, t), BF16),
            jax.ShapeDtypeStruct((bsz, n - hd, t), F32),
        ],
        compiler_params=_cp(("parallel", "parallel")),
        name="qproj",
    )(x, sc, sh, wt_bf)


def _log1p(x):
    u = 1.0 + x
    return jnp.where(u == 1.0, x, jnp.log(u) * x / jnp.where(u == 1.0, 1.0, u - 1.0))


def _expm1(x):
    u = jnp.exp(x)
    safe = (u != 1.0) & (u > 0.0)
    return jnp.where(u == 1.0, x, jnp.where(u > 0.0, (u - 1.0) * x / jnp.where(safe, jnp.log(u), 1.0), -1.0))


def _softplus(z):
    return jnp.maximum(z, 0.0) + _log1p(jnp.exp(-jnp.abs(z)))


def _rglru_gates(xc, wg_ref, bg_ref, lam_ref, n_heads):
    w = xc.shape[1]
    bw = w // n_heads
    xcb = xc.astype(BF16)
    gates = []
    for g in range(2):
        cols = [jnp.dot(xcb[:, i * bw:(i + 1) * bw], wg_ref[g, i], preferred_element_type=F32) for i in range(n_heads)]
        gates.append(jnp.concatenate(cols, axis=1) + bg_ref[g:g + 1, :])
    r = jax.nn.sigmoid(gates[0])
    i_g = jax.nn.sigmoid(gates[1])
    log_a = (-RGLRU_C * _softplus(-lam_ref[...])) * r
    a = jnp.exp(log_a)
    b_in = jnp.sqrt(-_expm1(2.0 * log_a)) * (i_g * xc)
    return a, b_in


def _rglru_seq_kernel(x_ref, sc_ref, sh_ref, conv0_ref, h0_ref, win_ref, wconv_ref, bconv_ref, wg_ref, bg_ref, lam_ref,
                      gated_ref, convout_ref, hout_ref, ext_ref, a_ref, b_ref, yb_ref, h_ref, *, tt, n_heads, cw):
    t = pl.program_id(1)
    w = a_ref.shape[1]
    pad = 8

    @pl.when(t == 0)
    def _():
        ext_ref[0:pad - (cw - 1), :] = jnp.zeros((pad - (cw - 1), w), F32)
        ext_ref[pad - (cw - 1):pad, :] = conv0_ref[0]
        h_ref[...] = h0_ref[0]

    hin = x_ref[0] * (1.0 + sc_ref[0]) + sh_ref[0]
    u = jnp.dot(hin.astype(BF16), win_ref[...], preferred_element_type=F32)
    yb_ref[...] = jax.nn.gelu(u[:, :w])
    ext_ref[pad:pad + tt, :] = u[:, w:]
    xc = bconv_ref[...] + sum(ext_ref[pad - (cw - 1) + k:pad - (cw - 1) + k + tt, :] * wconv_ref[k:k + 1, :] for k in range(cw))
    tail = ext_ref[tt:tt + pad, :]
    ext_ref[0:pad, :] = tail

    a, b_in = _rglru_gates(xc, wg_ref, bg_ref, lam_ref, n_heads)
    a_ref[...] = a
    b_ref[...] = b_in

    row = lax.broadcasted_iota(jnp.int32, (tt, LANES), 0)
    for c in range(w // LANES):
        cs = slice(c * LANES, (c + 1) * LANES)
        av = a_ref[:, cs]
        bv = b_ref[:, cs]
        s = 1
        while s < tt:
            keep = row >= s
            b_sh = jnp.where(keep, pltpu.roll(bv, s, 0), 0.0)
            a_sh = jnp.where(keep, pltpu.roll(av, s, 0), 1.0)
            bv = av * b_sh + bv
            av = av * a_sh
            s *= 2
        hs = av * h_ref[:, cs] + bv
        h_ref[:, cs] = hs[tt - 1:tt, :]
        gated_ref[0, :, cs] = (hs * yb_ref[:, cs]).astype(gated_ref.dtype)

    @pl.when(t == pl.num_programs(1) - 1)
    def _():
        convout_ref[0] = tail[pad - (cw - 1):pad, :]
        hout_ref[0] = h_ref[...]


def _rglru_seq(x, sc, sh, conv0, h0, win_bf, wconv, bconv, wg_bf, bg, lam, *, tt):
    bsz, t, d = x.shape
    cw, w = wconv.shape
    n_heads = wg_bf.shape[1]
    full = lambda shape: pl.BlockSpec(shape, lambda b, i: (0,) * len(shape))
    return pl.pallas_call(
        functools.partial(_rglru_seq_kernel, tt=tt, n_heads=n_heads, cw=cw),
        grid=(bsz, t // tt),
        in_specs=[
            pl.BlockSpec((1, tt, d), lambda b, i: (b, i, 0)),
            _mod_spec(sc, tt),
            _mod_spec(sh, tt),
            pl.BlockSpec((1, cw - 1, w), lambda b, i: (b, 0, 0)),
            pl.BlockSpec((1, 1, w), lambda b, i: (b, 0, 0)),
            full((d, 2 * w)),
            full((cw, w)),
            full((1, w)),
            full(wg_bf.shape),
            full((2, w)),
            full((1, w)),
        ],
        out_specs=[
            pl.BlockSpec((1, tt, w), lambda b, i: (b, i, 0)),
            pl.BlockSpec((1, cw - 1, w), lambda b, i: (b, 0, 0)),
            pl.BlockSpec((1, 1, w), lambda b, i: (b, 0, 0)),
        ],
        out_shape=[
            jax.ShapeDtypeStruct((bsz, t, w), BF16),
            jax.ShapeDtypeStruct((bsz, cw - 1, w), F32),
            jax.ShapeDtypeStruct((bsz, 1, w), F32),
        ],
        scratch_shapes=[
            pltpu.VMEM((tt + 8, w), F32),
            pltpu.VMEM((tt, w), F32),
            pltpu.VMEM((tt, w), F32),
            pltpu.VMEM((tt, w), F32),
            pltpu.VMEM((1, w), F32),
        ],
        compiler_params=_cp(("parallel", "arbitrary")),
        name="rglru_seq",
    )(x, sc, sh, conv0, h0, win_bf, wconv, bconv, wg_bf, bg, lam)


def _rglru_step_kernel(x_ref, sc_ref, sh_ref, conv_ref, h0_ref, win_ref, wconv_ref, bconv_ref, wg_ref, bg_ref, lam_ref,
                       gated_ref, convout_ref, hout_ref, *, n_heads, cw):
    w = h0_ref.shape[1]
    hin = x_ref[...] * (1.0 + sc_ref[...]) + sh_ref[...]
    u = jnp.dot(hin.astype(BF16), win_ref[...], preferred_element_type=F32)
    yb = jax.nn.gelu(u[:, :w])
    xb = u[:, w:]
    taps = [conv_ref[k] for k in range(cw - 1)] + [xb]
    xc = bconv_ref[...] + sum(taps[k] * wconv_ref[k:k + 1, :] for k in range(cw))
    a, b_in = _rglru_gates(xc, wg_ref, bg_ref, lam_ref, n_heads)
    h = a * h0_ref[...] + b_in
    gated_ref[...] = (h * yb).astype(gated_ref.dtype)
    for k in range(cw - 1):
        convout_ref[k] = taps[k + 1]
    hout_ref[...] = h


def _rglru_step(x, sc, sh, conv, h0, win_bf, wconv, bconv, wg_bf, bg, lam):
    r, _ = x.shape
    cw, w = wconv.shape
    n_heads = wg_bf.shape[1]
    return pl.pallas_call(
        functools.partial(_rglru_step_kernel, n_heads=n_heads, cw=cw),
        out_shape=[
            jax.ShapeDtypeStruct((r, w), BF16),
            jax.ShapeDtypeStruct((cw - 1, r, w), F32),
            jax.ShapeDtypeStruct((r, w), F32),
        ],
        compiler_params=pltpu.CompilerParams(vmem_limit_bytes=VMEM_LIMIT),
        name="rglru_step",
    )(x, sc, sh, conv, h0, win_bf, wconv, bconv, wg_bf, bg, lam)


def _layer_norm(v, g, b):
    mu = jnp.mean(v, axis=-1, keepdims=True)
    dv = v - mu
    var = jnp.mean(dv * dv, axis=-1, keepdims=True)
    return dv * lax.rsqrt(var + LN_EPS) * g + b


def _post_kernel(a_ref, x_ref, g1_ref, sc2_ref, sh2_ref, wout_ref, lng_ref, lnb_ref, wr_ref, br_ref,
                 x1_ref, hin2_ref, rt_ref, *, alpha, n_experts):
    y = jnp.dot(a_ref[0].astype(BF16), wout_ref[...], preferred_element_type=F32)
    x1 = _layer_norm(alpha * x_ref[0] + (1.0 + g1_ref[0]) * y, lng_ref[...], lnb_ref[...])
    x1_ref[0] = x1
    hin2 = x1 * (1.0 + sc2_ref[0]) + sh2_ref[0]
    hin2_ref[0] = hin2.astype(hin2_ref.dtype)
    logits = jnp.dot(hin2.astype(BF16), wr_ref[...].astype(BF16), preferred_element_type=F32) + br_ref[...]
    lane = lax.broadcasted_iota(jnp.int32, logits.shape, 1)
    work = jnp.where(lane < n_experts, logits, -jnp.inf)
    vals, idxs = [], []
    for _ in range(TOP_K):
        m = jnp.max(work, axis=-1, keepdims=True)
        idx = jnp.min(jnp.where(work == m, lane, LANES), axis=-1, keepdims=True)
        vals.append(m)
        idxs.append(idx)
        work = jnp.where(lane == idx, -jnp.inf, work)
    es = [jnp.exp(v - vals[0]) for v in vals]
    z = sum(es)
    rt = jnp.zeros(logits.shape, F32)
    for k in range(TOP_K):
        rt = jnp.where(lane == k, es[k] / z, rt)
        rt = jnp.where(lane == TOP_K + k, idxs[k].astype(F32), rt)
    rt_ref[0] = rt


def _post(a, x, g1, sc2, sh2, wout_bf, lng, lnb, wr_pad, br_pad, *, alpha, n_experts, tt):
    bsz, t, d = x.shape
    dk = a.shape[-1]
    full = lambda shape: pl.BlockSpec(shape, lambda b, i: (0,) * len(shape))
    row = pl.BlockSpec((1, tt, d), lambda b, i: (b, i, 0))
    return pl.pallas_call(
        functools.partial(_post_kernel, alpha=alpha, n_experts=n_experts),
        grid=(bsz, t // tt),
        in_specs=[
            pl.BlockSpec((1, tt, dk), lambda b, i: (b, i, 0)),
            row,
            _mod_spec(g1, tt),
            _mod_spec(sc2, tt),
            _mod_spec(sh2, tt),
            full((dk, d)),
            full((1, d)),
            full((1, d)),
            full((d, LANES)),
            full((1, LANES)),
        ],
        out_specs=[row, row, pl.BlockSpec((1, tt, LANES), lambda b, i: (b, i, 0))],
        out_shape=[
            jax.ShapeDtypeStruct((bsz, t, d), F32),
            jax.ShapeDtypeStruct((bsz, t, d), F32),
            jax.ShapeDtypeStruct((bsz, t, LANES), F32),
        ],
        compiler_params=_cp(("parallel", "parallel")),
        name="post",
    )(a, x, g1, sc2, sh2, wout_bf, lng, lnb, wr_pad, br_pad)


def _swiglu(g, u):
    g = jnp.minimum(g, SWIGLU_LIMIT)
    u = jnp.clip(u, -SWIGLU_LIMIT, SWIGLU_LIMIT)
    return g * jax.nn.sigmoid(SWIGLU_ALPHA * g) * (u + 1.0)


def _moe_grouped_kernel(vb_ref, ve_ref, lo_ref, hi_ref, dst_ref, xs_ref, wgu_ref, bgu_ref, wdn_ref, bdn_ref, yg_ref,
                        wgu_bf, wdn_bf, stage_ref, sem, *, chunk, tm, trash0):
    v = pl.program_id(0)
    last = pl.num_programs(0) - 1
    d, de2 = wgu_bf.shape
    de = de2 // 2
    e = ve_ref[v]
    pv = jnp.maximum(v - 1, 0)
    slot = v % 2

    def live(u):
        return hi_ref[u] > lo_ref[u]

    def wait_rows(s):
        pltpu.make_async_copy(stage_ref.at[s], yg_ref.at[pl.ds(0, tm)], sem.at[s]).wait()

    @pl.when((v == 0) | (e != ve_ref[pv]))
    def _():
        for c in range(de2 // chunk):
            wgu_bf[:, c * chunk:(c + 1) * chunk] = wgu_ref[0, :, c * chunk:(c + 1) * chunk].astype(BF16)
        for c in range(de // chunk):
            wdn_bf[c * chunk:(c + 1) * chunk, :] = wdn_ref[0, c * chunk:(c + 1) * chunk, :].astype(BF16)

    @pl.when(v == 0)
    def _():
        stage_ref[...] = jnp.zeros(stage_ref.shape, F32)
        for s in range(2):
            fill = pltpu.make_async_copy(stage_ref.at[s], yg_ref.at[pl.ds(trash0 + s * tm, tm)], sem.at[s])
            fill.start()
            fill.wait()

    @pl.when((v >= 2) & live(jnp.maximum(v - 2, 0)))
    def _():
        wait_rows(slot)

    lo = lo_ref[v]
    hi = hi_ref[v]

    @pl.when(hi > lo)
    def _():
        x = xs_ref[...].astype(BF16)
        acc = jnp.zeros((tm, d), F32) + bdn_ref[0]
        for c in range(de // chunk):
            g = jnp.dot(x, wgu_bf[:, c * chunk:(c + 1) * chunk], preferred_element_type=F32) + bgu_ref[0, :, c * chunk:(c + 1) * chunk]
            u = jnp.dot(x, wgu_bf[:, de + c * chunk:de + (c + 1) * chunk], preferred_element_type=F32) + bgu_ref[0, :, de + c * chunk:de + (c + 1) * chunk]
            h = _swiglu(g, u).astype(BF16)
            acc = acc + jnp.dot(h, wdn_bf[c * chunk:(c + 1) * chunk, :], preferred_element_type=F32)
        for s in range(2):
            @pl.when(slot == s)
            def _(s=s):
                stage_ref[s] = acc

                for i in range(tm):
                    pltpu.make_async_copy(stage_ref.at[s, pl.ds(i, 1)], yg_ref.at[pl.ds(dst_ref[0, 0, i], 1)],
                                          sem.at[s]).start()

    @pl.when(v == last)
    def _():
        @pl.when(hi > lo)
        def _():
            wait_rows(slot)

        @pl.when((v >= 1) & live(pv))
        def _():
            wait_rows(1 - slot)


def _moe_grouped(visits, dst, xs, w_gu, b_gu, w_dn, b_dn, *, tm, trash0, e_off):
    n_rows, d = xs.shape
    n_e, _, de2 = w_gu.shape
    de = de2 // 2
    n_vis = visits[0].shape[0]
    grid_spec = pltpu.PrefetchScalarGridSpec(
        num_scalar_prefetch=4,
        grid=(n_vis,),
        in_specs=[
            pl.BlockSpec((1, 1, tm), lambda v, vb, ve, lo, hi: (v, 0, 0), memory_space=pltpu.SMEM),
            pl.BlockSpec((tm, d), lambda v, vb, ve, lo, hi: (vb[v], 0)),
            pl.BlockSpec((1, d, de2), lambda v, vb, ve, lo, hi: (ve[v] + e_off, 0, 0)),
            pl.BlockSpec((1, 1, de2), lambda v, vb, ve, lo, hi: (ve[v] + e_off, 0, 0)),
            pl.BlockSpec((1, de, d), lambda v, vb, ve, lo, hi: (ve[v] + e_off, 0, 0)),
            pl.BlockSpec((1, 1, d), lambda v, vb, ve, lo, hi: (ve[v] + e_off, 0, 0)),
        ],
        out_specs=pl.BlockSpec(memory_space=pl.ANY),
        scratch_shapes=[
            pltpu.VMEM((d, de2), BF16),
            pltpu.VMEM((de, d), BF16),
            pltpu.VMEM((2, tm, d), F32),
            pltpu.SemaphoreType.DMA((2,)),
        ],
    )
    return pl.pallas_call(
        functools.partial(_moe_grouped_kernel, chunk=min(512, de), tm=tm, trash0=trash0),
        grid_spec=grid_spec,
        out_shape=jax.ShapeDtypeStruct((trash0 + 2 * tm, d), F32),
        compiler_params=_cp(("arbitrary",), 56 * 1024 * 1024),
        name="moe_grouped",
    )(*visits, dst, xs, w_gu, b_gu.reshape(n_e, 1, de2), w_dn, b_dn.reshape(n_e, 1, d))


def _moe_dense_kernel(x_ref, rt_ref, wg_ref, wu_ref, bg_ref, bu_ref, wdn_ref, bdn_ref, o_ref):
    e = pl.program_id(0)
    c = pl.program_id(1)

    @pl.when((e == 0) & (c == 0))
    def _():
        o_ref[...] = jnp.zeros(o_ref.shape, F32)

    rt = rt_ref[...]
    ef = e.astype(F32)
    gate = sum(jnp.where(rt[:, TOP_K + k:TOP_K + k + 1] == ef, rt[:, k:k + 1], 0.0) for k in range(TOP_K))
    x = x_ref[...].astype(BF16)
    g = jnp.dot(x, wg_ref[0].astype(BF16), preferred_element_type=F32) + bg_ref[0]
    u = jnp.dot(x, wu_ref[0].astype(BF16), preferred_element_type=F32) + bu_ref[0]
    h = _swiglu(g, u).astype(BF16)
    y = jnp.dot(h, wdn_ref[0].astype(BF16), preferred_element_type=F32)
    y = y + jnp.where(c == 0, 1.0, 0.0) * bdn_ref[0]
    o_ref[...] += gate * y


def _moe_dense(x_bf, rt, w_gu, b_gu, w_dn, b_dn, *, n_e, e_off, chunk=256):
    r, d = x_bf.shape
    n_all, _, de2 = w_gu.shape
    de = de2 // 2
    nc = de // chunk
    b_gu3 = b_gu.reshape(n_all, 1, de2)
    return pl.pallas_call(
        _moe_dense_kernel,
        grid=(n_e, nc),
        in_specs=[
            pl.BlockSpec((r, d), lambda e, c: (0, 0)),
            pl.BlockSpec((r, LANES), lambda e, c: (0, 0)),
            pl.BlockSpec((1, d, chunk), lambda e, c: (e + e_off, 0, c)),
            pl.BlockSpec((1, d, chunk), lambda e, c: (e + e_off, 0, nc + c)),
            pl.BlockSpec((1, 1, chunk), lambda e, c: (e + e_off, 0, c)),
            pl.BlockSpec((1, 1, chunk), lambda e, c: (e + e_off, 0, nc + c)),
            pl.BlockSpec((1, chunk, d), lambda e, c: (e + e_off, c, 0)),
            pl.BlockSpec((1, 1, d), lambda e, c: (e + e_off, 0, 0)),
        ],
        out_specs=pl.BlockSpec((r, d), lambda e, c: (0, 0)),
        out_shape=jax.ShapeDtypeStruct((r, d), F32),
        compiler_params=_cp(("arbitrary", "arbitrary")),
        name="moe_dense",
    )(x_bf, rt, w_gu, w_gu, b_gu3, b_gu3, w_dn, b_dn.reshape(n_all, 1, d))


def _final_kernel(x_ref, *refs, alpha, combine):
    if combine:
        ff_refs, (rt_ref, g2_ref, lng_ref, lnb_ref, o_ref) = refs[:TOP_K], refs[TOP_K:]
        rt = rt_ref[0]
        ff = sum(rt[:, k:k + 1] * ff_refs[k][...] for k in range(TOP_K))
    else:
        ff_ref, rt_ref, g2_ref, lng_ref, lnb_ref, o_ref = refs
        ff = ff_ref[0]
    o_ref[0] = _layer_norm(alpha * x_ref[0] + (1.0 + g2_ref[0]) * ff, lng_ref[...], lnb_ref[...])


def _final(x1, ff, rt, g2, lng, lnb, *, alpha, combine, tt):
    bsz, t, d = x1.shape
    row = pl.BlockSpec((1, tt, d), lambda b, i: (b, i, 0))
    nblk = bsz * t // tt
    if combine:
        ff_specs = [pl.BlockSpec((tt, d), functools.partial(lambda b, i, k: (k * nblk + b * (t // tt) + i, 0), k=k))
                    for k in range(TOP_K)]
        ffs = [ff] * TOP_K
    else:
        ff_specs, ffs = [row], [ff]
    full = lambda shape: pl.BlockSpec(shape, lambda b, i: (0,) * len(shape))
    return pl.pallas_call(
        functools.partial(_final_kernel, alpha=alpha, combine=combine),
        grid=(bsz, t // tt),
        in_specs=[row, *ff_specs, pl.BlockSpec((1, tt, LANES), lambda b, i: (b, i, 0)), _mod_spec(g2, tt), full((1, d)), full((1, d))],
        out_specs=row,
        out_shape=jax.ShapeDtypeStruct((bsz, t, d), F32),
        compiler_params=_cp(("parallel", "parallel")),
        name="final",
    )(x1, *ffs, rt, g2, lng, lnb)


def _bias_chain(dist, tab_ref, h0, n_heads_tab, group, thr):
    out = [jnp.full(dist.shape, tab_ref[h0 + r], F32) for r in range(group)]
    for b in range(1, len(thr)):
        ind = dist >= thr[b]
        out = [jnp.where(ind, tab_ref[b * n_heads_tab + h0 + r], out[r]) for r in range(group)]
    return out


def _nsa_seq_kernel(tab_ref, qt_ref, gt_ref, kc_ref, vct_ref, ks_ref, vst_ref, kw_ref, vwt_ref, toe_ref, o_ref,
                    sel_ref, flag_ref, ms_ref, accs_ref, mw_ref, accw_ref,
                    *, tq, group, dh, n_heads, thr, cmp_block, n_cmp_real, n_blk):
    g = pl.program_id(1)
    it = pl.program_id(2)
    q0 = it * tq
    h0 = g * group
    n_bkt = len(thr)
    far = thr[-1]
    bpt = tq // SEL_BLOCK
    n_tiles = ks_ref.shape[0] // tq
    assert far <= tq + 1 and WINDOW % tq == 0 and WINDOW >= 2 * tq - 1

    def lanes(parts):
        return jnp.concatenate(parts, axis=1)

    q4t = lanes([qt_ref[0, r * dh:(r + 1) * dh, :] for r in range(group)])
    far_bias = lanes([jnp.full((1, tq), tab_ref[(n_bkt - 1) * n_heads + h0 + r], F32) for r in range(group)])
    qi = lax.broadcasted_iota(jnp.int32, (1, tq), 1)
    ki = lax.broadcasted_iota(jnp.int32, (tq, 1), 0)
    qpos = q0 + qi

    ncp = kc_ref.shape[0]
    cidx = lax.broadcasted_iota(jnp.int32, (ncp, 1), 0)
    dist_c = qpos - (cidx * CMP_STRIDE + (cmp_block - 1))
    madd_c = jnp.where((dist_c >= 0) & (cidx < n_cmp_real), 0.0, NEG_INF)
    bias_c = _bias_chain(dist_c, tab_ref, h0, n_heads, group, thr)
    s_c = jnp.dot(kc_ref[...], q4t, preferred_element_type=F32) + lanes([b + madd_c for b in bias_c])
    m_c = jnp.maximum(jnp.max(s_c, axis=0, keepdims=True), M_INIT)
    e_c = jnp.exp(s_c - m_c)
    z_c = jnp.sum(e_c, axis=0, keepdims=True)
    p_c = e_c / jnp.where(z_c > 0, z_c, 1.0)
    o_c = jnp.dot(vct_ref[...], p_c.astype(BF16), preferred_element_type=F32)

    psum = sum(p_c[:, r * tq:(r + 1) * tq] for r in range(group))
    bj = lax.broadcasted_iota(jnp.int32, (n_blk, ncp), 0) * SEL_BLOCK
    ci = lax.broadcasted_iota(jnp.int32, (n_blk, ncp), 1) * CMP_STRIDE
    overlap_t = jnp.where((ci < bj + SEL_BLOCK) & (ci + cmp_block > bj), 1.0, 0.0)
    imp = jnp.dot(overlap_t, psum, preferred_element_type=F32, precision=HIGHEST)
    blk = lax.broadcasted_iota(jnp.int32, (n_blk, tq), 0)
    cur = qpos // SEL_BLOCK
    forced = (blk == 0) | ((blk >= cur - 1) & (blk <= cur))
    valid = blk * SEL_BLOCK <= qpos
    score = jnp.where(forced, SEL_BIG, jnp.where(valid, imp, -SEL_BIG))
    sel = jnp.full((n_blk, tq), NEG_INF, F32)
    for _ in range(min(N_SELECT, n_blk)):
        mx = jnp.max(score, axis=0, keepdims=True)
        idx = jnp.min(jnp.where(score == mx, blk, n_blk), axis=0, keepdims=True)
        hit = blk == idx
        sel = jnp.where(hit, 0.0, sel)
        score = jnp.where(hit, -jnp.inf, score)
    sel_ref[...] = sel
    for t in range(n_tiles):
        flag_ref[t] = jnp.max(sel[t * bpt:(t + 1) * bpt, :])

    ms_ref[...] = jnp.full(ms_ref.shape, M_INIT, F32)
    mw_ref[...] = jnp.full(mw_ref.shape, M_INIT, F32)
    accs_ref[...] = jnp.zeros(accs_ref.shape, F32)
    accw_ref[...] = jnp.zeros(accw_ref.shape, F32)
    ones = jnp.ones((SUBLANES_BF16, tq), BF16)

    def sel_madd(t):
        b0 = t * bpt
        return jnp.concatenate([jnp.broadcast_to(sel_ref[pl.ds(b0 + j, 1), :], (SEL_BLOCK, tq)) for j in range(bpt)], axis=0)

    def toeplitz(kind):
        return lanes([toe_ref[r, kind] for r in range(group)])

    def attend(k_ref, vt_ref, t, bias, madd, m_ref, acc_ref):
        kt = k_ref[pl.ds(pl.multiple_of(t * tq, tq), tq), :]
        s = jnp.dot(kt, q4t, preferred_element_type=F32) + bias
        if madd is not None:
            s = s + lanes([madd] * group)
        m_old = m_ref[...]
        m_new = jnp.maximum(m_old, jnp.max(s, axis=0, keepdims=True))
        e = jnp.exp(s - m_new).astype(BF16)
        vt = jnp.concatenate([vt_ref[t], ones], axis=0)
        acc_ref[...] = jnp.exp(m_old - m_new) * acc_ref[...] + jnp.dot(vt, e, preferred_element_type=F32)
        m_ref[...] = m_new

    causal = jnp.where(qi >= ki, 0.0, NEG_INF)

    def far_body(t, carry):
        @pl.when(flag_ref[t] > -1.0)
        def _():
            attend(ks_ref, vst_ref, t, far_bias, sel_madd(t), ms_ref, accs_ref)
        return carry

    lax.fori_loop(0, jnp.maximum(it - 1, 0), far_body, 0)

    t_prev = jnp.maximum(it - 1, 0)
    kill_prev = jnp.where(it >= 1, 0.0, NEG_INF)
    attend(ks_ref, vst_ref, t_prev, toeplitz(1), sel_madd(t_prev) + kill_prev, ms_ref, accs_ref)
    attend(ks_ref, vst_ref, it, toeplitz(0), sel_madd(it) + causal, ms_ref, accs_ref)

    for j in range(WINDOW // tq + 1):
        off = WINDOW - j * tq
        t_w = jnp.maximum(it - off // tq, 0)
        madd = None
        if off == 0:
            bias, madd = toeplitz(0), causal
        elif off == tq:
            bias = toeplitz(1)
        else:
            bias = far_bias
            if off + tq - 1 > WINDOW:
                madd = jnp.where(off + qi - ki <= WINDOW, 0.0, NEG_INF)
        if off > 0:
            kill = jnp.where(it >= off // tq, 0.0, NEG_INF)
            madd = kill if madd is None else madd + kill
            if madd.shape != (tq, tq):
                madd = jnp.broadcast_to(madd, (tq, tq))
        attend(kw_ref, vwt_ref, t_w, bias, madd, mw_ref, accw_ref)

    def finish(acc_ref):
        acc = acc_ref[...]
        l = acc[dh:dh + 1, :]
        return acc[:dh, :] / jnp.where(l > 0, l, 1.0)

    def gate(j):
        return lanes([gt_ref[0, r * 3 + j:r * 3 + j + 1, :] for r in range(group)])

    ot = gate(0) * o_c + gate(1) * finish(accs_ref) + gate(2) * finish(accw_ref)
    heads_per_store = LANES // dh
    for r in range(0, group, heads_per_store):
        stacked = jnp.concatenate([ot[:, (r + u) * tq:(r + u + 1) * tq] for u in range(heads_per_store)], axis=0)
        o_ref[0, :, r * dh:(r + heads_per_store) * dh] = stacked.T


def _nsa_seq(tab, qt, gt, kc, vct, kk, vt, toe, *, group, dh, cmp_block, n_cmp_real, tq):
    bsz, hd, s = qt.shape
    n_kv = hd // (group * dh)
    n_heads = n_kv * group
    ncp = kc.shape[2]
    n_tiles = s // tq
    n_blk = -(-s // SEL_BLOCK // 8) * 8
    thr = tuple(_rel_thresholds(tab.shape[0] // n_heads))
    rows = group * tq
    gpad = gt.shape[1] // n_kv
    assert s % tq == 0 and tq % SEL_BLOCK == 0 and dh * (LANES // dh) == LANES and group % (LANES // dh) == 0
    sq = lambda *dims: pl.BlockSpec(tuple(None if d == 0 else d for d in dims[:-1]), dims[-1])
    kern = functools.partial(_nsa_seq_kernel, tq=tq, group=group, dh=dh, n_heads=n_heads, thr=thr, cmp_block=cmp_block,
                             n_cmp_real=n_cmp_real, n_blk=n_blk)
    return pl.pallas_call(
        kern,
        grid=(bsz, n_kv, n_tiles),
        in_specs=[
            pl.BlockSpec(memory_space=pltpu.SMEM),
            pl.BlockSpec((1, group * dh, tq), lambda b, g, i: (b, g, i)),
            pl.BlockSpec((1, gpad, tq), lambda b, g, i: (b, g, i)),
            pl.BlockSpec((None, None, ncp, dh), lambda b, g, i: (b, g, 0, 0)),
            pl.BlockSpec((None, None, dh, ncp), lambda b, g, i: (b, g, 0, 0)),
            pl.BlockSpec((None, None, None, s, dh), lambda b, g, i: (0, b, g, 0, 0)),
            pl.BlockSpec((None, None, n_tiles, dh, tq), lambda b, g, i: (b, g, 0, 0, 0)),
            pl.BlockSpec((None, None, None, s, dh), lambda b, g, i: (1, b, g, 0, 0)),
            pl.BlockSpec((None, None, n_tiles, dh, tq), lambda b, g, i: (b, n_kv + g, 0, 0, 0)),
            pl.BlockSpec((group, 2, tq, tq), lambda b, g, i: (g, 0, 0, 0)),
        ],
        out_specs=pl.BlockSpec((1, tq, group * dh), lambda b, g, i: (b, i, g)),
        out_shape=jax.ShapeDtypeStruct((bsz, s, hd), F32),
        scratch_shapes=[
            pltpu.VMEM((n_blk, tq), F32),
            pltpu.SMEM((n_tiles,), F32),
            pltpu.VMEM((1, rows), F32),
            pltpu.VMEM((dh + SUBLANES_BF16, rows), F32),
            pltpu.VMEM((1, rows), F32),
            pltpu.VMEM((dh + SUBLANES_BF16, rows), F32),
        ],
        compiler_params=_cp(("parallel", "parallel", "arbitrary")),
        name="nsa_seq",
    )(tab, qt, gt, kc, vct, kk, vt, kk, vt, toe)


def _compress(k_head, length, w1, b1, w2, b2, pe):
    bsz, _, n_kv, dh = k_head.shape
    cmp_block = pe.shape[0]
    ratio = cmp_block // CMP_STRIDE
    hidden = w1.shape[-1]
    nc = (length - cmp_block) // CMP_STRIDE + 1
    r = bsz * n_kv
    x = k_head[:, :cmp_block].reshape(bsz, ratio, CMP_STRIDE, n_kv, dh).transpose(1, 0, 3, 2, 4).reshape(ratio, r, CMP_STRIDE * dh)
    w1r = w1.reshape(ratio, CMP_STRIDE * dh, hidden)
    part = [_linear(x[n], w1r, jnp.zeros((ratio, 1, hidden), F32)) for n in range(ratio)]
    const = _linear(pe.reshape(1, -1), w1.reshape(1, cmp_block * dh, hidden), b1.reshape(1, 1, hidden))[0]
    pre = sum(part[u][u:u + nc] for u in range(ratio)) + const
    n_cmp = pre.shape[0]
    out = _linear(pre.reshape(n_cmp * r, hidden), w2[None], b2.reshape(1, 1, dh), act_in='gelu')[0]
    return out.reshape(n_cmp, bsz, n_kv, dh).transpose(1, 0, 2, 3)


def _nsa_dec_kernel(pt_ref, qbd_ref, gt_ref, kct_ref, vcf_ref, bc_ref, bs_ref, bw_ref, bn_ref, win_ref, new_ref, *rest,
                    n_pages, group, n_kv, dh, cmp_block, qpos):
    page_refs, o_ref = rest[:n_pages], rest[n_pages]
    n_heads = group * n_kv
    gd = n_kv * dh
    page = page_refs[0].shape[4]
    qbd = qbd_ref[0]
    new = new_ref[0]

    s_c = jnp.dot(qbd, kct_ref[0], preferred_element_type=F32) + bc_ref[...]
    m_c = jnp.maximum(jnp.max(s_c, axis=-1, keepdims=True), M_INIT)
    e_c = jnp.exp(s_c - m_c)
    z_c = jnp.sum(e_c, axis=-1, keepdims=True)
    p_c = e_c / jnp.where(z_c > 0, z_c, 1.0)
    o_c = jnp.dot(p_c.astype(BF16), vcf_ref[0], preferred_element_type=F32)

    ncp = p_c.shape[1]
    g_of_h = lax.broadcasted_iota(jnp.int32, (n_kv, n_heads), 1) // group
    gsel = jnp.where(g_of_h == lax.broadcasted_iota(jnp.int32, (n_kv, n_heads), 0), 1.0, 0.0)
    psum = jnp.dot(gsel, p_c, preferred_element_type=F32, precision=HIGHEST)
    ci = lax.broadcasted_iota(jnp.int32, (ncp, LANES), 0) * CMP_STRIDE
    bj = lax.broadcasted_iota(jnp.int32, (ncp, LANES), 1) * SEL_BLOCK
    overlap = jnp.where((ci < bj + SEL_BLOCK) & (ci + cmp_block > bj), 1.0, 0.0)
    imp = jnp.dot(psum, overlap, preferred_element_type=F32, precision=HIGHEST)
    n_blk = qpos // SEL_BLOCK + 1
    assert n_blk <= LANES
    cur = qpos // SEL_BLOCK
    blk = lax.broadcasted_iota(jnp.int32, (n_kv, LANES), 1)
    forced = (blk == 0) | ((blk >= cur - 1) & (blk <= cur))
    valid = blk * SEL_BLOCK <= qpos
    score = jnp.where(forced, SEL_BIG, jnp.where(valid, imp, -SEL_BIG))
    score = jnp.where(blk < n_blk, score, -jnp.inf)
    si = lax.broadcasted_iota(jnp.int32, (LANES, LANES), 0)
    li = lax.broadcasted_iota(jnp.int32, (LANES, LANES), 1)
    sel_rows = []
    for g in range(n_kv):
        a = jnp.broadcast_to(score[g:g + 1, :], (LANES, LANES))
        bcol = jnp.sum(jnp.where(si == li, a, 0.0), axis=1, keepdims=True)
        ahead = (a > bcol) | ((a == bcol) & (li < si))
        rank = jnp.sum(jnp.where(ahead, 1.0, 0.0), axis=1, keepdims=True)
        picked = jnp.where(rank < min(N_SELECT, n_blk), 1.0, 0.0)
        sel_rows.append(jnp.sum(jnp.where(si == li, picked, 0.0), axis=0, keepdims=True))
    sel = jnp.concatenate(sel_rows, axis=0)
    h_of_g = lax.broadcasted_iota(jnp.int32, (n_heads, n_kv), 0) // group
    gsel_t = jnp.where(h_of_g == lax.broadcasted_iota(jnp.int32, (n_heads, n_kv), 1), 1.0, 0.0)
    madd_blk = (jnp.dot(gsel_t, sel, preferred_element_type=F32) - 1.0) * -NEG_INF

    def attend(kts, vts, bias, s_new, k_new, v_new):
        s = jnp.concatenate([jnp.dot(qbd, kt, preferred_element_type=F32) for kt in kts], axis=1) + bias
        s_new = s_new + jnp.sum(qbd.astype(F32) * k_new.astype(BF16).astype(F32), axis=-1, keepdims=True)
        m = jnp.maximum(jnp.maximum(jnp.max(s, axis=-1, keepdims=True), s_new), M_INIT)
        e = jnp.exp(s - m)
        e_new = jnp.exp(s_new - m)
        l = jnp.sum(e, axis=-1, keepdims=True) + e_new
        acc = e_new.astype(BF16).astype(F32) * v_new.astype(BF16).astype(F32)
        off = 0
        for vt in vts:
            n = vt.shape[1]
            acc = acc + lax.dot_general(e[:, off:off + n].astype(BF16), vt, NT_DIMS, preferred_element_type=F32)
            off += n
        return acc / jnp.where(l > 0, l, 1.0)

    n_keys = n_pages * page
    ej = lax.broadcasted_iota(jnp.int32, (LANES, n_keys), 0)
    ek = lax.broadcasted_iota(jnp.int32, (LANES, n_keys), 1) // SEL_BLOCK
    expand = jnp.where(ej == ek, 1.0, 0.0).astype(BF16)
    madd = jnp.dot((madd_blk == 0.0).astype(BF16), expand, preferred_element_type=F32)
    madd = (madd - 1.0) * -NEG_INF
    pair = 2 if n_pages % 2 == 0 else 1
    tile = lambda refs, u: jnp.concatenate([r[0, u].reshape(gd, page) for r in refs], axis=1).astype(BF16)
    groups = [page_refs[i:i + pair] for i in range(0, n_pages, pair)]
    o_s = attend([tile(rs, 0) for rs in groups], [tile(rs, 1) for rs in groups], bs_ref[...] + madd,
                 bn_ref[...] + madd_blk[:, cur:cur + 1], new[:, 2 * gd:3 * gd], new[:, 3 * gd:4 * gd])

    n_buf = win_ref.shape[4]
    o_w = attend([win_ref[0, 0].reshape(gd, n_buf).astype(BF16)], [win_ref[1, 0].reshape(gd, n_buf).astype(BF16)],
                 bw_ref[...], bn_ref[...], new[:, 4 * gd:5 * gd], new[:, 5 * gd:6 * gd])

    gt = gt_ref[0]
    o = gt[:, 0:1] * o_c + gt[:, 1:2] * o_s + gt[:, 2:3] * o_w
    hg = lax.broadcasted_iota(jnp.int32, (n_heads, 1), 0) // group
    o_ref[0] = sum(jnp.where(hg == g, o[:, g * dh:(g + 1) * dh], 0.0) for g in range(n_kv))


def _nsa_dec(page_table, qbd, gates, kct, vcf, bias_c, bias_s, bias_w, bias_n, cache_win2, kv_new, cache2,
             *, group, n_kv, dh, cmp_block, qpos):
    bsz, n_heads, gd = qbd.shape
    n_pages = page_table.shape[1]
    page = cache2.shape[4]
    n_buf = cache_win2.shape[4]
    assert page % SEL_BLOCK == 0 and n_pages * page == qpos and cache2.shape[1] == 4
    full = lambda a: pl.BlockSpec(a.shape, lambda b, pt: (0,) * a.ndim)
    per_b = lambda a: pl.BlockSpec((1,) + a.shape[1:], lambda b, pt: (b,) + (0,) * (a.ndim - 1))
    page_specs = [pl.BlockSpec((1, 2, n_kv, dh, page), functools.partial(lambda b, pt, p: (pt[b, p], 1, 0, 0, 0), p=p))
                  for p in range(n_pages)]
    win_spec = pl.BlockSpec((2, 1, n_kv, dh, n_buf), lambda b, pt: (0, b, 0, 0, 0))
    grid_spec = pltpu.PrefetchScalarGridSpec(
        num_scalar_prefetch=1,
        grid=(bsz,),
        in_specs=[per_b(qbd), per_b(gates), per_b(kct), per_b(vcf), full(bias_c), full(bias_s), full(bias_w), full(bias_n),
                  win_spec, per_b(kv_new), *page_specs],
        out_specs=pl.BlockSpec((1, n_heads, dh), lambda b, pt: (b, 0, 0)),
    )
    kern = functools.partial(_nsa_dec_kernel, n_pages=n_pages, group=group, n_kv=n_kv, dh=dh, cmp_block=cmp_block, qpos=qpos)
    return pl.pallas_call(
        kern,
        grid_spec=grid_spec,
        out_shape=jax.ShapeDtypeStruct((bsz, n_heads, dh), F32),
        compiler_params=_cp(("parallel",)),
        name="nsa_dec",
    )(page_table, qbd, gates, kct, vcf, bias_c, bias_s, bias_w, bias_n, cache_win2, kv_new, *([cache2] * n_pages))


def _static_bias(rel_table, dist, valid):
    thr = np.asarray(_rel_thresholds(rel_table.shape[0]))
    bucket = np.searchsorted(thr, np.maximum(dist, 0), side='right') - 1
    return jnp.where(jnp.asarray(valid)[None, :], rel_table.astype(F32).T[:, bucket], NEG_INF)


def _route(rt, n_experts, tm):
    n_tok = rt.shape[0]
    n_as = n_tok * TOP_K
    assert n_as % tm == 0
    n_blocks = n_as // tm
    flat_e = rt[:, TOP_K:2 * TOP_K].astype(jnp.int32).reshape(-1)
    order = jnp.argsort(flat_e)
    se = flat_e[order]
    tok_sorted = (order // TOP_K).astype(jnp.int32)
    dst_sorted = ((order % TOP_K) * n_tok + order // TOP_K).astype(jnp.int32).reshape(n_blocks, tm)
    experts = jnp.arange(n_experts)
    starts = jnp.searchsorted(se, experts, side='left').astype(jnp.int32)
    ends = jnp.searchsorted(se, experts, side='right').astype(jnp.int32)
    first_blk = starts // tm
    n_vis_e = jnp.where(ends > starts, (ends - 1) // tm - first_blk + 1, 0)
    cum = jnp.cumsum(n_vis_e)
    v = jnp.arange(n_blocks + n_experts - 1)
    ve = jnp.minimum(jnp.searchsorted(cum, v, side='right'), n_experts - 1)
    live = v < cum[-1]
    vb = first_blk[ve] + v - (cum[ve] - n_vis_e[ve])
    e_last = jnp.max(jnp.where(ends > starts, experts, 0))
    ve = jnp.where(live, ve, e_last).astype(jnp.int32)
    vb = jnp.where(live, vb, n_blocks - 1).astype(jnp.int32)
    lo = jnp.where(live, starts[ve], 0).astype(jnp.int32)
    hi = jnp.where(live, ends[ve], 0).astype(jnp.int32)
    i = jnp.arange(tm)[None, :]
    row = vb[:, None] * tm + i
    trash = n_as + (v % 2)[:, None] * tm + i
    dst = jnp.where((row >= lo[:, None]) & (row < hi[:, None]), dst_sorted[vb], trash).astype(jnp.int32)
    return tok_sorted, dst[:, None, :], (vb, ve, lo, hi)


def _split_mod(mod, bsz, per_row):
    parts = jnp.split(mod, 6, axis=-1)
    if per_row:
        return [p[None] for p in parts]
    return [p[:, None] for p in parts]


def _trunk(x, mods, kv_mod, conv0, h0, make_attend, p, *, per_row):
    depth = p['w_ada'].shape[0]
    n_a = p['w_in_a'].shape[0]
    bsz, t, d = x.shape
    n_experts = p['w_router'].shape[-1]
    alpha = (2 * depth) ** 0.25
    tt = min(256, t)
    conv_new, h_new = [], []
    attend, kv_state = None, None
    for l in range(depth):
        sh1, sc1, g1, sh2, sc2, g2 = _split_mod(mods[l], bsz, per_row)
        if l < n_a:
            win_bf = p['w_in_a'][l].astype(BF16)
            wg_bf = p['w_gate_a'][l].astype(BF16)
            args = (win_bf, p['w_conv'][l], p['b_conv'][l][None], wg_bf, p['b_gate_a'][l], p['lru_lambda'][l][None])
            if per_row:
                gated, cb, hl = _rglru_step(x[0], sc1[0], sh1[0], conv0[l].swapaxes(0, 1), h0[l], *args)
                gated, cb = gated[None], cb.swapaxes(0, 1)
            else:
                gated, cb, hl = _rglru_seq(x, sc1, sh1, conv0[l], h0[l][:, None], *args, tt=tt)
                hl = hl[:, 0]
            conv_new.append(cb)
            h_new.append(hl)
            mix, wout = gated, p['w_out_a'][l]
        else:
            if l == n_a:
                ksh, ksc = jnp.split(kv_mod, 2, axis=-1)
                ksh, ksc = (ksh[None], ksc[None]) if per_row else (ksh[:, None], ksc[:, None])
                attend, kv_state = make_attend(x, ksc, ksh)
            lb = l - n_a
            mix, wout = attend(x, sc1, sh1, p['w_in_b'][lb]), p['w_out_b'][lb]
        wr_pad = jnp.zeros((d, LANES), F32).at[:, :n_experts].set(p['w_router'][l])
        br_pad = jnp.zeros((1, LANES), F32).at[0, :n_experts].set(p['b_router'][l])
        x1, hin2, rt = _post(mix, x, g1, sc2, sh2, wout.astype(BF16), p['ln_g'][l, 0][None], p['ln_b'][l, 0][None],
                             wr_pad, br_pad, alpha=alpha, n_experts=n_experts, tt=tt)
        n_tok = bsz * t
        rt2 = rt.reshape(n_tok, LANES)
        moe_w = tuple(p[k].reshape((-1,) + p[k].shape[2:]) for k in ('w_gu', 'b_gu', 'w_down', 'b_down'))
        if per_row:
            ff = _moe_dense(hin2.reshape(n_tok, d), rt2, *moe_w, n_e=n_experts, e_off=l * n_experts).reshape(bsz, t, d)
            x = _final(x1, ff, rt, g2, p['ln_g'][l, 1][None], p['ln_b'][l, 1][None], alpha=alpha, combine=False, tt=tt)
        else:
            tok_sorted, dst, visits = _route(rt2, n_experts, MOE_ROWS)
            xs = hin2.reshape(n_tok, d)[tok_sorted]
            yg = _moe_grouped(visits, dst, xs, *moe_w, tm=MOE_ROWS, trash0=TOP_K * n_tok, e_off=l * n_experts)
            x = _final(x1, yg, rt, g2, p['ln_g'][l, 1][None], p['ln_b'][l, 1][None], alpha=alpha, combine=True, tt=tt)
    return x, jnp.stack(conv_new), jnp.stack(h_new), kv_state


def _qg_weight_t(w_in, n_kv, group, dh):
    d = w_in.shape[0]
    hd = n_kv * group * dh
    wg = w_in[:, hd:].reshape(d, n_kv, 3 * group)
    wg = jnp.pad(wg, ((0, 0), (0, 0), (0, 16 - 3 * group))).reshape(d, n_kv * 16)
    return jnp.concatenate([w_in[:, :hd], wg], axis=1).T.astype(BF16)


def _toeplitz_bias(rel_table, tq):
    n_buckets = rel_table.shape[0]
    thr = np.asarray(_rel_thresholds(n_buckets))
    kk = np.arange(tq)[:, None]
    ii = np.arange(tq)[None, :]
    dist = np.maximum(np.stack([ii - kk, tq + ii - kk]), 0)
    bucket = np.searchsorted(thr, dist, side='right') - 1
    return rel_table.astype(F32).T[:, bucket]


def kernel(x_prompt, x_sample, c_prompt, c_sample, state_conv, state_h, cache_kv, cache_win, page_table, w_ada, b_ada, ln_g, ln_b, w_in_a, w_conv, b_conv, w_gate_a, b_gate_a, lru_lambda, w_out_a, w_ada_kv, b_ada_kv, w_kv, cmp_w1, cmp_b1, cmp_w2, cmp_b2, cmp_pe, w_in_b, w_out_b, rel_table, w_router, b_router, w_gu, b_gu, w_down, b_down):
    p = dict(w_ada=w_ada, b_ada=b_ada, ln_g=ln_g, ln_b=ln_b, w_in_a=w_in_a, w_conv=w_conv, b_conv=b_conv,
             w_gate_a=w_gate_a, b_gate_a=b_gate_a, lru_lambda=lru_lambda, w_out_a=w_out_a, w_kv=w_kv,
             w_in_b=w_in_b, w_out_b=w_out_b, w_router=w_router, b_router=b_router, w_gu=w_gu, b_gu=b_gu,
             w_down=w_down, b_down=b_down)
    bsz, seq, d = x_prompt.shape
    dec_b = x_sample.shape[0]
    n_a = w_in_a.shape[0]
    cw = w_conv.shape[1]
    lru_w = w_conv.shape[2]
    n_kv, dh = cache_kv.shape[3], cache_kv.shape[4]
    n_heads = rel_table.shape[1]
    group = n_heads // n_kv
    hd = n_heads * dh
    gd = n_kv * dh
    n_parts = w_kv.shape[1] // gd
    assert n_parts == 6 and 3 * group <= 16
    cmp_block = cmp_pe.shape[1]
    scale = dh ** -0.5
    assert math.log2(scale) == round(math.log2(scale))

    c_all = jnp.concatenate([c_prompt, c_sample], axis=0)
    mods = _linear(c_all, w_ada, b_ada[:, None, :], act_in='silu')
    kv_mod = _linear(c_all, w_ada_kv[None], b_ada_kv[None, None, :], act_in='silu')[0]
    tab = rel_table.astype(F32).reshape(-1)

    def cmp_kv(k_head, v_head, length):
        kc = _compress(k_head, length, cmp_w1[0], cmp_b1[0], cmp_w2[0], cmp_b2[0], cmp_pe[0])
        vc = _compress(v_head, length, cmp_w1[1], cmp_b1[1], cmp_w2[1], cmp_b2[1], cmp_pe[1])
        return kc, vc

    def make_prompt_attend(x, ksc, ksh):
        tq = min(NSA_TILE, seq)
        wvt = jnp.concatenate([w_kv[:, 3 * gd:4 * gd], w_kv[:, 5 * gd:6 * gd]], axis=1).T.astype(BF16)
        kv, kk, vt = _kvproj(x, ksc, ksh, w_kv.astype(BF16), wvt, n_kv=n_kv, dh=dh, k_parts=(2, 4), tt=tq)
        kv6 = kv.reshape(bsz, seq, n_parts, n_kv, dh)
        kc, vc = cmp_kv(kv6[:, :cmp_block, 0], kv6[:, :cmp_block, 1], seq)
        n_cmp = kc.shape[1]
        ncp = -(-n_cmp // SUBLANES_BF16) * SUBLANES_BF16
        padc = lambda a: jnp.pad(a, ((0, 0), (0, ncp - n_cmp), (0, 0), (0, 0))).astype(BF16)
        kc_g = padc(kc).transpose(0, 2, 1, 3)
        vc_t = padc(vc).transpose(0, 2, 3, 1)
        toe = _toeplitz_bias(rel_table, tq)

        def attend(x, sc1, sh1, w_in):
            qt, gt = _qproj(x, sc1, sh1, _qg_weight_t(w_in, n_kv, group, dh), hd=hd, scale=scale, tt=tq)
            return _nsa_seq(tab, qt, gt, kc_g, vc_t, kk, vt, toe, group=group, dh=dh, cmp_block=cmp_block,
                            n_cmp_real=n_cmp, tq=tq)

        n_win = min(WINDOW, seq)
        return attend, (kv6[:, :, :4], kv6[:, seq - n_win:, 4:])

    def make_sample_attend(x, ksc, ksh):
        kv = _modlinear(x, ksc, ksh, w_kv.astype(BF16), tt=dec_b)
        kv6 = kv.reshape(dec_b, 1, n_parts, n_kv, dh)
        page = cache_kv.shape[1]
        past_len = page_table.shape[1] * page
        n_buf = cache_win.shape[1]
        assert page >= cmp_block and past_len >= cmp_block
        head = cache_kv[page_table[:, 0], :cmp_block]
        kc, vc = cmp_kv(head[:, :, 0], head[:, :, 1], past_len + 1)
        n_cmp = kc.shape[1]
        ncp = -(-n_cmp // SUBLANES_BF16) * SUBLANES_BF16
        padc = lambda a: jnp.pad(a, ((0, 0), (0, ncp - n_cmp), (0, 0), (0, 0))).astype(BF16).reshape(dec_b, ncp, gd)
        kct, vcf = padc(kc).transpose(0, 2, 1), padc(vc)
        c = np.arange(ncp)
        dist_c = past_len - (c * CMP_STRIDE + cmp_block - 1)
        bias_c = _static_bias(rel_table, dist_c, (dist_c >= 0) & (c < n_cmp))
        bias_s = _static_bias(rel_table, past_len - np.arange(past_len), np.ones(past_len, bool))
        dist_w = n_buf - np.arange(n_buf)
        bias_w = _static_bias(rel_table, dist_w, (dist_w <= WINDOW) & (past_len - dist_w >= 0))
        bias_n = rel_table.astype(F32)[0][:, None]
        cache2 = cache_kv.transpose(0, 2, 3, 4, 1)
        cache_win2 = cache_win.transpose(2, 0, 3, 4, 1)
        win = jnp.concatenate([cache_win, kv6[:, :, 4:]], axis=1)
        own_group = (jnp.arange(n_heads)[:, None] // group == jnp.arange(n_kv)[None, :]).astype(BF16)

        def attend(x, sc1, sh1, w_in):
            qt, gt = _qproj(x, sc1, sh1, _qg_weight_t(w_in, n_kv, group, dh), hd=hd, scale=scale, tt=dec_b)
            q = qt[0].T.reshape(dec_b, n_heads, dh)
            qbd = (q[:, :, None, :] * own_group[None, :, :, None]).reshape(dec_b, n_heads, gd)
            gates = gt[0].reshape(n_kv, 16, dec_b)[:, :3 * group].transpose(2, 0, 1).reshape(dec_b, n_heads, 3)
            gates = jnp.pad(gates, ((0, 0), (0, 0), (0, LANES - 3)))
            o = _nsa_dec(page_table, qbd, gates, kct, vcf, bias_c, bias_s, bias_w, bias_n, cache_win2,
                         kv.reshape(dec_b, 1, n_parts * gd), cache2, group=group, n_kv=n_kv, dh=dh, cmp_block=cmp_block, qpos=past_len)
            return o.reshape(1, dec_b, hd)

        return attend, (kv6[:, :, :4], win[:, -n_buf:])

    conv0 = jnp.zeros((n_a, bsz, cw - 1, lru_w), F32)
    h0 = jnp.zeros((n_a, bsz, lru_w), F32)
    y_prompt, prompt_conv, prompt_h, (prompt_kv, prompt_win) = _trunk(
        x_prompt, mods[:, :bsz], kv_mod[:bsz], conv0, h0, make_prompt_attend, p, per_row=False)
    y_sample, sample_conv, sample_h, (sample_kv, sample_win) = _trunk(
        x_sample.reshape(1, dec_b, d), mods[:, bsz:], kv_mod[bsz:], state_conv, state_h, make_sample_attend, p, per_row=True)
    return (y_prompt, y_sample.reshape(dec_b, 1, d), prompt_conv, prompt_h, prompt_kv, prompt_win,
            sample_conv, sample_h, sample_kv, sample_win)
```

```python
import functools
import math

import numpy as np
import jax
import jax.numpy as jnp
from jax import lax
from jax.experimental import pallas as pl
from jax.experimental.pallas import tpu as pltpu

F32 = jnp.float32
BF16 = jnp.bfloat16
HIGHEST = lax.Precision.HIGHEST
NT_DIMS = (((1,), (1,)), ((), ()))

CMP_STRIDE = 16
SEL_BLOCK = 64
N_SELECT = 16
WINDOW = 512
REL_MAX_DIST = 128
TOP_K = 4
RGLRU_C = 8.0
SWIGLU_LIMIT = 7.0
SWIGLU_ALPHA = 1.702
MOE_TILE = 256
LN_EPS = 1e-5
SEL_BIG = 1e9
NEG_INF = -1e30
M_INIT = -1e29

LANES = 128
SUBLANES_BF16 = 16
VMEM_LIMIT = 48 * 1024 * 1024
NSA_TILE = 256


def _cp(sem, vmem=VMEM_LIMIT):
    return pltpu.CompilerParams(dimension_semantics=sem, vmem_limit_bytes=vmem)


def _rel_thresholds(n_buckets):
    exact = n_buckets // 2
    d_max = REL_MAX_DIST + 1
    buckets = []
    for d in range(d_max + 1):
        if d < exact:
            buckets.append(d)
        else:
            large = exact + int(math.log(max(d, 1) / exact) / math.log(REL_MAX_DIST / exact) * (n_buckets - exact))
            buckets.append(min(large, n_buckets - 1))
    return [next(d for d in range(d_max + 1) if buckets[d] >= b) for b in range(n_buckets)]


def _linear_kernel(x_ref, w_ref, b_ref, o_ref, *, act_in):
    x = x_ref[...]
    if act_in == 'silu':
        x = x * jax.nn.sigmoid(x)
    elif act_in == 'gelu':
        x = jax.nn.gelu(x)
    y = jnp.dot(x.astype(BF16), w_ref[0].astype(BF16), preferred_element_type=F32)
    o_ref[0] = y + b_ref[0]


def _linear(x, w, b, *, act_in=None, tn=512):
    m, k = x.shape
    nl, _, n = w.shape
    tn = min(tn, n)
    return pl.pallas_call(
        functools.partial(_linear_kernel, act_in=act_in),
        grid=(nl, n // tn),
        in_specs=[
            pl.BlockSpec((m, k), lambda l, j: (0, 0)),
            pl.BlockSpec((1, k, tn), lambda l, j: (l, 0, j)),
            pl.BlockSpec((1, 1, tn), lambda l, j: (l, 0, j)),
        ],
        out_specs=pl.BlockSpec((1, m, tn), lambda l, j: (l, 0, j)),
        out_shape=jax.ShapeDtypeStruct((nl, m, n), F32),
        compiler_params=_cp(("parallel", "parallel")),
        name="linear",
    )(x, w, b)


def _mod_spec(mod, tt):
    d = mod.shape[-1]
    if mod.shape[1] == 1:
        return pl.BlockSpec((1, 1, d), lambda b, i: (b, 0, 0))
    return pl.BlockSpec((1, tt, d), lambda b, i: (b, i, 0))


def _modlinear_kernel(x_ref, sc_ref, sh_ref, w_ref, o_ref):
    hin = x_ref[0] * (1.0 + sc_ref[0]) + sh_ref[0]
    o_ref[0] = jnp.dot(hin.astype(BF16), w_ref[...], preferred_element_type=F32)


def _modlinear(x, sc, sh, w_bf, *, tt):
    bsz, t, d = x.shape
    n = w_bf.shape[1]
    return pl.pallas_call(
        _modlinear_kernel,
        grid=(bsz, t // tt),
        in_specs=[
            pl.BlockSpec((1, tt, d), lambda b, i: (b, i, 0)),
            _mod_spec(sc, tt),
            _mod_spec(sh, tt),
            pl.BlockSpec((d, n), lambda b, i: (0, 0)),
        ],
        out_specs=pl.BlockSpec((1, tt, n), lambda b, i: (b, i, 0)),
        out_shape=jax.ShapeDtypeStruct((bsz, t, n), F32),
        compiler_params=_cp(("parallel", "parallel")),
        name="modlinear",
    )(x, sc, sh, w_bf)


def _kvproj_kernel(x_ref, sc_ref, sh_ref, w_ref, wvt_ref, kv_ref, kk_ref, vt_ref, *, n_kv, dh, k_parts):
    hin = (x_ref[0] * (1.0 + sc_ref[0]) + sh_ref[0]).astype(BF16)
    y = jnp.dot(hin, w_ref[...], preferred_element_type=F32)
    kv_ref[0] = y
    gd = n_kv * dh
    for j, part in enumerate(k_parts):
        for g in range(n_kv):
            kk_ref[j, 0, g] = y[:, part * gd + g * dh:part * gd + (g + 1) * dh].astype(BF16)
    yt = lax.dot_general(wvt_ref[...], hin, NT_DIMS, preferred_element_type=F32)
    for j in range(vt_ref.shape[1]):
        vt_ref[0, j, 0] = yt[j * dh:(j + 1) * dh].astype(BF16)


def _kvproj(x, sc, sh, w_bf, wvt_bf, *, n_kv, dh, k_parts, tt):
    bsz, t, d = x.shape
    n = w_bf.shape[1]
    nv = wvt_bf.shape[0] // dh
    return pl.pallas_call(
        functools.partial(_kvproj_kernel, n_kv=n_kv, dh=dh, k_parts=k_parts),
        grid=(bsz, t // tt),
        in_specs=[
            pl.BlockSpec((1, tt, d), lambda b, i: (b, i, 0)),
            _mod_spec(sc, tt),
            _mod_spec(sh, tt),
            pl.BlockSpec((d, n), lambda b, i: (0, 0)),
            pl.BlockSpec((nv * dh, d), lambda b, i: (0, 0)),
        ],
        out_specs=[
            pl.BlockSpec((1, tt, n), lambda b, i: (b, i, 0)),
            pl.BlockSpec((len(k_parts), 1, n_kv, tt, dh), lambda b, i: (0, b, 0, i, 0)),
            pl.BlockSpec((1, nv, 1, dh, tt), lambda b, i: (b, 0, i, 0, 0)),
        ],
        out_shape=[
            jax.ShapeDtypeStruct((bsz, t, n), F32),
            jax.ShapeDtypeStruct((len(k_parts), bsz, n_kv, t, dh), BF16),
            jax.ShapeDtypeStruct((bsz, nv, t // tt, dh, tt), BF16),
        ],
        compiler_params=_cp(("parallel", "parallel")),
        name="kvproj",
    )(x, sc, sh, w_bf, wvt_bf)


def _qproj_kernel(x_ref, sc_ref, sh_ref, wt_ref, qt_ref, gt_ref, *, hd, scale):
    hin = (x_ref[0] * (1.0 + sc_ref[0]) + sh_ref[0]).astype(BF16)
    yt = lax.dot_general(wt_ref[...], hin, NT_DIMS, preferred_element_type=F32)
    qt_ref[0] = (yt[:hd] * scale).astype(BF16)
    gt_ref[0] = jax.nn.sigmoid(yt[hd:])


def _qproj(x, sc, sh, wt_bf, *, hd, scale, tt):
    bsz, t, d = x.shape
    n = wt_bf.shape[0]
    return pl.pallas_call(
        functools.partial(_qproj_kernel, hd=hd, scale=scale),
        grid=(bsz, t // tt),
        in_specs=[
            pl.BlockSpec((1, tt, d), lambda b, i: (b, i, 0)),
            _mod_spec(sc, tt),
            _mod_spec(sh, tt),
            pl.BlockSpec((n, d), lambda b, i: (0, 0)),
        ],
        out_specs=[
            pl.BlockSpec((1, hd, tt), lambda b, i: (b, 0, i)),
            pl.BlockSpec((1, n - hd, tt), lambda b, i: (b, 0, i)),
        ],
        out_shape=[
            jax.ShapeDtypeStruct((bsz, hd, t), BF16),
            jax.ShapeDtypeStruct((bsz, n - hd, t), F32),
        ],
        compiler_params=_cp(("parallel", "parallel")),
        name="qproj",
    )(x, sc, sh, wt_bf)


def _log1p(x):
    u = 1.0 + x
    return jnp.where(u == 1.0, x, jnp.log(u) * x / jnp.where(u == 1.0, 1.0, u - 1.0))


def _expm1(x):
    u = jnp.exp(x)
    safe = (u != 1.0) & (u > 0.0)
    return jnp.where(u == 1.0, x, jnp.where(u > 0.0, (u - 1.0) * x / jnp.where(safe, jnp.log(u), 1.0), -1.0))


def _softplus(z):
    return jnp.maximum(z, 0.0) + _log1p(jnp.exp(-jnp.abs(z)))


def _rglru_gates(xc, wg_ref, bg_ref, lam_ref, n_heads):
    w = xc.shape[1]
    bw = w // n_heads
    xcb = xc.astype(BF16)
    gates = []
    for g in range(2):
        cols = [jnp.dot(xcb[:, i * bw:(i + 1) * bw], wg_ref[g, i], preferred_element_type=F32) for i in range(n_heads)]
        gates.append(jnp.concatenate(cols, axis=1) + bg_ref[g:g + 1, :])
    r = jax.nn.sigmoid(gates[0])
    i_g = jax.nn.sigmoid(gates[1])
    log_a = (-RGLRU_C * _softplus(-lam_ref[...])) * r
    a = jnp.exp(log_a)
    b_in = jnp.sqrt(-_expm1(2.0 * log_a)) * (i_g * xc)
    return a, b_in


def _rglru_seq_kernel(x_ref, sc_ref, sh_ref, conv0_ref, h0_ref, win_ref, wconv_ref, bconv_ref, wg_ref, bg_ref, lam_ref,
                      gated_ref, convout_ref, hout_ref, ext_ref, a_ref, b_ref, yb_ref, h_ref, *, tt, n_heads, cw):
    t = pl.program_id(1)
    w = a_ref.shape[1]
    pad = 8

    @pl.when(t == 0)
    def _():
        ext_ref[0:pad - (cw - 1), :] = jnp.zeros((pad - (cw - 1), w), F32)
        ext_ref[pad - (cw - 1):pad, :] = conv0_ref[0]
        h_ref[...] = h0_ref[0]

    hin = x_ref[0] * (1.0 + sc_ref[0]) + sh_ref[0]
    u = jnp.dot(hin.astype(BF16), win_ref[...], preferred_element_type=F32)
    yb_ref[...] = jax.nn.gelu(u[:, :w])
    ext_ref[pad:pad + tt, :] = u[:, w:]
    xc = bconv_ref[...] + sum(ext_ref[pad - (cw - 1) + k:pad - (cw - 1) + k + tt, :] * wconv_ref[k:k + 1, :] for k in range(cw))
    tail = ext_ref[tt:tt + pad, :]
    ext_ref[0:pad, :] = tail

    a, b_in = _rglru_gates(xc, wg_ref, bg_ref, lam_ref, n_heads)
    a_ref[...] = a
    b_ref[...] = b_in

    row = lax.broadcasted_iota(jnp.int32, (tt, LANES), 0)
    for c in range(w // LANES):
        cs = slice(c * LANES, (c + 1) * LANES)
        av = a_ref[:, cs]
        bv = b_ref[:, cs]
        s = 1
        while s < tt:
            keep = row >= s
            b_sh = jnp.where(keep, pltpu.roll(bv, s, 0), 0.0)
            a_sh = jnp.where(keep, pltpu.roll(av, s, 0), 1.0)
            bv = av * b_sh + bv
            av = av * a_sh
            s *= 2
        hs = av * h_ref[:, cs] + bv
        h_ref[:, cs] = hs[tt - 1:tt, :]
        gated_ref[0, :, cs] = (hs * yb_ref[:, cs]).astype(gated_ref.dtype)

    @pl.when(t == pl.num_programs(1) - 1)
    def _():
        convout_ref[0] = tail[pad - (cw - 1):pad, :]
        hout_ref[0] = h_ref[...]


def _rglru_seq(x, sc, sh, conv0, h0, win_bf, wconv, bconv, wg_bf, bg, lam, *, tt):
    bsz, t, d = x.shape
    cw, w = wconv.shape
    n_heads = wg_bf.shape[1]
    full = lambda shape: pl.BlockSpec(shape, lambda b, i: (0,) * len(shape))
    return pl.pallas_call(
        functools.partial(_rglru_seq_kernel, tt=tt, n_heads=n_heads, cw=cw),
        grid=(bsz, t // tt),
        in_specs=[
            pl.BlockSpec((1, tt, d), lambda b, i: (b, i, 0)),
            _mod_spec(sc, tt),
            _mod_spec(sh, tt),
            pl.BlockSpec((1, cw - 1, w), lambda b, i: (b, 0, 0)),
            pl.BlockSpec((1, 1, w), lambda b, i: (b, 0, 0)),
            full((d, 2 * w)),
            full((cw, w)),
            full((1, w)),
            full(wg_bf.shape),
            full((2, w)),
            full((1, w)),
        ],
        out_specs=[
            pl.BlockSpec((1, tt, w), lambda b, i: (b, i, 0)),
            pl.BlockSpec((1, cw - 1, w), lambda b, i: (b, 0, 0)),
            pl.BlockSpec((1, 1, w), lambda b, i: (b, 0, 0)),
        ],
        out_shape=[
            jax.ShapeDtypeStruct((bsz, t, w), BF16),
            jax.ShapeDtypeStruct((bsz, cw - 1, w), F32),
            jax.ShapeDtypeStruct((bsz, 1, w), F32),
        ],
        scratch_shapes=[
            pltpu.VMEM((tt + 8, w), F32),
            pltpu.VMEM((tt, w), F32),
            pltpu.VMEM((tt, w), F32),
            pltpu.VMEM((tt, w), F32),
            pltpu.VMEM((1, w), F32),
        ],
        compiler_params=_cp(("parallel", "arbitrary")),
        name="rglru_seq",
    )(x, sc, sh, conv0, h0, win_bf, wconv, bconv, wg_bf, bg, lam)


def _rglru_step_kernel(x_ref, sc_ref, sh_ref, conv_ref, h0_ref, win_ref, wconv_ref, bconv_ref, wg_ref, bg_ref, lam_ref,
                       gated_ref, convout_ref, hout_ref, *, n_heads, cw):
    w = h0_ref.shape[1]
    hin = x_ref[...] * (1.0 + sc_ref[...]) + sh_ref[...]
    u = jnp.dot(hin.astype(BF16), win_ref[...], preferred_element_type=F32)
    yb = jax.nn.gelu(u[:, :w])
    xb = u[:, w:]
    taps = [conv_ref[k] for k in range(cw - 1)] + [xb]
    xc = bconv_ref[...] + sum(taps[k] * wconv_ref[k:k + 1, :] for k in range(cw))
    a, b_in = _rglru_gates(xc, wg_ref, bg_ref, lam_ref, n_heads)
    h = a * h0_ref[...] + b_in
    gated_ref[...] = (h * yb).astype(gated_ref.dtype)
    for k in range(cw - 1):
        convout_ref[k] = taps[k + 1]
    hout_ref[...] = h


def _rglru_step(x, sc, sh, conv, h0, win_bf, wconv, bconv, wg_bf, bg, lam):
    r, _ = x.shape
    cw, w = wconv.shape
    n_heads = wg_bf.shape[1]
    return pl.pallas_call(
        functools.partial(_rglru_step_kernel, n_heads=n_heads, cw=cw),
        out_shape=[
            jax.ShapeDtypeStruct((r, w), BF16),
            jax.ShapeDtypeStruct((cw - 1, r, w), F32),
            jax.ShapeDtypeStruct((r, w), F32),
        ],
        compiler_params=pltpu.CompilerParams(vmem_limit_bytes=VMEM_LIMIT),
        name="rglru_step",
    )(x, sc, sh, conv, h0, win_bf, wconv, bconv, wg_bf, bg, lam)


def _layer_norm(v, g, b):
    mu = jnp.mean(v, axis=-1, keepdims=True)
    dv = v - mu
    var = jnp.mean(dv * dv, axis=-1, keepdims=True)
    return dv * lax.rsqrt(var + LN_EPS) * g + b


def _post_kernel(a_ref, x_ref, g1_ref, sc2_ref, sh2_ref, wout_ref, lng_ref, lnb_ref, wr_ref, br_ref,
                 x1_ref, hin2_ref, rt_ref, *, alpha, n_experts):
    y = jnp.dot(a_ref[0].astype(BF16), wout_ref[...], preferred_element_type=F32)
    x1 = _layer_norm(alpha * x_ref[0] + (1.0 + g1_ref[0]) * y, lng_ref[...], lnb_ref[...])
    x1_ref[0] = x1
    hin2 = x1 * (1.0 + sc2_ref[0]) + sh2_ref[0]
    hin2_ref[0] = hin2.astype(hin2_ref.dtype)
    logits = jnp.dot(hin2.astype(BF16), wr_ref[...].astype(BF16), preferred_element_type=F32) + br_ref[...]
    lane = lax.broadcasted_iota(jnp.int32, logits.shape, 1)
    work = jnp.where(lane < n_experts, logits, -jnp.inf)
    vals, idxs = [], []
    for _ in range(TOP_K):
        m = jnp.max(work, axis=-1, keepdims=True)
        idx = jnp.min(jnp.where(work == m, lane, LANES), axis=-1, keepdims=True)
        vals.append(m)
        idxs.append(idx)
        work = jnp.where(lane == idx, -jnp.inf, work)
    es = [jnp.exp(v - vals[0]) for v in vals]
    z = sum(es)
    rt = jnp.zeros(logits.shape, F32)
    for k in range(TOP_K):
        rt = jnp.where(lane == k, es[k] / z, rt)
        rt = jnp.where(lane == TOP_K + k, idxs[k].astype(F32), rt)
    rt_ref[0] = rt


def _post(a, x, g1, sc2, sh2, wout_bf, lng, lnb, wr_pad, br_pad, *, alpha, n_experts, tt):
    bsz, t, d = x.shape
    dk = a.shape[-1]
    full = lambda shape: pl.BlockSpec(shape, lambda b, i: (0,) * len(shape))
    row = pl.BlockSpec((1, tt, d), lambda b, i: (b, i, 0))
    return pl.pallas_call(
        functools.partial(_post_kernel, alpha=alpha, n_experts=n_experts),
        grid=(bsz, t // tt),
        in_specs=[
            pl.BlockSpec((1, tt, dk), lambda b, i: (b, i, 0)),
            row,
            _mod_spec(g1, tt),
            _mod_spec(sc2, tt),
            _mod_spec(sh2, tt),
            full((dk, d)),
            full((1, d)),
            full((1, d)),
            full((d, LANES)),
            full((1, LANES)),
        ],
        out_specs=[row, row, pl.BlockSpec((1, tt, LANES), lambda b, i: (b, i, 0))],
        out_shape=[
            jax.ShapeDtypeStruct((bsz, t, d), F32),
            jax.ShapeDtypeStruct((bsz, t, d), F32),
            jax.ShapeDtypeStruct((bsz, t, LANES), F32),
        ],
        compiler_params=_cp(("parallel", "parallel")),
        name="post",
    )(a, x, g1, sc2, sh2, wout_bf, lng, lnb, wr_pad, br_pad)


def _swiglu(g, u):
    g = jnp.minimum(g, SWIGLU_LIMIT)
    u = jnp.clip(u, -SWIGLU_LIMIT, SWIGLU_LIMIT)
    return g * jax.nn.sigmoid(SWIGLU_ALPHA * g) * (u + 1.0)


def _moe_grouped_kernel(vb_ref, ve_ref, lo_ref, hi_ref, dst_ref, xs_ref, wgu_ref, bgu_ref, wdn_ref, bdn_ref, yg_ref,
                        wgu_bf, wdn_bf, stage_ref, sem, *, chunk, tm, trash0):
    v = pl.program_id(0)
    last = pl.num_programs(0) - 1
    d, de2 = wgu_bf.shape
    de = de2 // 2
    e = ve_ref[v]
    pv = jnp.maximum(v - 1, 0)
    slot = v % 2

    def live(u):
        return hi_ref[u] > lo_ref[u]

    def wait_rows(s):
        pltpu.make_async_copy(stage_ref.at[s], yg_ref.at[pl.ds(0, tm)], sem.at[s]).wait()

    @pl.when((v == 0) | (e != ve_ref[pv]))
    def _():
        for c in range(de2 // chunk):
            wgu_bf[:, c * chunk:(c + 1) * chunk] = wgu_ref[0, :, c * chunk:(c + 1) * chunk].astype(BF16)
        for c in range(de // chunk):
            wdn_bf[c * chunk:(c + 1) * chunk, :] = wdn_ref[0, c * chunk:(c + 1) * chunk, :].astype(BF16)

    @pl.when(v == 0)
    def _():
        stage_ref[...] = jnp.zeros(stage_ref.shape, F32)
        for s in range(2):
            fill = pltpu.make_async_copy(stage_ref.at[s], yg_ref.at[pl.ds(trash0 + s * tm, tm)], sem.at[s])
            fill.start()
            fill.wait()

    @pl.when((v >= 2) & live(jnp.maximum(v - 2, 0)))
    def _():
        wait_rows(slot)

    lo = lo_ref[v]
    hi = hi_ref[v]

    @pl.when(hi > lo)
    def _():
        x = xs_ref[...].astype(BF16)
        acc = jnp.zeros((tm, d), F32) + bdn_ref[0]
        for c in range(de // chunk):
            g = jnp.dot(x, wgu_bf[:, c * chunk:(c + 1) * chunk], preferred_element_type=F32) + bgu_ref[0, :, c * chunk:(c + 1) * chunk]
            u = jnp.dot(x, wgu_bf[:, de + c * chunk:de + (c + 1) * chunk], preferred_element_type=F32) + bgu_ref[0, :, de + c * chunk:de + (c + 1) * chunk]
            h = _swiglu(g, u).astype(BF16)
            acc = acc + jnp.dot(h, wdn_bf[c * chunk:(c + 1) * chunk, :], preferred_element_type=F32)
        for s in range(2):
            @pl.when(slot == s)
            def _(s=s):
                stage_ref[s] = acc

                for i in range(tm):
                    pltpu.make_async_copy(stage_ref.at[s, pl.ds(i, 1)], yg_ref.at[pl.ds(dst_ref[0, 0, i], 1)],
                                          sem.at[s]).start()

    @pl.when(v == last)
    def _():
        @pl.when(hi > lo)
        def _():
            wait_rows(slot)

        @pl.when((v >= 1) & live(pv))
        def _():
            wait_rows(1 - slot)


def _moe_grouped(visits, dst, xs, w_gu, b_gu, w_dn, b_dn, *, tm, trash0, e_off):
    n_rows, d = xs.shape
    n_e, _, de2 = w_gu.shape
    de = de2 // 2
    n_vis = visits[0].shape[0]
    grid_spec = pltpu.PrefetchScalarGridSpec(
        num_scalar_prefetch=4,
        grid=(n_vis,),
        in_specs=[
            pl.BlockSpec((1, 1, tm), lambda v, vb, ve, lo, hi: (v, 0, 0), memory_space=pltpu.SMEM),
            pl.BlockSpec((tm, d), lambda v, vb, ve, lo, hi: (vb[v], 0)),
            pl.BlockSpec((1, d, de2), lambda v, vb, ve, lo, hi: (ve[v] + e_off, 0, 0)),
            pl.BlockSpec((1, 1, de2), lambda v, vb, ve, lo, hi: (ve[v] + e_off, 0, 0)),
            pl.BlockSpec((1, de, d), lambda v, vb, ve, lo, hi: (ve[v] + e_off, 0, 0)),
            pl.BlockSpec((1, 1, d), lambda v, vb, ve, lo, hi: (ve[v] + e_off, 0, 0)),
        ],
        out_specs=pl.BlockSpec(memory_space=pl.ANY),
        scratch_shapes=[
            pltpu.VMEM((d, de2), BF16),
            pltpu.VMEM((de, d), BF16),
            pltpu.VMEM((2, tm, d), F32),
            pltpu.SemaphoreType.DMA((2,)),
        ],
    )
    return pl.pallas_call(
        functools.partial(_moe_grouped_kernel, chunk=min(512, de), tm=tm, trash0=trash0),
        grid_spec=grid_spec,
        out_shape=jax.ShapeDtypeStruct((trash0 + 2 * tm, d), F32),
        compiler_params=_cp(("arbitrary",), 56 * 1024 * 1024),
        name="moe_grouped",
    )(*visits, dst, xs, w_gu, b_gu.reshape(n_e, 1, de2), w_dn, b_dn.reshape(n_e, 1, d))


def _moe_dense_kernel(x_ref, rt_ref, wg_ref, wu_ref, bg_ref, bu_ref, wdn_ref, bdn_ref, o_ref):
    e = pl.program_id(0)
    c = pl.program_id(1)

    @pl.when((e == 0) & (c == 0))
    def _():
        o_ref[...] = jnp.zeros(o_ref.shape, F32)

    rt = rt_ref[...]
    ef = e.astype(F32)
    gate = sum(jnp.where(rt[:, TOP_K + k:TOP_K + k + 1] == ef, rt[:, k:k + 1], 0.0) for k in range(TOP_K))
    x = x_ref[...].astype(BF16)
    g = jnp.dot(x, wg_ref[0].astype(BF16), preferred_element_type=F32) + bg_ref[0]
    u = jnp.dot(x, wu_ref[0].astype(BF16), preferred_element_type=F32) + bu_ref[0]
    h = _swiglu(g, u).astype(BF16)
    y = jnp.dot(h, wdn_ref[0].astype(BF16), preferred_element_type=F32)
    y = y + jnp.where(c == 0, 1.0, 0.0) * bdn_ref[0]
    o_ref[...] += gate * y


def _moe_dense(x_bf, rt, w_gu, b_gu, w_dn, b_dn, *, n_e, e_off, chunk=256):
    r, d = x_bf.shape
    n_all, _, de2 = w_gu.shape
    de = de2 // 2
    nc = de // chunk
    b_gu3 = b_gu.reshape(n_all, 1, de2)
    return pl.pallas_call(
        _moe_dense_kernel,
        grid=(n_e, nc),
        in_specs=[
            pl.BlockSpec((r, d), lambda e, c: (0, 0)),
            pl.BlockSpec((r, LANES), lambda e, c: (0, 0)),
            pl.BlockSpec((1, d, chunk), lambda e, c: (e + e_off, 0, c)),
            pl.BlockSpec((1, d, chunk), lambda e, c: (e + e_off, 0, nc + c)),
            pl.BlockSpec((1, 1, chunk), lambda e, c: (e + e_off, 0, c)),
            pl.BlockSpec((1, 1, chunk), lambda e, c: (e + e_off, 0, nc + c)),
            pl.BlockSpec((1, chunk, d), lambda e, c: (e + e_off, c, 0)),
            pl.BlockSpec((1, 1, d), lambda e, c: (e + e_off, 0, 0)),
        ],
        out_specs=pl.BlockSpec((r, d), lambda e, c: (0, 0)),
        out_shape=jax.ShapeDtypeStruct((r, d), F32),
        compiler_params=_cp(("arbitrary", "arbitrary")),
        name="moe_dense",
    )(x_bf, rt, w_gu, w_gu, b_gu3, b_gu3, w_dn, b_dn.reshape(n_all, 1, d))


def _final_kernel(x_ref, *refs, alpha, combine):
    if combine:
        ff_refs, (rt_ref, g2_ref, lng_ref, lnb_ref, o_ref) = refs[:TOP_K], refs[TOP_K:]
        rt = rt_ref[0]
        ff = sum(rt[:, k:k + 1] * ff_refs[k][...] for k in range(TOP_K))
    else:
        ff_ref, rt_ref, g2_ref, lng_ref, lnb_ref, o_ref = refs
        ff = ff_ref[0]
    o_ref[0] = _layer_norm(alpha * x_ref[0] + (1.0 + g2_ref[0]) * ff, lng_ref[...], lnb_ref[...])


def _final(x1, ff, rt, g2, lng, lnb, *, alpha, combine, tt):
    bsz, t, d = x1.shape
    row = pl.BlockSpec((1, tt, d), lambda b, i: (b, i, 0))
    nblk = bsz * t // tt
    if combine:
        ff_specs = [pl.BlockSpec((tt, d), functools.partial(lambda b, i, k: (k * nblk + b * (t // tt) + i, 0), k=k))
                    for k in range(TOP_K)]
        ffs = [ff] * TOP_K
    else:
        ff_specs, ffs = [row], [ff]
    full = lambda shape: pl.BlockSpec(shape, lambda b, i: (0,) * len(shape))
    return pl.pallas_call(
        functools.partial(_final_kernel, alpha=alpha, combine=combine),
        grid=(bsz, t // tt),
        in_specs=[row, *ff_specs, pl.BlockSpec((1, tt, LANES), lambda b, i: (b, i, 0)), _mod_spec(g2, tt), full((1, d)), full((1, d))],
        out_specs=row,
        out_shape=jax.ShapeDtypeStruct((bsz, t, d), F32),
        compiler_params=_cp(("parallel", "parallel")),
        name="final",
    )(x1, *ffs, rt, g2, lng, lnb)


def _bias_chain(dist, tab_ref, h0, n_heads_tab, group, thr):
    out = [jnp.full(dist.shape, tab_ref[h0 + r], F32) for r in range(group)]
    for b in range(1, len(thr)):
        ind = dist >= thr[b]
        out = [jnp.where(ind, tab_ref[b * n_heads_tab + h0 + r], out[r]) for r in range(group)]
    return out


def _nsa_seq_kernel(tab_ref, qt_ref, gt_ref, kc_ref, vct_ref, ks_ref, vst_ref, kw_ref, vwt_ref, toe_ref, o_ref,
                    sel_ref, flag_ref, ms_ref, accs_ref, mw_ref, accw_ref,
                    *, tq, group, dh, n_heads, thr, cmp_block, n_cmp_real, n_blk):
    g = pl.program_id(1)
    it = pl.program_id(2)
    q0 = it * tq
    h0 = g * group
    n_bkt = len(thr)
    far = thr[-1]
    bpt = tq // SEL_BLOCK
    n_tiles = ks_ref.shape[0] // tq
    assert far <= tq + 1 and WINDOW % tq == 0 and WINDOW >= 2 * tq - 1

    def lanes(parts):
        return jnp.concatenate(parts, axis=1)

    q4t = lanes([qt_ref[0, r * dh:(r + 1) * dh, :] for r in range(group)])
    far_bias = lanes([jnp.full((1, tq), tab_ref[(n_bkt - 1) * n_heads + h0 + r], F32) for r in range(group)])
    qi = lax.broadcasted_iota(jnp.int32, (1, tq), 1)
    ki = lax.broadcasted_iota(jnp.int32, (tq, 1), 0)
    qpos = q0 + qi

    ncp = kc_ref.shape[0]
    cidx = lax.broadcasted_iota(jnp.int32, (ncp, 1), 0)
    dist_c = qpos - (cidx * CMP_STRIDE + (cmp_block - 1))
    madd_c = jnp.where((dist_c >= 0) & (cidx < n_cmp_real), 0.0, NEG_INF)
    bias_c = _bias_chain(dist_c, tab_ref, h0, n_heads, group, thr)
    s_c = jnp.dot(kc_ref[...], q4t, preferred_element_type=F32) + lanes([b + madd_c for b in bias_c])
    m_c = jnp.maximum(jnp.max(s_c, axis=0, keepdims=True), M_INIT)
    e_c = jnp.exp(s_c - m_c)
    z_c = jnp.sum(e_c, axis=0, keepdims=True)
    p_c = e_c / jnp.where(z_c > 0, z_c, 1.0)
    o_c = jnp.dot(vct_ref[...], p_c.astype(BF16), preferred_element_type=F32)

    psum = sum(p_c[:, r * tq:(r + 1) * tq] for r in range(group))
    bj = lax.broadcasted_iota(jnp.int32, (n_blk, ncp), 0) * SEL_BLOCK
    ci = lax.broadcasted_iota(jnp.int32, (n_blk, ncp), 1) * CMP_STRIDE
    overlap_t = jnp.where((ci < bj + SEL_BLOCK) & (ci + cmp_block > bj), 1.0, 0.0)
    imp = jnp.dot(overlap_t, psum, preferred_element_type=F32, precision=HIGHEST)
    blk = lax.broadcasted_iota(jnp.int32, (n_blk, tq), 0)
    cur = qpos // SEL_BLOCK
    forced = (blk == 0) | ((blk >= cur - 1) & (blk <= cur))
    valid = blk * SEL_BLOCK <= qpos
    score = jnp.where(forced, SEL_BIG, jnp.where(valid, imp, -SEL_BIG))
    sel = jnp.full((n_blk, tq), NEG_INF, F32)
    for _ in range(min(N_SELECT, n_blk)):
        mx = jnp.max(score, axis=0, keepdims=True)
        idx = jnp.min(jnp.where(score == mx, blk, n_blk), axis=0, keepdims=True)
        hit = blk == idx
        sel = jnp.where(hit, 0.0, sel)
        score = jnp.where(hit, -jnp.inf, score)
    sel_ref[...] = sel
    for t in range(n_tiles):
        flag_ref[t] = jnp.max(sel[t * bpt:(t + 1) * bpt, :])

    ms_ref[...] = jnp.full(ms_ref.shape, M_INIT, F32)
    mw_ref[...] = jnp.full(mw_ref.shape, M_INIT, F32)
    accs_ref[...] = jnp.zeros(accs_ref.shape, F32)
    accw_ref[...] = jnp.zeros(accw_ref.shape, F32)
    ones = jnp.ones((SUBLANES_BF16, tq), BF16)

    def sel_madd(t):
        b0 = t * bpt
        return jnp.concatenate([jnp.broadcast_to(sel_ref[pl.ds(b0 + j, 1), :], (SEL_BLOCK, tq)) for j in range(bpt)], axis=0)

    def toeplitz(kind):
        return lanes([toe_ref[r, kind] for r in range(group)])

    def attend(k_ref, vt_ref, t, bias, madd, m_ref, acc_ref):
        kt = k_ref[pl.ds(pl.multiple_of(t * tq, tq), tq), :]
        s = jnp.dot(kt, q4t, preferred_element_type=F32) + bias
        if madd is not None:
            s = s + lanes([madd] * group)
        m_old = m_ref[...]
        m_new = jnp.maximum(m_old, jnp.max(s, axis=0, keepdims=True))
        e = jnp.exp(s - m_new).astype(BF16)
        vt = jnp.concatenate([vt_ref[t], ones], axis=0)
        acc_ref[...] = jnp.exp(m_old - m_new) * acc_ref[...] + jnp.dot(vt, e, preferred_element_type=F32)
        m_ref[...] = m_new

    causal = jnp.where(qi >= ki, 0.0, NEG_INF)

    def far_body(t, carry):
        @pl.when(flag_ref[t] > -1.0)
        def _():
            attend(ks_ref, vst_ref, t, far_bias, sel_madd(t), ms_ref, accs_ref)
        return carry

    lax.fori_loop(0, jnp.maximum(it - 1, 0), far_body, 0)

    t_prev = jnp.maximum(it - 1, 0)
    kill_prev = jnp.where(it >= 1, 0.0, NEG_INF)
    attend(ks_ref, vst_ref, t_prev, toeplitz(1), sel_madd(t_prev) + kill_prev, ms_ref, accs_ref)
    attend(ks_ref, vst_ref, it, toeplitz(0), sel_madd(it) + causal, ms_ref, accs_ref)

    for j in range(WINDOW // tq + 1):
        off = WINDOW - j * tq
        t_w = jnp.maximum(it - off // tq, 0)
        madd = None
        if off == 0:
            bias, madd = toeplitz(0), causal
        elif off == tq:
            bias = toeplitz(1)
        else:
            bias = far_bias
            if off + tq - 1 > WINDOW:
                madd = jnp.where(off + qi - ki <= WINDOW, 0.0, NEG_INF)
        if off > 0:
            kill = jnp.where(it >= off // tq, 0.0, NEG_INF)
            madd = kill if madd is None else madd + kill
            if madd.shape != (tq, tq):
                madd = jnp.broadcast_to(madd, (tq, tq))
        attend(kw_ref, vwt_ref, t_w, bias, madd, mw_ref, accw_ref)

    def finish(acc_ref):
        acc = acc_ref[...]
        l = acc[dh:dh + 1, :]
        return acc[:dh, :] / jnp.where(l > 0, l, 1.0)

    def gate(j):
        return lanes([gt_ref[0, r * 3 + j:r * 3 + j + 1, :] for r in range(group)])

    ot = gate(0) * o_c + gate(1) * finish(accs_ref) + gate(2) * finish(accw_ref)
    heads_per_store = LANES // dh
    for r in range(0, group, heads_per_store):
        stacked = jnp.concatenate([ot[:, (r + u) * tq:(r + u + 1) * tq] for u in range(heads_per_store)], axis=0)
        o_ref[0, :, r * dh:(r + heads_per_store) * dh] = stacked.T


def _nsa_seq(tab, qt, gt, kc, vct, kk, vt, toe, *, group, dh, cmp_block, n_cmp_real, tq):
    bsz, hd, s = qt.shape
    n_kv = hd // (group * dh)
    n_heads = n_kv * group
    ncp = kc.shape[2]
    n_tiles = s // tq
    n_blk = -(-s // SEL_BLOCK // 8) * 8
    thr = tuple(_rel_thresholds(tab.shape[0] // n_heads))
    rows = group * tq
    gpad = gt.shape[1] // n_kv
    assert s % tq == 0 and tq % SEL_BLOCK == 0 and dh * (LANES // dh) == LANES and group % (LANES // dh) == 0
    sq = lambda *dims: pl.BlockSpec(tuple(None if d == 0 else d for d in dims[:-1]), dims[-1])
    kern = functools.partial(_nsa_seq_kernel, tq=tq, group=group, dh=dh, n_heads=n_heads, thr=thr, cmp_block=cmp_block,
                             n_cmp_real=n_cmp_real, n_blk=n_blk)
    return pl.pallas_call(
        kern,
        grid=(bsz, n_kv, n_tiles),
        in_specs=[
            pl.BlockSpec(memory_space=pltpu.SMEM),
            pl.BlockSpec((1, group * dh, tq), lambda b, g, i: (b, g, i)),
            pl.BlockSpec((1, gpad, tq), lambda b, g, i: (b, g, i)),
            pl.BlockSpec((None, None, ncp, dh), lambda b, g, i: (b, g, 0, 0)),
            pl.BlockSpec((None, None, dh, ncp), lambda b, g, i: (b, g, 0, 0)),
            pl.BlockSpec((None, None, None, s, dh), lambda b, g, i: (0, b, g, 0, 0)),
            pl.BlockSpec((None, None, n_tiles, dh, tq), lambda b, g, i: (b, g, 0, 0, 0)),
            pl.BlockSpec((None, None, None, s, dh), lambda b, g, i: (1, b, g, 0, 0)),
            pl.BlockSpec((None, None, n_tiles, dh, tq), lambda b, g, i: (b, n_kv + g, 0, 0, 0)),
            pl.BlockSpec((group, 2, tq, tq), lambda b, g, i: (g, 0, 0, 0)),
        ],
        out_specs=pl.BlockSpec((1, tq, group * dh), lambda b, g, i: (b, i, g)),
        out_shape=jax.ShapeDtypeStruct((bsz, s, hd), F32),
        scratch_shapes=[
            pltpu.VMEM((n_blk, tq), F32),
            pltpu.SMEM((n_tiles,), F32),
            pltpu.VMEM((1, rows), F32),
            pltpu.VMEM((dh + SUBLANES_BF16, rows), F32),
            pltpu.VMEM((1, rows), F32),
            pltpu.VMEM((dh + SUBLANES_BF16, rows), F32),
        ],
        compiler_params=_cp(("parallel", "parallel", "arbitrary")),
        name="nsa_seq",
    )(tab, qt, gt, kc, vct, kk, vt, kk, vt, toe)


def _compress(k_head, length, w1, b1, w2, b2, pe):
    bsz, _, n_kv, dh = k_head.shape
    cmp_block = pe.shape[0]
    ratio = cmp_block // CMP_STRIDE
    hidden = w1.shape[-1]
    nc = (length - cmp_block) // CMP_STRIDE + 1
    r = bsz * n_kv
    x = k_head[:, :cmp_block].reshape(bsz, ratio, CMP_STRIDE, n_kv, dh).transpose(1, 0, 3, 2, 4).reshape(ratio, r, CMP_STRIDE * dh)
    w1r = w1.reshape(ratio, CMP_STRIDE * dh, hidden)
    part = [_linear(x[n], w1r, jnp.zeros((ratio, 1, hidden), F32)) for n in range(ratio)]
    const = _linear(pe.reshape(1, -1), w1.reshape(1, cmp_block * dh, hidden), b1.reshape(1, 1, hidden))[0]
    pre = sum(part[u][u:u + nc] for u in range(ratio)) + const
    n_cmp = pre.shape[0]
    out = _linear(pre.reshape(n_cmp * r, hidden), w2[None], b2.reshape(1, 1, dh), act_in='gelu')[0]
    return out.reshape(n_cmp, bsz, n_kv, dh).transpose(1, 0, 2, 3)


def _nsa_dec_kernel(pt_ref, qbd_ref, gt_ref, kct_ref, vcf_ref, bc_ref, bs_ref, bw_ref, bn_ref, win_ref, new_ref, *rest,
                    n_pages, group, n_kv, dh, cmp_block, qpos):
    page_refs, o_ref = rest[:n_pages], rest[n_pages]
    n_heads = group * n_kv
    gd = n_kv * dh
    page = page_refs[0].shape[4]
    qbd = qbd_ref[0]
    new = new_ref[0]

    s_c = jnp.dot(qbd, kct_ref[0], preferred_element_type=F32) + bc_ref[...]
    m_c = jnp.maximum(jnp.max(s_c, axis=-1, keepdims=True), M_INIT)
    e_c = jnp.exp(s_c - m_c)
    z_c = jnp.sum(e_c, axis=-1, keepdims=True)
    p_c = e_c / jnp.where(z_c > 0, z_c, 1.0)
    o_c = jnp.dot(p_c.astype(BF16), vcf_ref[0], preferred_element_type=F32)

    ncp = p_c.shape[1]
    g_of_h = lax.broadcasted_iota(jnp.int32, (n_kv, n_heads), 1) // group
    gsel = jnp.where(g_of_h == lax.broadcasted_iota(jnp.int32, (n_kv, n_heads), 0), 1.0, 0.0)
    psum = jnp.dot(gsel, p_c, preferred_element_type=F32, precision=HIGHEST)
    ci = lax.broadcasted_iota(jnp.int32, (ncp, LANES), 0) * CMP_STRIDE
    bj = lax.broadcasted_iota(jnp.int32, (ncp, LANES), 1) * SEL_BLOCK
    overlap = jnp.where((ci < bj + SEL_BLOCK) & (ci + cmp_block > bj), 1.0, 0.0)
    imp = jnp.dot(psum, overlap, preferred_element_type=F32, precision=HIGHEST)
    n_blk = qpos // SEL_BLOCK + 1
    assert n_blk <= LANES
    cur = qpos // SEL_BLOCK
    blk = lax.broadcasted_iota(jnp.int32, (n_kv, LANES), 1)
    forced = (blk == 0) | ((blk >= cur - 1) & (blk <= cur))
    valid = blk * SEL_BLOCK <= qpos
    score = jnp.where(forced, SEL_BIG, jnp.where(valid, imp, -SEL_BIG))
    score = jnp.where(blk < n_blk, score, -jnp.inf)
    si = lax.broadcasted_iota(jnp.int32, (LANES, LANES), 0)
    li = lax.broadcasted_iota(jnp.int32, (LANES, LANES), 1)
    sel_rows = []
    for g in range(n_kv):
        a = jnp.broadcast_to(score[g:g + 1, :], (LANES, LANES))
        bcol = jnp.sum(jnp.where(si == li, a, 0.0), axis=1, keepdims=True)
        ahead = (a > bcol) | ((a == bcol) & (li < si))
        rank = jnp.sum(jnp.where(ahead, 1.0, 0.0), axis=1, keepdims=True)
        picked = jnp.where(rank < min(N_SELECT, n_blk), 1.0, 0.0)
        sel_rows.append(jnp.sum(jnp.where(si == li, picked, 0.0), axis=0, keepdims=True))
    sel = jnp.concatenate(sel_rows, axis=0)
    h_of_g = lax.broadcasted_iota(jnp.int32, (n_heads, n_kv), 0) // group
    gsel_t = jnp.where(h_of_g == lax.broadcasted_iota(jnp.int32, (n_heads, n_kv), 1), 1.0, 0.0)
    madd_blk = (jnp.dot(gsel_t, sel, preferred_element_type=F32) - 1.0) * -NEG_INF

    def attend(kts, vts, bias, s_new, k_new, v_new):
        s = jnp.concatenate([jnp.dot(qbd, kt, preferred_element_type=F32) for kt in kts], axis=1) + bias
        s_new = s_new + jnp.sum(qbd.astype(F32) * k_new.astype(BF16).astype(F32), axis=-1, keepdims=True)
        m = jnp.maximum(jnp.maximum(jnp.max(s, axis=-1, keepdims=True), s_new), M_INIT)
        e = jnp.exp(s - m)
        e_new = jnp.exp(s_new - m)
        l = jnp.sum(e, axis=-1, keepdims=True) + e_new
        acc = e_new.astype(BF16).astype(F32) * v_new.astype(BF16).astype(F32)
        off = 0
        for vt in vts:
            n = vt.shape[1]
            acc = acc + lax.dot_general(e[:, off:off + n].astype(BF16), vt, NT_DIMS, preferred_element_type=F32)
            off += n
        return acc / jnp.where(l > 0, l, 1.0)

    n_keys = n_pages * page
    ej = lax.broadcasted_iota(jnp.int32, (LANES, n_keys), 0)
    ek = lax.broadcasted_iota(jnp.int32, (LANES, n_keys), 1) // SEL_BLOCK
    expand = jnp.where(ej == ek, 1.0, 0.0).astype(BF16)
    madd = jnp.dot((madd_blk == 0.0).astype(BF16), expand, preferred_element_type=F32)
    madd = (madd - 1.0) * -NEG_INF
    pair = 2 if n_pages % 2 == 0 else 1
    tile = lambda refs, u: jnp.concatenate([r[0, u].reshape(gd, page) for r in refs], axis=1).astype(BF16)
    groups = [page_refs[i:i + pair] for i in range(0, n_pages, pair)]
    o_s = attend([tile(rs, 0) for rs in groups], [tile(rs, 1) for rs in groups], bs_ref[...] + madd,
                 bn_ref[...] + madd_blk[:, cur:cur + 1], new[:, 2 * gd:3 * gd], new[:, 3 * gd:4 * gd])

    n_buf = win_ref.shape[4]
    o_w = attend([win_ref[0, 0].reshape(gd, n_buf).astype(BF16)], [win_ref[0, 1].reshape(gd, n_buf).astype(BF16)],
                 bw_ref[...], bn_ref[...], new[:, 4 * gd:5 * gd], new[:, 5 * gd:6 * gd])

    gt = gt_ref[0]
    o = gt[:, 0:1] * o_c + gt[:, 1:2] * o_s + gt[:, 2:3] * o_w
    hg = lax.broadcasted_iota(jnp.int32, (n_heads, 1), 0) // group
    o_ref[0] = sum(jnp.where(hg == g, o[:, g * dh:(g + 1) * dh], 0.0) for g in range(n_kv))


def _nsa_dec(page_table, qbd, gates, kct, vcf, bias_c, bias_s, bias_w, bias_n, cache_win2, kv_new, cache2,
             *, group, n_kv, dh, cmp_block, qpos):
    bsz, n_heads, gd = qbd.shape
    n_pages = page_table.shape[1]
    page = cache2.shape[4]
    n_buf = cache_win2.shape[4]
    assert page % SEL_BLOCK == 0 and n_pages * page == qpos and cache2.shape[1] == 4
    full = lambda a: pl.BlockSpec(a.shape, lambda b, pt: (0,) * a.ndim)
    per_b = lambda a: pl.BlockSpec((1,) + a.shape[1:], lambda b, pt: (b,) + (0,) * (a.ndim - 1))
    page_specs = [pl.BlockSpec((1, 2, n_kv, dh, page), functools.partial(lambda b, pt, p: (pt[b, p], 1, 0, 0, 0), p=p))
                  for p in range(n_pages)]
    win_spec = pl.BlockSpec((1, 2, n_kv, dh, n_buf), lambda b, pt: (b, 0, 0, 0, 0))
    grid_spec = pltpu.PrefetchScalarGridSpec(
        num_scalar_prefetch=1,
        grid=(bsz,),
        in_specs=[per_b(qbd), per_b(gates), per_b(kct), per_b(vcf), full(bias_c), full(bias_s), full(bias_w), full(bias_n),
                  win_spec, per_b(kv_new), *page_specs],
        out_specs=pl.BlockSpec((1, n_heads, dh), lambda b, pt: (b, 0, 0)),
    )
    kern = functools.partial(_nsa_dec_kernel, n_pages=n_pages, group=group, n_kv=n_kv, dh=dh, cmp_block=cmp_block, qpos=qpos)
    return pl.pallas_call(
        kern,
        grid_spec=grid_spec,
        out_shape=jax.ShapeDtypeStruct((bsz, n_heads, dh), F32),
        compiler_params=_cp(("parallel",)),
        name="nsa_dec",
    )(page_table, qbd, gates, kct, vcf, bias_c, bias_s, bias_w, bias_n, cache_win2, kv_new, *([cache2] * n_pages))


def _static_bias(rel_table, dist, valid):
    thr = np.asarray(_rel_thresholds(rel_table.shape[0]))
    bucket = np.searchsorted(thr, np.maximum(dist, 0), side='right') - 1
    return jnp.where(jnp.asarray(valid)[None, :], rel_table.astype(F32).T[:, bucket], NEG_INF)


def _route(rt, n_experts, tm):
    n_tok = rt.shape[0]
    n_as = n_tok * TOP_K
    assert n_as % tm == 0
    n_blocks = n_as // tm
    flat_e = rt[:, TOP_K:2 * TOP_K].astype(jnp.int32).reshape(-1)
    order = jnp.argsort(flat_e)
    se = flat_e[order]
    tok_sorted = (order // TOP_K).astype(jnp.int32)
    dst_sorted = ((order % TOP_K) * n_tok + order // TOP_K).astype(jnp.int32).reshape(n_blocks, tm)
    experts = jnp.arange(n_experts)
    starts = jnp.searchsorted(se, experts, side='left').astype(jnp.int32)
    ends = jnp.searchsorted(se, experts, side='right').astype(jnp.int32)
    first_blk = starts // tm
    n_vis_e = jnp.where(ends > starts, (ends - 1) // tm - first_blk + 1, 0)
    cum = jnp.cumsum(n_vis_e)
    v = jnp.arange(n_blocks + n_experts - 1)
    ve = jnp.minimum(jnp.searchsorted(cum, v, side='right'), n_experts - 1)
    live = v < cum[-1]
    vb = first_blk[ve] + v - (cum[ve] - n_vis_e[ve])
    e_last = jnp.max(jnp.where(ends > starts, experts, 0))
    ve = jnp.where(live, ve, e_last).astype(jnp.int32)
    vb = jnp.where(live, vb, n_blocks - 1).astype(jnp.int32)
    lo = jnp.where(live, starts[ve], 0).astype(jnp.int32)
    hi = jnp.where(live, ends[ve], 0).astype(jnp.int32)
    i = jnp.arange(tm)[None, :]
    row = vb[:, None] * tm + i
    trash = n_as + (v % 2)[:, None] * tm + i
    dst = jnp.where((row >= lo[:, None]) & (row < hi[:, None]), dst_sorted[vb], trash).astype(jnp.int32)
    return tok_sorted, dst[:, None, :], (vb, ve, lo, hi)


def _split_mod(mod, bsz, per_row):
    parts = jnp.split(mod, 6, axis=-1)
    if per_row:
        return [p[None] for p in parts]
    return [p[:, None] for p in parts]


def _trunk(x, mods, kv_mod, conv0, h0, make_attend, p, *, per_row):
    depth = p['w_ada'].shape[0]
    n_a = p['w_in_a'].shape[0]
    bsz, t, d = x.shape
    n_experts = p['w_router'].shape[-1]
    alpha = (2 * depth) ** 0.25
    tt = min(256, t)
    conv_new, h_new = [], []
    attend, kv_state = None, None
    for l in range(depth):
        sh1, sc1, g1, sh2, sc2, g2 = _split_mod(mods[l], bsz, per_row)
        if l < n_a:
            win_bf = p['w_in_a'][l].astype(BF16)
            wg_bf = p['w_gate_a'][l].astype(BF16)
            args = (win_bf, p['w_conv'][l], p['b_conv'][l][None], wg_bf, p['b_gate_a'][l], p['lru_lambda'][l][None])
            if per_row:
                gated, cb, hl = _rglru_step(x[0], sc1[0], sh1[0], conv0[l].swapaxes(0, 1), h0[l], *args)
                gated, cb = gated[None], cb.swapaxes(0, 1)
            else:
                gated, cb, hl = _rglru_seq(x, sc1, sh1, conv0[l], h0[l][:, None], *args, tt=tt)
                hl = hl[:, 0]
            conv_new.append(cb)
            h_new.append(hl)
            mix, wout = gated, p['w_out_a'][l]
        else:
            if l == n_a:
                ksh, ksc = jnp.split(kv_mod, 2, axis=-1)
                ksh, ksc = (ksh[None], ksc[None]) if per_row else (ksh[:, None], ksc[:, None])
                attend, kv_state = make_attend(x, ksc, ksh)
            lb = l - n_a
            mix, wout = attend(x, sc1, sh1, p['w_in_b'][lb]), p['w_out_b'][lb]
        wr_pad = jnp.zeros((d, LANES), F32).at[:, :n_experts].set(p['w_router'][l])
        br_pad = jnp.zeros((1, LANES), F32).at[0, :n_experts].set(p['b_router'][l])
        x1, hin2, rt = _post(mix, x, g1, sc2, sh2, wout.astype(BF16), p['ln_g'][l, 0][None], p['ln_b'][l, 0][None],
                             wr_pad, br_pad, alpha=alpha, n_experts=n_experts, tt=tt)
        n_tok = bsz * t
        rt2 = rt.reshape(n_tok, LANES)
        moe_w = tuple(p[k].reshape((-1,) + p[k].shape[2:]) for k in ('w_gu', 'b_gu', 'w_down', 'b_down'))
        if per_row:
            ff = _moe_dense(hin2.reshape(n_tok, d), rt2, *moe_w, n_e=n_experts, e_off=l * n_experts).reshape(bsz, t, d)
            x = _final(x1, ff, rt, g2, p['ln_g'][l, 1][None], p['ln_b'][l, 1][None], alpha=alpha, combine=False, tt=tt)
        else:
            tm = math.gcd(MOE_TILE, n_tok * TOP_K)
            tok_sorted, dst, visits = _route(rt2, n_experts, tm)
            xs = hin2.reshape(n_tok, d)[tok_sorted]
            yg = _moe_grouped(visits, dst, xs, *moe_w, tm=tm, trash0=TOP_K * n_tok, e_off=l * n_experts)
            x = _final(x1, yg, rt, g2, p['ln_g'][l, 1][None], p['ln_b'][l, 1][None], alpha=alpha, combine=True, tt=tt)
    return x, jnp.stack(conv_new), jnp.stack(h_new), kv_state


def _qg_weight_t(w_in, n_kv, group, dh):
    d = w_in.shape[0]
    hd = n_kv * group * dh
    wg = w_in[:, hd:].reshape(d, n_kv, 3 * group)
    wg = jnp.pad(wg, ((0, 0), (0, 0), (0, 16 - 3 * group))).reshape(d, n_kv * 16)
    return jnp.concatenate([w_in[:, :hd], wg], axis=1).T.astype(BF16)


def _toeplitz_bias(rel_table, tq):
    n_buckets = rel_table.shape[0]
    thr = np.asarray(_rel_thresholds(n_buckets))
    kk = np.arange(tq)[:, None]
    ii = np.arange(tq)[None, :]
    dist = np.maximum(np.stack([ii - kk, tq + ii - kk]), 0)
    bucket = np.searchsorted(thr, dist, side='right') - 1
    return rel_table.astype(F32).T[:, bucket]


def kernel(x_prompt, x_sample, c_prompt, c_sample, state_conv, state_h, cache_kv, cache_win, page_table, w_ada, b_ada, ln_g, ln_b, w_in_a, w_conv, b_conv, w_gate_a, b_gate_a, lru_lambda, w_out_a, w_ada_kv, b_ada_kv, w_kv, cmp_w1, cmp_b1, cmp_w2, cmp_b2, cmp_pe, w_in_b, w_out_b, rel_table, w_router, b_router, w_gu, b_gu, w_down, b_down):
    p = dict(w_ada=w_ada, b_ada=b_ada, ln_g=ln_g, ln_b=ln_b, w_in_a=w_in_a, w_conv=w_conv, b_conv=b_conv,
             w_gate_a=w_gate_a, b_gate_a=b_gate_a, lru_lambda=lru_lambda, w_out_a=w_out_a, w_kv=w_kv,
             w_in_b=w_in_b, w_out_b=w_out_b, w_router=w_router, b_router=b_router, w_gu=w_gu, b_gu=b_gu,
             w_down=w_down, b_down=b_down)
    bsz, seq, d = x_prompt.shape
    dec_b = x_sample.shape[0]
    n_a = w_in_a.shape[0]
    cw = w_conv.shape[1]
    lru_w = w_conv.shape[2]
    n_kv, dh = cache_kv.shape[3], cache_kv.shape[4]
    n_heads = rel_table.shape[1]
    group = n_heads // n_kv
    hd = n_heads * dh
    gd = n_kv * dh
    n_parts = w_kv.shape[1] // gd
    assert n_parts == 6 and 3 * group <= 16
    cmp_block = cmp_pe.shape[1]
    scale = dh ** -0.5
    assert math.log2(scale) == round(math.log2(scale))

    c_all = jnp.concatenate([c_prompt, c_sample], axis=0)
    mods = _linear(c_all, w_ada, b_ada[:, None, :], act_in='silu')
    kv_mod = _linear(c_all, w_ada_kv[None], b_ada_kv[None, None, :], act_in='silu')[0]
    tab = rel_table.astype(F32).reshape(-1)

    def cmp_kv(k_head, v_head, length):
        kc = _compress(k_head, length, cmp_w1[0], cmp_b1[0], cmp_w2[0], cmp_b2[0], cmp_pe[0])
        vc = _compress(v_head, length, cmp_w1[1], cmp_b1[1], cmp_w2[1], cmp_b2[1], cmp_pe[1])
        return kc, vc

    def make_prompt_attend(x, ksc, ksh):
        tq = min(NSA_TILE, seq)
        wvt = jnp.concatenate([w_kv[:, 3 * gd:4 * gd], w_kv[:, 5 * gd:6 * gd]], axis=1).T.astype(BF16)
        kv, kk, vt = _kvproj(x, ksc, ksh, w_kv.astype(BF16), wvt, n_kv=n_kv, dh=dh, k_parts=(2, 4), tt=tq)
        kv6 = kv.reshape(bsz, seq, n_parts, n_kv, dh)
        kc, vc = cmp_kv(kv6[:, :cmp_block, 0], kv6[:, :cmp_block, 1], seq)
        n_cmp = kc.shape[1]
        ncp = -(-n_cmp // SUBLANES_BF16) * SUBLANES_BF16
        padc = lambda a: jnp.pad(a, ((0, 0), (0, ncp - n_cmp), (0, 0), (0, 0))).astype(BF16)
        kc_g = padc(kc).transpose(0, 2, 1, 3)
        vc_t = padc(vc).transpose(0, 2, 3, 1)
        toe = _toeplitz_bias(rel_table, tq)

        def attend(x, sc1, sh1, w_in):
            qt, gt = _qproj(x, sc1, sh1, _qg_weight_t(w_in, n_kv, group, dh), hd=hd, scale=scale, tt=tq)
            return _nsa_seq(tab, qt, gt, kc_g, vc_t, kk, vt, toe, group=group, dh=dh, cmp_block=cmp_block,
                            n_cmp_real=n_cmp, tq=tq)

        n_win = min(WINDOW, seq)
        return attend, (kv6[:, :, :4], kv6[:, seq - n_win:, 4:])

    def make_sample_attend(x, ksc, ksh):
        kv = _modlinear(x, ksc, ksh, w_kv.astype(BF16), tt=dec_b)
        kv6 = kv.reshape(dec_b, 1, n_parts, n_kv, dh)
        page = cache_kv.shape[1]
        past_len = page_table.shape[1] * page
        n_buf = cache_win.shape[1]
        assert page >= cmp_block and past_len >= cmp_block
        cache2 = cache_kv.transpose(0, 2, 3, 4, 1)
        cache_win2 = cache_win.transpose(0, 2, 3, 4, 1)
        head = cache2[page_table[:, 0], :2, :, :, :cmp_block].transpose(0, 1, 4, 2, 3)
        kc, vc = cmp_kv(head[:, 0], head[:, 1], past_len + 1)
        n_cmp = kc.shape[1]
        ncp = -(-n_cmp // SUBLANES_BF16) * SUBLANES_BF16
        padc = lambda a: jnp.pad(a, ((0, 0), (0, ncp - n_cmp), (0, 0), (0, 0))).astype(BF16).reshape(dec_b, ncp, gd)
        kct, vcf = padc(kc).transpose(0, 2, 1), padc(vc)
        c = np.arange(ncp)
        dist_c = past_len - (c * CMP_STRIDE + cmp_block - 1)
        bias_c = _static_bias(rel_table, dist_c, (dist_c >= 0) & (c < n_cmp))
        bias_s = _static_bias(rel_table, past_len - np.arange(past_len), np.ones(past_len, bool))
        dist_w = n_buf - np.arange(n_buf)
        bias_w = _static_bias(rel_table, dist_w, (dist_w <= WINDOW) & (past_len - dist_w >= 0))
        bias_n = rel_table.astype(F32)[0][:, None]
        win =jnp.concatenate([cache_win, kv6[:, :, 4:]], axis=1)
        own_group = (jnp.arange(n_heads)[:, None] // group == jnp.arange(n_kv)[None, :]).astype(BF16)

        def attend(x, sc1, sh1, w_in):
            qt, gt = _qproj(x, sc1, sh1, _qg_weight_t(w_in, n_kv, group, dh), hd=hd, scale=scale, tt=dec_b)
            q = qt[0].T.reshape(dec_b, n_heads, dh)
            qbd = (q[:, :, None, :] * own_group[None, :, :, None]).reshape(dec_b, n_heads, gd)
            gates = gt[0].reshape(n_kv, 16, dec_b)[:, :3 * group].transpose(2, 0, 1).reshape(dec_b, n_heads, 3)
            gates = jnp.pad(gates, ((0, 0), (0, 0), (0, LANES - 3)))
            o = _nsa_dec(page_table, qbd, gates, kct, vcf, bias_c, bias_s, bias_w, bias_n, cache_win2,
                         kv.reshape(dec_b, 1, n_parts * gd), cache2, group=group, n_kv=n_kv, dh=dh, cmp_block=cmp_block, qpos=past_len)
            return o.reshape(1, dec_b, hd)

        return attend, (kv6[:, :, :4], win[:, -n_buf:])

    conv0 = jnp.zeros((n_a, bsz, cw - 1, lru_w), F32)
    h0 = jnp.zeros((n_a, bsz, lru_w), F32)
    y_prompt, prompt_conv, prompt_h, (prompt_kv, prompt_win) = _trunk(
        x_prompt, mods[:, :bsz], kv_mod[:bsz], conv0, h0, make_prompt_attend, p, per_row=False)
    y_sample, sample_conv, sample_h, (sample_kv, sample_win) = _trunk(
        x_sample.reshape(1, dec_b, d), mods[:, bsz:], kv_mod[bsz:], state_conv, state_h, make_sample_attend, p, per_row=True)
    return (y_prompt, y_sample.reshape(dec_b, 1, d), prompt_conv, prompt_h, prompt_kv, prompt_win,
            sample_conv, sample_h, sample_kv, sample_win)
```

```python
import functools
import math

import numpy as np
import jax
import jax.numpy as jnp
from jax import lax
from jax.experimental import pallas as pl
from jax.experimental.pallas import tpu as pltpu

F32 = jnp.float32
BF16 = jnp.bfloat16
HIGHEST = lax.Precision.HIGHEST
NT_DIMS = (((1,), (1,)), ((), ()))

CMP_STRIDE = 16
SEL_BLOCK = 64
N_SELECT = 16
WINDOW = 512
REL_MAX_DIST = 128
TOP_K = 4
RGLRU_C = 8.0
SWIGLU_LIMIT = 7.0
SWIGLU_ALPHA = 1.702
MOE_TILE = 256
LN_EPS = 1e-5
SEL_BIG = 1e9
NEG_INF = -1e30
M_INIT = -1e29

LANES = 128
SUBLANES_BF16 = 16
VMEM_LIMIT = 48 * 1024 * 1024
NSA_TILE = 256


def _cp(sem, vmem=VMEM_LIMIT):
    return pltpu.CompilerParams(dimension_semantics=sem, vmem_limit_bytes=vmem)


def _rel_thresholds(n_buckets):
    exact = n_buckets // 2
    d_max = REL_MAX_DIST + 1
    buckets = []
    for d in range(d_max + 1):
        if d < exact:
            buckets.append(d)
        else:
            large = exact + int(math.log(max(d, 1) / exact) / math.log(REL_MAX_DIST / exact) * (n_buckets - exact))
            buckets.append(min(large, n_buckets - 1))
    return [next(d for d in range(d_max + 1) if buckets[d] >= b) for b in range(n_buckets)]


def _linear_kernel(x_ref, w_ref, b_ref, o_ref, *, act_in):
    x = x_ref[...]
    if act_in == 'silu':
        x = x * jax.nn.sigmoid(x)
    elif act_in == 'gelu':
        x = jax.nn.gelu(x)
    y = jnp.dot(x.astype(BF16), w_ref[0].astype(BF16), preferred_element_type=F32)
    o_ref[0] = y + b_ref[0]


def _linear(x, w, b, *, act_in=None, tn=512):
    m, k = x.shape
    nl, _, n = w.shape
    tn = min(tn, n)
    return pl.pallas_call(
        functools.partial(_linear_kernel, act_in=act_in),
        grid=(nl, n // tn),
        in_specs=[
            pl.BlockSpec((m, k), lambda l, j: (0, 0)),
            pl.BlockSpec((1, k, tn), lambda l, j: (l, 0, j)),
            pl.BlockSpec((1, 1, tn), lambda l, j: (l, 0, j)),
        ],
        out_specs=pl.BlockSpec((1, m, tn), lambda l, j: (l, 0, j)),
        out_shape=jax.ShapeDtypeStruct((nl, m, n), F32),
        compiler_params=_cp(("parallel", "parallel")),
        name="linear",
    )(x, w, b)


def _mod_spec(mod, tt):
    d = mod.shape[-1]
    if mod.shape[1] == 1:
        return pl.BlockSpec((1, 1, d), lambda b, i: (b, 0, 0))
    return pl.BlockSpec((1, tt, d), lambda b, i: (b, i, 0))


def _modlinear_kernel(x_ref, sc_ref, sh_ref, w_ref, o_ref):
    hin = x_ref[0] * (1.0 + sc_ref[0]) + sh_ref[0]
    o_ref[0] = jnp.dot(hin.astype(BF16), w_ref[...], preferred_element_type=F32)


def _modlinear(x, sc, sh, w_bf, *, tt):
    bsz, t, d = x.shape
    n = w_bf.shape[1]
    return pl.pallas_call(
        _modlinear_kernel,
        grid=(bsz, t // tt),
        in_specs=[
            pl.BlockSpec((1, tt, d), lambda b, i: (b, i, 0)),
            _mod_spec(sc, tt),
            _mod_spec(sh, tt),
            pl.BlockSpec((d, n), lambda b, i: (0, 0)),
        ],
        out_specs=pl.BlockSpec((1, tt, n), lambda b, i: (b, i, 0)),
        out_shape=jax.ShapeDtypeStruct((bsz, t, n), F32),
        compiler_params=_cp(("parallel", "parallel")),
        name="modlinear",
    )(x, sc, sh, w_bf)


def _kvproj_kernel(x_ref, sc_ref, sh_ref, w_ref, wvt_ref, kv_ref, kk_ref, vt_ref, *, n_kv, dh, k_parts):
    hin = (x_ref[0] * (1.0 + sc_ref[0]) + sh_ref[0]).astype(BF16)
    y = jnp.dot(hin, w_ref[...], preferred_element_type=F32)
    kv_ref[0] = y
    gd = n_kv * dh
    for j, part in enumerate(k_parts):
        for g in range(n_kv):
            kk_ref[j, 0, g] = y[:, part * gd + g * dh:part * gd + (g + 1) * dh].astype(BF16)
    yt = lax.dot_general(wvt_ref[...], hin, NT_DIMS, preferred_element_type=F32)
    for j in range(vt_ref.shape[1]):
        vt_ref[0, j, 0] = yt[j * dh:(j + 1) * dh].astype(BF16)


def _kvproj(x, sc, sh, w_bf, wvt_bf, *, n_kv, dh, k_parts, tt):
    bsz, t, d = x.shape
    n = w_bf.shape[1]
    nv = wvt_bf.shape[0] // dh
    return pl.pallas_call(
        functools.partial(_kvproj_kernel, n_kv=n_kv, dh=dh, k_parts=k_parts),
        grid=(bsz, t // tt),
        in_specs=[
            pl.BlockSpec((1, tt, d), lambda b, i: (b, i, 0)),
            _mod_spec(sc, tt),
            _mod_spec(sh, tt),
            pl.BlockSpec((d, n), lambda b, i: (0, 0)),
            pl.BlockSpec((nv * dh, d), lambda b, i: (0, 0)),
        ],
        out_specs=[
            pl.BlockSpec((1, tt, n), lambda b, i: (b, i, 0)),
            pl.BlockSpec((len(k_parts), 1, n_kv, tt, dh), lambda b, i: (0, b, 0, i, 0)),
            pl.BlockSpec((1, nv, 1, dh, tt), lambda b, i: (b, 0, i, 0, 0)),
        ],
        out_shape=[
            jax.ShapeDtypeStruct((bsz, t, n), F32),
            jax.ShapeDtypeStruct((len(k_parts), bsz, n_kv, t, dh), BF16),
            jax.ShapeDtypeStruct((bsz, nv, t // tt, dh, tt), BF16),
        ],
        compiler_params=_cp(("parallel", "parallel")),
        name="kvproj",
    )(x, sc, sh, w_bf, wvt_bf)


def _qproj_kernel(x_ref, sc_ref, sh_ref, wt_ref, qt_ref, gt_ref, *, hd, scale):
    hin = (x_ref[0] * (1.0 + sc_ref[0]) + sh_ref[0]).astype(BF16)
    yt = lax.dot_general(wt_ref[...], hin, NT_DIMS, preferred_element_type=F32)
    qt_ref[0] = (yt[:hd] * scale).astype(BF16)
    gt_ref[0] = jax.nn.sigmoid(yt[hd:])


def _qproj(x, sc, sh, wt_bf, *, hd, scale, tt):
    bsz, t, d = x.shape
    n = wt_bf.shape[0]
    return pl.pallas_call(
        functools.partial(_qproj_kernel, hd=hd, scale=scale),
        grid=(bsz, t // tt),
        in_specs=[
            pl.BlockSpec((1, tt, d), lambda b, i: (b, i, 0)),
            _mod_spec(sc, tt),
            _mod_spec(sh, tt),
            pl.BlockSpec((n, d), lambda b, i: (0, 0)),
        ],
        out_specs=[
            pl.BlockSpec((1, hd, tt), lambda b, i: (b, 0, i)),
            pl.BlockSpec((1, n - hd, tt), lambda b, i: (b, 0, i)),
        ],
        out_shape=[
            jax.ShapeDtypeStruct((bsz, hd, t), BF16),
            jax.ShapeDtypeStruct((bsz, n - hd, t), F32),
        ],
        compiler_params=_cp(("parallel", "parallel")),
        name="qproj",
    )(x, sc, sh, wt_bf)


def _log1p(x):
    u = 1.0 + x
    return jnp.where(u == 1.0, x, jnp.log(u) * x / jnp.where(u == 1.0, 1.0, u - 1.0))


def _expm1(x):
    u = jnp.exp(x)
    safe = (u != 1.0) & (u > 0.0)
    return jnp.where(u == 1.0, x, jnp.where(u > 0.0, (u - 1.0) * x / jnp.where(safe, jnp.log(u), 1.0), -1.0))


def _softplus(z):
    return jnp.maximum(z, 0.0) + _log1p(jnp.exp(-jnp.abs(z)))


def _rglru_gates(xc, wg_ref, bg_ref, lam_ref, n_heads):
    w = xc.shape[1]
    bw = w // n_heads
    xcb = xc.astype(BF16)
    gates = []
    for g in range(2):
        cols = [jnp.dot(xcb[:, i * bw:(i + 1) * bw], wg_ref[g, i], preferred_element_type=F32) for i in range(n_heads)]
        gates.append(jnp.concatenate(cols, axis=1) + bg_ref[g:g + 1, :])
    r = jax.nn.sigmoid(gates[0])
    i_g = jax.nn.sigmoid(gates[1])
    log_a = (-RGLRU_C * _softplus(-lam_ref[...])) * r
    a = jnp.exp(log_a)
    b_in = jnp.sqrt(-_expm1(2.0 * log_a)) * (i_g * xc)
    return a, b_in


def _rglru_seq_kernel(x_ref, sc_ref, sh_ref, conv0_ref, h0_ref, win_ref, wconv_ref, bconv_ref, wg_ref, bg_ref, lam_ref,
                      gated_ref, convout_ref, hout_ref, ext_ref, a_ref, b_ref, yb_ref, h_ref, *, tt, n_heads, cw):
    t = pl.program_id(1)
    w = a_ref.shape[1]
    pad = 8

    @pl.when(t == 0)
    def _():
        ext_ref[0:pad - (cw - 1), :] = jnp.zeros((pad - (cw - 1), w), F32)
        ext_ref[pad - (cw - 1):pad, :] = conv0_ref[0]
        h_ref[...] = h0_ref[0]

    hin = x_ref[0] * (1.0 + sc_ref[0]) + sh_ref[0]
    u = jnp.dot(hin.astype(BF16), win_ref[...], preferred_element_type=F32)
    yb_ref[...] = jax.nn.gelu(u[:, :w])
    ext_ref[pad:pad + tt, :] = u[:, w:]
    xc = bconv_ref[...] + sum(ext_ref[pad - (cw - 1) + k:pad - (cw - 1) + k + tt, :] * wconv_ref[k:k + 1, :] for k in range(cw))
    tail = ext_ref[tt:tt + pad, :]
    ext_ref[0:pad, :] = tail

    a, b_in = _rglru_gates(xc, wg_ref, bg_ref, lam_ref, n_heads)
    a_ref[...] = a
    b_ref[...] = b_in

    row = lax.broadcasted_iota(jnp.int32, (tt, LANES), 0)
    for c in range(w // LANES):
        cs = slice(c * LANES, (c + 1) * LANES)
        av = a_ref[:, cs]
        bv = b_ref[:, cs]
        s = 1
        while s < tt:
            keep = row >= s
            b_sh = jnp.where(keep, pltpu.roll(bv, s, 0), 0.0)
            a_sh = jnp.where(keep, pltpu.roll(av, s, 0), 1.0)
            bv = av * b_sh + bv
            av = av * a_sh
            s *= 2
        hs = av * h_ref[:, cs] + bv
        h_ref[:, cs] = hs[tt - 1:tt, :]
        gated_ref[0, :, cs] = (hs * yb_ref[:, cs]).astype(gated_ref.dtype)

    @pl.when(t == pl.num_programs(1) - 1)
    def _():
        convout_ref[0] = tail[pad - (cw - 1):pad, :]
        hout_ref[0] = h_ref[...]


def _rglru_seq(x, sc, sh, conv0, h0, win_bf, wconv, bconv, wg_bf, bg, lam, *, tt):
    bsz, t, d = x.shape
    cw, w = wconv.shape
    n_heads = wg_bf.shape[1]
    full = lambda shape: pl.BlockSpec(shape, lambda b, i: (0,) * len(shape))
    return pl.pallas_call(
        functools.partial(_rglru_seq_kernel, tt=tt, n_heads=n_heads, cw=cw),
        grid=(bsz, t // tt),
        in_specs=[
            pl.BlockSpec((1, tt, d), lambda b, i: (b, i, 0)),
            _mod_spec(sc, tt),
            _mod_spec(sh, tt),
            pl.BlockSpec((1, cw - 1, w), lambda b, i: (b, 0, 0)),
            pl.BlockSpec((1, 1, w), lambda b, i: (b, 0, 0)),
            full((d, 2 * w)),
            full((cw, w)),
            full((1, w)),
            full(wg_bf.shape),
            full((2, w)),
            full((1, w)),
        ],
        out_specs=[
            pl.BlockSpec((1, tt, w), lambda b, i: (b, i, 0)),
            pl.BlockSpec((1, cw - 1, w), lambda b, i: (b, 0, 0)),
            pl.BlockSpec((1, 1, w), lambda b, i: (b, 0, 0)),
        ],
        out_shape=[
            jax.ShapeDtypeStruct((bsz, t, w), BF16),
            jax.ShapeDtypeStruct((bsz, cw - 1, w), F32),
            jax.ShapeDtypeStruct((bsz, 1, w), F32),
        ],
        scratch_shapes=[
            pltpu.VMEM((tt + 8, w), F32),
            pltpu.VMEM((tt, w), F32),
            pltpu.VMEM((tt, w), F32),
            pltpu.VMEM((tt, w), F32),
            pltpu.VMEM((1, w), F32),
        ],
        compiler_params=_cp(("parallel", "arbitrary")),
        name="rglru_seq",
    )(x, sc, sh, conv0, h0, win_bf, wconv, bconv, wg_bf, bg, lam)


def _rglru_step_kernel(x_ref, sc_ref, sh_ref, conv_ref, h0_ref, win_ref, wconv_ref, bconv_ref, wg_ref, bg_ref, lam_ref,
                       gated_ref, convout_ref, hout_ref, *, n_heads, cw):
    w = h0_ref.shape[1]
    hin = x_ref[...] * (1.0 + sc_ref[...]) + sh_ref[...]
    u = jnp.dot(hin.astype(BF16), win_ref[...], preferred_element_type=F32)
    yb = jax.nn.gelu(u[:, :w])
    xb = u[:, w:]
    taps = [conv_ref[k] for k in range(cw - 1)] + [xb]
    xc = bconv_ref[...] + sum(taps[k] * wconv_ref[k:k + 1, :] for k in range(cw))
    a, b_in = _rglru_gates(xc, wg_ref, bg_ref, lam_ref, n_heads)
    h = a * h0_ref[...] + b_in
    gated_ref[...] = (h * yb).astype(gated_ref.dtype)
    for k in range(cw - 1):
        convout_ref[k] = taps[k + 1]
    hout_ref[...] = h


def _rglru_step(x, sc, sh, conv, h0, win_bf, wconv, bconv, wg_bf, bg, lam):
    r, _ = x.shape
    cw, w = wconv.shape
    n_heads = wg_bf.shape[1]
    return pl.pallas_call(
        functools.partial(_rglru_step_kernel, n_heads=n_heads, cw=cw),
        out_shape=[
            jax.ShapeDtypeStruct((r, w), BF16),
            jax.ShapeDtypeStruct((cw - 1, r, w), F32),
            jax.ShapeDtypeStruct((r, w), F32),
        ],
        compiler_params=pltpu.CompilerParams(vmem_limit_bytes=VMEM_LIMIT),
        name="rglru_step",
    )(x, sc, sh, conv, h0, win_bf, wconv, bconv, wg_bf, bg, lam)


def _layer_norm(v, g, b):
    mu = jnp.mean(v, axis=-1, keepdims=True)
    dv = v - mu
    var = jnp.mean(dv * dv, axis=-1, keepdims=True)
    return dv * lax.rsqrt(var + LN_EPS) * g + b


def _post_kernel(a_ref, x_ref, g1_ref, sc2_ref, sh2_ref, wout_ref, lng_ref, lnb_ref, wr_ref, br_ref,
                 x1_ref, hin2_ref, rt_ref, *, alpha, n_experts):
    y = jnp.dot(a_ref[0].astype(BF16), wout_ref[...], preferred_element_type=F32)
    x1 = _layer_norm(alpha * x_ref[0] + (1.0 + g1_ref[0]) * y, lng_ref[...], lnb_ref[...])
    x1_ref[0] = x1
    hin2 = x1 * (1.0 + sc2_ref[0]) + sh2_ref[0]
    hin2_ref[0] = hin2.astype(hin2_ref.dtype)
    logits = jnp.dot(hin2.astype(BF16), wr_ref[...].astype(BF16), preferred_element_type=F32) + br_ref[...]
    lane = lax.broadcasted_iota(jnp.int32, logits.shape, 1)
    work = jnp.where(lane < n_experts, logits, -jnp.inf)
    vals, idxs = [], []
    for _ in range(TOP_K):
        m = jnp.max(work, axis=-1, keepdims=True)
        idx = jnp.min(jnp.where(work == m, lane, LANES), axis=-1, keepdims=True)
        vals.append(m)
        idxs.append(idx)
        work = jnp.where(lane == idx, -jnp.inf, work)
    es = [jnp.exp(v - vals[0]) for v in vals]
    z = sum(es)
    rt = jnp.zeros(logits.shape, F32)
    for k in range(TOP_K):
        rt = jnp.where(lane == k, es[k] / z, rt)
        rt = jnp.where(lane == TOP_K + k, idxs[k].astype(F32), rt)
    rt_ref[0] = rt


def _post(a, x, g1, sc2, sh2, wout_bf, lng, lnb, wr_pad, br_pad, *, alpha, n_experts, tt):
    bsz, t, d = x.shape
    dk = a.shape[-1]
    full = lambda shape: pl.BlockSpec(shape, lambda b, i: (0,) * len(shape))
    row = pl.BlockSpec((1, tt, d), lambda b, i: (b, i, 0))
    return pl.pallas_call(
        functools.partial(_post_kernel, alpha=alpha, n_experts=n_experts),
        grid=(bsz, t // tt),
        in_specs=[
            pl.BlockSpec((1, tt, dk), lambda b, i: (b, i, 0)),
            row,
            _mod_spec(g1, tt),
            _mod_spec(sc2, tt),
            _mod_spec(sh2, tt),
            full((dk, d)),
            full((1, d)),
            full((1, d)),
            full((d, LANES)),
            full((1, LANES)),
        ],
        out_specs=[row, row, pl.BlockSpec((1, tt, LANES), lambda b, i: (b, i, 0))],
        out_shape=[
            jax.ShapeDtypeStruct((bsz, t, d), F32),
            jax.ShapeDtypeStruct((bsz, t, d), F32),
            jax.ShapeDtypeStruct((bsz, t, LANES), F32),
        ],
        compiler_params=_cp(("parallel", "parallel")),
        name="post",
    )(a, x, g1, sc2, sh2, wout_bf, lng, lnb, wr_pad, br_pad)


def _swiglu(g, u):
    g = jnp.minimum(g, SWIGLU_LIMIT)
    u = jnp.clip(u, -SWIGLU_LIMIT, SWIGLU_LIMIT)
    return g * jax.nn.sigmoid(SWIGLU_ALPHA * g) * (u + 1.0)


def _moe_grouped_kernel(vb_ref, ve_ref, lo_ref, hi_ref, dst_ref, xs_ref, wgu_ref, bgu_ref, wdn_ref, bdn_ref, yg_ref,
                        wgu_bf, wdn_bf, stage_ref, sem, *, chunk, tm, trash0):
    v = pl.program_id(0)
    last = pl.num_programs(0) - 1
    d, de2 = wgu_bf.shape
    de = de2 // 2
    e = ve_ref[v]
    pv = jnp.maximum(v - 1, 0)
    slot = v % 2

    def live(u):
        return hi_ref[u] > lo_ref[u]

    def wait_rows(s):
        pltpu.make_async_copy(stage_ref.at[s], yg_ref.at[pl.ds(0, tm)], sem.at[s]).wait()

    @pl.when((v == 0) | (e != ve_ref[pv]))
    def _():
        for c in range(de2 // chunk):
            wgu_bf[:, c * chunk:(c + 1) * chunk] = wgu_ref[0, :, c * chunk:(c + 1) * chunk].astype(BF16)
        for c in range(de // chunk):
            wdn_bf[c * chunk:(c + 1) * chunk, :] = wdn_ref[0, c * chunk:(c + 1) * chunk, :].astype(BF16)

    @pl.when(v == 0)
    def _():
        stage_ref[...] = jnp.zeros(stage_ref.shape, F32)
        for s in range(2):
            fill = pltpu.make_async_copy(stage_ref.at[s], yg_ref.at[pl.ds(trash0 + s * tm, tm)], sem.at[s])
            fill.start()
            fill.wait()

    @pl.when((v >= 2) & live(jnp.maximum(v - 2, 0)))
    def _():
        wait_rows(slot)

    lo = lo_ref[v]
    hi = hi_ref[v]

    @pl.when(hi > lo)
    def _():
        x = xs_ref[...].astype(BF16)
        acc = jnp.zeros((tm, d), F32) + bdn_ref[0]
        for c in range(de // chunk):
            g = jnp.dot(x, wgu_bf[:, c * chunk:(c + 1) * chunk], preferred_element_type=F32) + bgu_ref[0, :, c * chunk:(c + 1) * chunk]
            u = jnp.dot(x, wgu_bf[:, de + c * chunk:de + (c + 1) * chunk], preferred_element_type=F32) + bgu_ref[0, :, de + c * chunk:de + (c + 1) * chunk]
            h = _swiglu(g, u).astype(BF16)
            acc = acc + jnp.dot(h, wdn_bf[c * chunk:(c + 1) * chunk, :], preferred_element_type=F32)
        for s in range(2):
            @pl.when(slot == s)
            def _(s=s):
                stage_ref[s] = acc

                for i in range(tm):
                    pltpu.make_async_copy(stage_ref.at[s, pl.ds(i, 1)], yg_ref.at[pl.ds(dst_ref[0, 0, i], 1)],
                                          sem.at[s]).start()

    @pl.when(v == last)
    def _():
        @pl.when(hi > lo)
        def _():
            wait_rows(slot)

        @pl.when((v >= 1) & live(pv))
        def _():
            wait_rows(1 - slot)


def _moe_grouped(visits, dst, xs, w_gu, b_gu, w_dn, b_dn, *, tm, trash0, e_off):
    n_rows, d = xs.shape
    n_e, _, de2 = w_gu.shape
    de = de2 // 2
    n_vis = visits[0].shape[0]
    grid_spec = pltpu.PrefetchScalarGridSpec(
        num_scalar_prefetch=4,
        grid=(n_vis,),
        in_specs=[
            pl.BlockSpec((1, 1, tm), lambda v, vb, ve, lo, hi: (v, 0, 0), memory_space=pltpu.SMEM),
            pl.BlockSpec((tm, d), lambda v, vb, ve, lo, hi: (vb[v], 0)),
            pl.BlockSpec((1, d, de2), lambda v, vb, ve, lo, hi: (ve[v] + e_off, 0, 0)),
            pl.BlockSpec((1, 1, de2), lambda v, vb, ve, lo, hi: (ve[v] + e_off, 0, 0)),
            pl.BlockSpec((1, de, d), lambda v, vb, ve, lo, hi: (ve[v] + e_off, 0, 0)),
            pl.BlockSpec((1, 1, d), lambda v, vb, ve, lo, hi: (ve[v] + e_off, 0, 0)),
        ],
        out_specs=pl.BlockSpec(memory_space=pl.ANY),
        scratch_shapes=[
            pltpu.VMEM((d, de2), BF16),
            pltpu.VMEM((de, d), BF16),
            pltpu.VMEM((2, tm, d), F32),
            pltpu.SemaphoreType.DMA((2,)),
        ],
    )
    return pl.pallas_call(
        functools.partial(_moe_grouped_kernel, chunk=min(512, de), tm=tm, trash0=trash0),
        grid_spec=grid_spec,
        out_shape=jax.ShapeDtypeStruct((trash0 + 2 * tm, d), F32),
        compiler_params=_cp(("arbitrary",), 56 * 1024 * 1024),
        name="moe_grouped",
    )(*visits, dst, xs, w_gu, b_gu.reshape(n_e, 1, de2), w_dn, b_dn.reshape(n_e, 1, d))


def _moe_dense_kernel(x_ref, rt_ref, wg_ref, wu_ref, bg_ref, bu_ref, wdn_ref, bdn_ref, o_ref):
    e = pl.program_id(0)
    c = pl.program_id(1)

    @pl.when((e == 0) & (c == 0))
    def _():
        o_ref[...] = jnp.zeros(o_ref.shape, F32)

    rt = rt_ref[...]
    ef = e.astype(F32)
    gate = sum(jnp.where(rt[:, TOP_K + k:TOP_K + k + 1] == ef, rt[:, k:k + 1], 0.0) for k in range(TOP_K))
    x = x_ref[...].astype(BF16)
    g = jnp.dot(x, wg_ref[0].astype(BF16), preferred_element_type=F32) + bg_ref[0]
    u = jnp.dot(x, wu_ref[0].astype(BF16), preferred_element_type=F32) + bu_ref[0]
    h = _swiglu(g, u).astype(BF16)
    y = jnp.dot(h, wdn_ref[0].astype(BF16), preferred_element_type=F32)
    y = y + jnp.where(c == 0, 1.0, 0.0) * bdn_ref[0]
    o_ref[...] += gate * y


def _moe_dense(x_bf, rt, w_gu, b_gu, w_dn, b_dn, *, n_e, e_off, chunk=512):
    r, d = x_bf.shape
    n_all, _, de2 = w_gu.shape
    de = de2 // 2
    nc = de // chunk
    b_gu3 = b_gu.reshape(n_all, 1, de2)
    return pl.pallas_call(
        _moe_dense_kernel,
        grid=(n_e, nc),
        in_specs=[
            pl.BlockSpec((r, d), lambda e, c: (0, 0)),
            pl.BlockSpec((r, LANES), lambda e, c: (0, 0)),
            pl.BlockSpec((1, d, chunk), lambda e, c: (e + e_off, 0, c)),
            pl.BlockSpec((1, d, chunk), lambda e, c: (e + e_off, 0, nc + c)),
            pl.BlockSpec((1, 1, chunk), lambda e, c: (e + e_off, 0, c)),
            pl.BlockSpec((1, 1, chunk), lambda e, c: (e + e_off, 0, nc + c)),
            pl.BlockSpec((1, chunk, d), lambda e, c: (e + e_off, c, 0)),
            pl.BlockSpec((1, 1, d), lambda e, c: (e + e_off, 0, 0)),
        ],
        out_specs=pl.BlockSpec((r, d), lambda e, c: (0, 0)),
        out_shape=jax.ShapeDtypeStruct((r, d), F32),
        compiler_params=_cp(("arbitrary", "arbitrary")),
        name="moe_dense",
    )(x_bf, rt, w_gu, w_gu, b_gu3, b_gu3, w_dn, b_dn.reshape(n_all, 1, d))


def _final_kernel(x_ref, *refs, alpha, combine):
    if combine:
        ff_refs, (rt_ref, g2_ref, lng_ref, lnb_ref, o_ref) = refs[:TOP_K], refs[TOP_K:]
        rt = rt_ref[0]
        ff = sum(rt[:, k:k + 1] * ff_refs[k][...] for k in range(TOP_K))
    else:
        ff_ref, rt_ref, g2_ref, lng_ref, lnb_ref, o_ref = refs
        ff = ff_ref[0]
    o_ref[0] = _layer_norm(alpha * x_ref[0] + (1.0 + g2_ref[0]) * ff, lng_ref[...], lnb_ref[...])


def _final(x1, ff, rt, g2, lng, lnb, *, alpha, combine, tt):
    bsz, t, d = x1.shape
    row = pl.BlockSpec((1, tt, d), lambda b, i: (b, i, 0))
    nblk = bsz * t // tt
    if combine:
        ff_specs = [pl.BlockSpec((tt, d), functools.partial(lambda b, i, k: (k * nblk + b * (t // tt) + i, 0), k=k))
                    for k in range(TOP_K)]
        ffs = [ff] * TOP_K
    else:
        ff_specs, ffs = [row], [ff]
    full = lambda shape: pl.BlockSpec(shape, lambda b, i: (0,) * len(shape))
    return pl.pallas_call(
        functools.partial(_final_kernel, alpha=alpha, combine=combine),
        grid=(bsz, t // tt),
        in_specs=[row, *ff_specs, pl.BlockSpec((1, tt, LANES), lambda b, i: (b, i, 0)), _mod_spec(g2, tt), full((1, d)), full((1, d))],
        out_specs=row,
        out_shape=jax.ShapeDtypeStruct((bsz, t, d), F32),
        compiler_params=_cp(("parallel", "parallel")),
        name="final",
    )(x1, *ffs, rt, g2, lng, lnb)


def _bias_chain(dist, tab_ref, h0, n_heads_tab, group, thr):
    out = [jnp.full(dist.shape, tab_ref[h0 + r], F32) for r in range(group)]
    for b in range(1, len(thr)):
        ind = dist >= thr[b]
        out = [jnp.where(ind, tab_ref[b * n_heads_tab + h0 + r], out[r]) for r in range(group)]
    return out


def _nsa_seq_kernel(tab_ref, qt_ref, gt_ref, kc_ref, vct_ref, ks_ref, vst_ref, kw_ref, vwt_ref, toe_ref, o_ref,
                    sel_ref, flag_ref, ms_ref, accs_ref, mw_ref, accw_ref,
                    *, tq, group, dh, n_heads, thr, cmp_block, n_cmp_real, n_blk):
    g = pl.program_id(1)
    it = pl.program_id(2)
    q0 = it * tq
    h0 = g * group
    n_bkt = len(thr)
    far = thr[-1]
    bpt = tq // SEL_BLOCK
    n_tiles = ks_ref.shape[0] // tq
    assert far <= tq + 1 and WINDOW % tq == 0 and WINDOW >= 2 * tq - 1

    def lanes(parts):
        return jnp.concatenate(parts, axis=1)

    q4t = lanes([qt_ref[0, r * dh:(r + 1) * dh, :] for r in range(group)])
    far_bias = lanes([jnp.full((1, tq), tab_ref[(n_bkt - 1) * n_heads + h0 + r], F32) for r in range(group)])
    qi = lax.broadcasted_iota(jnp.int32, (1, tq), 1)
    ki = lax.broadcasted_iota(jnp.int32, (tq, 1), 0)
    qpos = q0 + qi

    ncp = kc_ref.shape[0]
    cidx = lax.broadcasted_iota(jnp.int32, (ncp, 1), 0)
    dist_c = qpos - (cidx * CMP_STRIDE + (cmp_block - 1))
    madd_c = jnp.where((dist_c >= 0) & (cidx < n_cmp_real), 0.0, NEG_INF)
    bias_c = _bias_chain(dist_c, tab_ref, h0, n_heads, group, thr)
    s_c = jnp.dot(kc_ref[...], q4t, preferred_element_type=F32) + lanes([b + madd_c for b in bias_c])
    m_c = jnp.maximum(jnp.max(s_c, axis=0, keepdims=True), M_INIT)
    e_c = jnp.exp(s_c - m_c)
    z_c = jnp.sum(e_c, axis=0, keepdims=True)
    p_c = e_c / jnp.where(z_c > 0, z_c, 1.0)
    o_c = jnp.dot(vct_ref[...], p_c.astype(BF16), preferred_element_type=F32)

    psum = sum(p_c[:, r * tq:(r + 1) * tq] for r in range(group))
    bj = lax.broadcasted_iota(jnp.int32, (n_blk, ncp), 0) * SEL_BLOCK
    ci = lax.broadcasted_iota(jnp.int32, (n_blk, ncp), 1) * CMP_STRIDE
    overlap_t = jnp.where((ci < bj + SEL_BLOCK) & (ci + cmp_block > bj), 1.0, 0.0)
    imp = jnp.dot(overlap_t, psum, preferred_element_type=F32, precision=HIGHEST)
    blk = lax.broadcasted_iota(jnp.int32, (n_blk, tq), 0)
    cur = qpos // SEL_BLOCK
    forced = (blk == 0) | ((blk >= cur - 1) & (blk <= cur))
    valid = blk * SEL_BLOCK <= qpos
    score = jnp.where(forced, SEL_BIG, jnp.where(valid, imp, -SEL_BIG))
    sel = jnp.full((n_blk, tq), NEG_INF, F32)
    for _ in range(min(N_SELECT, n_blk)):
        mx = jnp.max(score, axis=0, keepdims=True)
        idx = jnp.min(jnp.where(score == mx, blk, n_blk), axis=0, keepdims=True)
        hit = blk == idx
        sel = jnp.where(hit, 0.0, sel)
        score = jnp.where(hit, -jnp.inf, score)
    sel_ref[...] = sel
    for t in range(n_tiles):
        flag_ref[t] = jnp.max(sel[t * bpt:(t + 1) * bpt, :])

    ms_ref[...] = jnp.full(ms_ref.shape, M_INIT, F32)
    mw_ref[...] = jnp.full(mw_ref.shape, M_INIT, F32)
    accs_ref[...] = jnp.zeros(accs_ref.shape, F32)
    accw_ref[...] = jnp.zeros(accw_ref.shape, F32)
    ones = jnp.ones((SUBLANES_BF16, tq), BF16)

    def sel_madd(t):
        b0 = t * bpt
        return jnp.concatenate([jnp.broadcast_to(sel_ref[pl.ds(b0 + j, 1), :], (SEL_BLOCK, tq)) for j in range(bpt)], axis=0)

    def toeplitz(kind):
        return lanes([toe_ref[r, kind] for r in range(group)])

    def attend(k_ref, vt_ref, t, bias, madd, m_ref, acc_ref):
        kt = k_ref[pl.ds(pl.multiple_of(t * tq, tq), tq), :]
        s = jnp.dot(kt, q4t, preferred_element_type=F32) + bias
        if madd is not None:
            s = s + lanes([madd] * group)
        m_old = m_ref[...]
        m_new = jnp.maximum(m_old, jnp.max(s, axis=0, keepdims=True))
        e = jnp.exp(s - m_new).astype(BF16)
        vt = jnp.concatenate([vt_ref[t], ones], axis=0)
        acc_ref[...] = jnp.exp(m_old - m_new) * acc_ref[...] + jnp.dot(vt, e, preferred_element_type=F32)
        m_ref[...] = m_new

    causal = jnp.where(qi >= ki, 0.0, NEG_INF)

    def far_body(t, carry):
        @pl.when(flag_ref[t] > -1.0)
        def _():
            attend(ks_ref, vst_ref, t, far_bias, sel_madd(t), ms_ref, accs_ref)
        return carry

    lax.fori_loop(0, jnp.maximum(it - 1, 0), far_body, 0)

    t_prev = jnp.maximum(it - 1, 0)
    kill_prev = jnp.where(it >= 1, 0.0, NEG_INF)
    attend(ks_ref, vst_ref, t_prev, toeplitz(1), sel_madd(t_prev) + kill_prev, ms_ref, accs_ref)
    attend(ks_ref, vst_ref, it, toeplitz(0), sel_madd(it) + causal, ms_ref, accs_ref)

    for j in range(WINDOW // tq + 1):
        off = WINDOW - j * tq
        t_w = jnp.maximum(it - off // tq, 0)
        madd = None
        if off == 0:
            bias, madd = toeplitz(0), causal
        elif off == tq:
            bias = toeplitz(1)
        else:
            bias = far_bias
            if off + tq - 1 > WINDOW:
                madd = jnp.where(off + qi - ki <= WINDOW, 0.0, NEG_INF)
        if off > 0:
            kill = jnp.where(it >= off // tq, 0.0, NEG_INF)
            madd = kill if madd is None else madd + kill
            if madd.shape != (tq, tq):
                madd = jnp.broadcast_to(madd, (tq, tq))
        attend(kw_ref, vwt_ref, t_w, bias, madd, mw_ref, accw_ref)

    def finish(acc_ref):
        acc = acc_ref[...]
        l = acc[dh:dh + 1, :]
        return acc[:dh, :] / jnp.where(l > 0, l, 1.0)

    def gate(j):
        return lanes([gt_ref[0, r * 3 + j:r * 3 + j + 1, :] for r in range(group)])

    ot = gate(0) * o_c + gate(1) * finish(accs_ref) + gate(2) * finish(accw_ref)
    heads_per_store = LANES // dh
    for r in range(0, group, heads_per_store):
        stacked = jnp.concatenate([ot[:, (r + u) * tq:(r + u + 1) * tq] for u in range(heads_per_store)], axis=0)
        o_ref[0, :, r * dh:(r + heads_per_store) * dh] = stacked.T


def _nsa_seq(tab, qt, gt, kc, vct, kk, vt, toe, *, group, dh, cmp_block, n_cmp_real, tq):
    bsz, hd, s = qt.shape
    n_kv = hd // (group * dh)
    n_heads = n_kv * group
    ncp = kc.shape[2]
    n_tiles = s // tq
    n_blk = -(-s // SEL_BLOCK // 8) * 8
    thr = tuple(_rel_thresholds(tab.shape[0] // n_heads))
    rows = group * tq
    gpad = gt.shape[1] // n_kv
    assert s % tq == 0 and tq % SEL_BLOCK == 0 and dh * (LANES // dh) == LANES and group % (LANES // dh) == 0
    sq = lambda *dims: pl.BlockSpec(tuple(None if d == 0 else d for d in dims[:-1]), dims[-1])
    kern = functools.partial(_nsa_seq_kernel, tq=tq, group=group, dh=dh, n_heads=n_heads, thr=thr, cmp_block=cmp_block,
                             n_cmp_real=n_cmp_real, n_blk=n_blk)
    return pl.pallas_call(
        kern,
        grid=(bsz, n_kv, n_tiles),
        in_specs=[
            pl.BlockSpec(memory_space=pltpu.SMEM),
            pl.BlockSpec((1, group * dh, tq), lambda b, g, i: (b, g, i)),
            pl.BlockSpec((1, gpad, tq), lambda b, g, i: (b, g, i)),
            pl.BlockSpec((None, None, ncp, dh), lambda b, g, i: (b, g, 0, 0)),
            pl.BlockSpec((None, None, dh, ncp), lambda b, g, i: (b, g, 0, 0)),
            pl.BlockSpec((None, None, None, s, dh), lambda b, g, i: (0, b, g, 0, 0)),
            pl.BlockSpec((None, None, n_tiles, dh, tq), lambda b, g, i: (b, g, 0, 0, 0)),
            pl.BlockSpec((None, None, None, s, dh), lambda b, g, i: (1, b, g, 0, 0)),
            pl.BlockSpec((None, None, n_tiles, dh, tq), lambda b, g, i: (b, n_kv + g, 0, 0, 0)),
            pl.BlockSpec((group, 2, tq, tq), lambda b, g, i: (g, 0, 0, 0)),
        ],
        out_specs=pl.BlockSpec((1, tq, group * dh), lambda b, g, i: (b, i, g)),
        out_shape=jax.ShapeDtypeStruct((bsz, s, hd), F32),
        scratch_shapes=[
            pltpu.VMEM((n_blk, tq), F32),
            pltpu.SMEM((n_tiles,), F32),
            pltpu.VMEM((1, rows), F32),
            pltpu.VMEM((dh + SUBLANES_BF16, rows), F32),
            pltpu.VMEM((1, rows), F32),
            pltpu.VMEM((dh + SUBLANES_BF16, rows), F32),
        ],
        compiler_params=_cp(("parallel", "parallel", "arbitrary")),
        name="nsa_seq",
    )(tab, qt, gt, kc, vct, kk, vt, kk, vt, toe)


def _compress(k_head, length, w1, b1, w2, b2, pe):
    bsz, _, n_kv, dh = k_head.shape
    cmp_block = pe.shape[0]
    ratio = cmp_block // CMP_STRIDE
    hidden = w1.shape[-1]
    nc = (length - cmp_block) // CMP_STRIDE + 1
    r = bsz * n_kv
    x = k_head[:, :cmp_block].reshape(bsz, ratio, CMP_STRIDE, n_kv, dh).transpose(1, 0, 3, 2, 4).reshape(ratio, r, CMP_STRIDE * dh)
    w1r = w1.reshape(ratio, CMP_STRIDE * dh, hidden)
    part = [_linear(x[n], w1r, jnp.zeros((ratio, 1, hidden), F32)) for n in range(ratio)]
    const = _linear(pe.reshape(1, -1), w1.reshape(1, cmp_block * dh, hidden), b1.reshape(1, 1, hidden))[0]
    pre = sum(part[u][u:u + nc] for u in range(ratio)) + const
    n_cmp = pre.shape[0]
    out = _linear(pre.reshape(n_cmp * r, hidden), w2[None], b2.reshape(1, 1, dh), act_in='gelu')[0]
    return out.reshape(n_cmp, bsz, n_kv, dh).transpose(1, 0, 2, 3)


def _nsa_dec_kernel(pt_ref, qbd_ref, gt_ref, kct_ref, vcf_ref, bc_ref, bs_ref, bw_ref, bn_ref, win_ref, new_ref, *rest,
                    n_pages, group, n_kv, dh, cmp_block, qpos):
    page_refs, o_ref = rest[:n_pages], rest[n_pages]
    n_heads = group * n_kv
    gd = n_kv * dh
    page = page_refs[0].shape[4]
    qbd = qbd_ref[0]
    new = new_ref[0]

    s_c = jnp.dot(qbd, kct_ref[0], preferred_element_type=F32) + bc_ref[...]
    m_c = jnp.maximum(jnp.max(s_c, axis=-1, keepdims=True), M_INIT)
    e_c = jnp.exp(s_c - m_c)
    z_c = jnp.sum(e_c, axis=-1, keepdims=True)
    p_c = e_c / jnp.where(z_c > 0, z_c, 1.0)
    o_c = jnp.dot(p_c.astype(BF16), vcf_ref[0], preferred_element_type=F32)

    ncp = p_c.shape[1]
    g_of_h = lax.broadcasted_iota(jnp.int32, (n_kv, n_heads), 1) // group
    gsel = jnp.where(g_of_h == lax.broadcasted_iota(jnp.int32, (n_kv, n_heads), 0), 1.0, 0.0)
    psum = jnp.dot(gsel, p_c, preferred_element_type=F32, precision=HIGHEST)
    ci = lax.broadcasted_iota(jnp.int32, (ncp, LANES), 0) * CMP_STRIDE
    bj = lax.broadcasted_iota(jnp.int32, (ncp, LANES), 1) * SEL_BLOCK
    overlap = jnp.where((ci < bj + SEL_BLOCK) & (ci + cmp_block > bj), 1.0, 0.0)
    imp = jnp.dot(psum, overlap, preferred_element_type=F32, precision=HIGHEST)
    n_blk = qpos // SEL_BLOCK + 1
    assert n_blk <= LANES
    cur = qpos // SEL_BLOCK
    blk = lax.broadcasted_iota(jnp.int32, (n_kv, LANES), 1)
    forced = (blk == 0) | ((blk >= cur - 1) & (blk <= cur))
    valid = blk * SEL_BLOCK <= qpos
    score = jnp.where(forced, SEL_BIG, jnp.where(valid, imp, -SEL_BIG))
    score = jnp.where(blk < n_blk, score, -jnp.inf)
    si = lax.broadcasted_iota(jnp.int32, (LANES, LANES), 0)
    li = lax.broadcasted_iota(jnp.int32, (LANES, LANES), 1)
    sel_rows = []
    for g in range(n_kv):
        a = jnp.broadcast_to(score[g:g + 1, :], (LANES, LANES))
        bcol = jnp.sum(jnp.where(si == li, a, 0.0), axis=1, keepdims=True)
        ahead = (a > bcol) | ((a == bcol) & (li < si))
        rank = jnp.sum(jnp.where(ahead, 1.0, 0.0), axis=1, keepdims=True)
        picked = jnp.where(rank < min(N_SELECT, n_blk), 1.0, 0.0)
        sel_rows.append(jnp.sum(jnp.where(si == li, picked, 0.0), axis=0, keepdims=True))
    sel = jnp.concatenate(sel_rows, axis=0)
    h_of_g = lax.broadcasted_iota(jnp.int32, (n_heads, n_kv), 0) // group
    gsel_t = jnp.where(h_of_g == lax.broadcasted_iota(jnp.int32, (n_heads, n_kv), 1), 1.0, 0.0)
    madd_blk = (jnp.dot(gsel_t, sel, preferred_element_type=F32) - 1.0) * -NEG_INF

    def attend(kts, vts, bias, s_new, k_new, v_new):
        s = jnp.concatenate([jnp.dot(qbd, kt, preferred_element_type=F32) for kt in kts], axis=1) + bias
        s_new = s_new + jnp.sum(qbd.astype(F32) * k_new.astype(BF16).astype(F32), axis=-1, keepdims=True)
        m = jnp.maximum(jnp.maximum(jnp.max(s, axis=-1, keepdims=True), s_new), M_INIT)
        e = jnp.exp(s - m)
        e_new = jnp.exp(s_new - m)
        l = jnp.sum(e, axis=-1, keepdims=True) + e_new
        acc = e_new.astype(BF16).astype(F32) * v_new.astype(BF16).astype(F32)
        off = 0
        for vt in vts:
            n = vt.shape[1]
            acc = acc + lax.dot_general(e[:, off:off + n].astype(BF16), vt, NT_DIMS, preferred_element_type=F32)
            off += n
        return acc / jnp.where(l > 0, l, 1.0)

    n_keys = n_pages * page
    ej = lax.broadcasted_iota(jnp.int32, (LANES, n_keys), 0)
    ek = lax.broadcasted_iota(jnp.int32, (LANES, n_keys), 1) // SEL_BLOCK
    expand = jnp.where(ej == ek, 1.0, 0.0).astype(BF16)
    madd = jnp.dot((madd_blk == 0.0).astype(BF16), expand, preferred_element_type=F32)
    madd = (madd - 1.0) * -NEG_INF
    pair = 2 if n_pages % 2 == 0 else 1
    tile = lambda refs, u: jnp.concatenate([r[0, u].reshape(gd, page) for r in refs], axis=1).astype(BF16)
    groups = [page_refs[i:i + pair] for i in range(0, n_pages, pair)]
    o_s = attend([tile(rs, 0) for rs in groups], [tile(rs, 1) for rs in groups], bs_ref[...] + madd,
                 bn_ref[...] + madd_blk[:, cur:cur + 1], new[:, 2 * gd:3 * gd], new[:, 3 * gd:4 * gd])

    n_buf = win_ref.shape[4]
    o_w = attend([win_ref[0, 0].reshape(gd, n_buf).astype(BF16)], [win_ref[0, 1].reshape(gd, n_buf).astype(BF16)],
                 bw_ref[...], bn_ref[...], new[:, 4 * gd:5 * gd], new[:, 5 * gd:6 * gd])

    gt = gt_ref[0]
    o = gt[:, 0:1] * o_c + gt[:, 1:2] * o_s + gt[:, 2:3] * o_w
    hg = lax.broadcasted_iota(jnp.int32, (n_heads, 1), 0) // group
    o_ref[0] = sum(jnp.where(hg == g, o[:, g * dh:(g + 1) * dh], 0.0) for g in range(n_kv))


def _nsa_dec(page_table, qbd, gates, kct, vcf, bias_c, bias_s, bias_w, bias_n, cache_win2, kv_new, cache2,
             *, group, n_kv, dh, cmp_block, qpos):
    bsz, n_heads, gd = qbd.shape
    n_pages = page_table.shape[1]
    page = cache2.shape[4]
    n_buf = cache_win2.shape[4]
    assert page % SEL_BLOCK == 0 and n_pages * page == qpos and cache2.shape[1] == 4
    full = lambda a: pl.BlockSpec(a.shape, lambda b, pt: (0,) * a.ndim)
    per_b = lambda a: pl.BlockSpec((1,) + a.shape[1:], lambda b, pt: (b,) + (0,) * (a.ndim - 1))
    page_specs = [pl.BlockSpec((1, 2, n_kv, dh, page), functools.partial(lambda b, pt, p: (pt[b, p], 1, 0, 0, 0), p=p))
                  for p in range(n_pages)]
    win_spec = pl.BlockSpec((1, 2, n_kv, dh, n_buf), lambda b, pt: (b, 0, 0, 0, 0))
    grid_spec = pltpu.PrefetchScalarGridSpec(
        num_scalar_prefetch=1,
        grid=(bsz,),
        in_specs=[per_b(qbd), per_b(gates), per_b(kct), per_b(vcf), full(bias_c), full(bias_s), full(bias_w), full(bias_n),
                  win_spec, per_b(kv_new), *page_specs],
        out_specs=pl.BlockSpec((1, n_heads, dh), lambda b, pt: (b, 0, 0)),
    )
    kern = functools.partial(_nsa_dec_kernel, n_pages=n_pages, group=group, n_kv=n_kv, dh=dh, cmp_block=cmp_block, qpos=qpos)
    return pl.pallas_call(
        kern,
        grid_spec=grid_spec,
        out_shape=jax.ShapeDtypeStruct((bsz, n_heads, dh), F32),
        compiler_params=_cp(("parallel",)),
        name="nsa_dec",
    )(page_table, qbd, gates, kct, vcf, bias_c, bias_s, bias_w, bias_n, cache_win2, kv_new, *([cache2] * n_pages))


def _static_bias(rel_table, dist, valid):
    thr = np.asarray(_rel_thresholds(rel_table.shape[0]))
    bucket = np.searchsorted(thr, np.maximum(dist, 0), side='right') - 1
    return jnp.where(jnp.asarray(valid)[None, :], rel_table.astype(F32).T[:, bucket], NEG_INF)


def _route(rt, n_experts, tm):
    n_tok = rt.shape[0]
    n_as = n_tok * TOP_K
    assert n_as % tm == 0
    n_blocks = n_as // tm
    flat_e = rt[:, TOP_K:2 * TOP_K].astype(jnp.int32).reshape(-1)
    assert n_experts * n_as < 2 ** 31
    packed = jnp.sort(flat_e * n_as + jnp.arange(n_as, dtype=jnp.int32))
    order = packed % n_as
    se = packed // n_as
    tok_sorted = (order // TOP_K).astype(jnp.int32)
    dst_sorted = ((order % TOP_K) * n_tok + order // TOP_K).astype(jnp.int32).reshape(n_blocks, tm)
    experts = jnp.arange(n_experts)
    starts = jnp.searchsorted(se, experts, side='left').astype(jnp.int32)
    ends = jnp.searchsorted(se, experts, side='right').astype(jnp.int32)
    first_blk = starts // tm
    n_vis_e = jnp.where(ends > starts, (ends - 1) // tm - first_blk + 1, 0)
    cum = jnp.cumsum(n_vis_e)
    v = jnp.arange(n_blocks + n_experts - 1)
    ve = jnp.minimum(jnp.searchsorted(cum, v, side='right'), n_experts - 1)
    live = v < cum[-1]
    vb = first_blk[ve] + v - (cum[ve] - n_vis_e[ve])
    e_last = jnp.max(jnp.where(ends > starts, experts, 0))
    ve = jnp.where(live, ve, e_last).astype(jnp.int32)
    vb = jnp.where(live, vb, n_blocks - 1).astype(jnp.int32)
    lo = jnp.where(live, starts[ve], 0).astype(jnp.int32)
    hi = jnp.where(live, ends[ve], 0).astype(jnp.int32)
    i = jnp.arange(tm)[None, :]
    row = vb[:, None] * tm + i
    trash = n_as + (v % 2)[:, None] * tm + i
    dst = jnp.where((row >= lo[:, None]) & (row < hi[:, None]), dst_sorted[vb], trash).astype(jnp.int32)
    return tok_sorted, dst[:, None, :], (vb, ve, lo, hi)


def _split_mod(mod, bsz, per_row):
    parts = jnp.split(mod, 6, axis=-1)
    if per_row:
        return [p[None] for p in parts]
    return [p[:, None] for p in parts]


def _trunk(x, mods, kv_mod, conv0, h0, make_attend, p, *, per_row):
    depth = p['w_ada'].shape[0]
    n_a = p['w_in_a'].shape[0]
    bsz, t, d = x.shape
    n_experts = p['w_router'].shape[-1]
    alpha = (2 * depth) ** 0.25
    tt = min(256, t)
    conv_new, h_new = [], []
    attend, kv_state = None, None
    for l in range(depth):
        sh1, sc1, g1, sh2, sc2, g2 = _split_mod(mods[l], bsz, per_row)
        if l < n_a:
            win_bf = p['w_in_a'][l].astype(BF16)
            wg_bf = p['w_gate_a'][l].astype(BF16)
            args = (win_bf, p['w_conv'][l], p['b_conv'][l][None], wg_bf, p['b_gate_a'][l], p['lru_lambda'][l][None])
            if per_row:
                gated, cb, hl = _rglru_step(x[0], sc1[0], sh1[0], conv0[l].swapaxes(0, 1), h0[l], *args)
                gated, cb = gated[None], cb.swapaxes(0, 1)
            else:
                gated, cb, hl = _rglru_seq(x, sc1, sh1, conv0[l], h0[l][:, None], *args, tt=tt)
                hl = hl[:, 0]
            conv_new.append(cb)
            h_new.append(hl)
            mix, wout = gated, p['w_out_a'][l]
        else:
            if l == n_a:
                ksh, ksc = jnp.split(kv_mod, 2, axis=-1)
                ksh, ksc = (ksh[None], ksc[None]) if per_row else (ksh[:, None], ksc[:, None])
                attend, kv_state = make_attend(x, ksc, ksh)
            lb = l - n_a
            mix, wout = attend(x, sc1, sh1, p['w_in_b'][lb]), p['w_out_b'][lb]
        wr_pad = jnp.zeros((d, LANES), F32).at[:, :n_experts].set(p['w_router'][l])
        br_pad = jnp.zeros((1, LANES), F32).at[0, :n_experts].set(p['b_router'][l])
        x1, hin2, rt = _post(mix, x, g1, sc2, sh2, wout.astype(BF16), p['ln_g'][l, 0][None], p['ln_b'][l, 0][None],
                             wr_pad, br_pad, alpha=alpha, n_experts=n_experts, tt=tt)
        n_tok = bsz * t
        rt2 = rt.reshape(n_tok, LANES)
        moe_w = tuple(p[k].reshape((-1,) + p[k].shape[2:]) for k in ('w_gu', 'b_gu', 'w_down', 'b_down'))
        if per_row:
            ff = _moe_dense(hin2.reshape(n_tok, d), rt2, *moe_w, n_e=n_experts, e_off=l * n_experts).reshape(bsz, t, d)
            x = _final(x1, ff, rt, g2, p['ln_g'][l, 1][None], p['ln_b'][l, 1][None], alpha=alpha, combine=False, tt=tt)
        else:
            tm = math.gcd(MOE_TILE, n_tok * TOP_K)
            tok_sorted, dst, visits = _route(rt2, n_experts, tm)
            xs = hin2.reshape(n_tok, d)[tok_sorted]
            yg = _moe_grouped(visits, dst, xs, *moe_w, tm=tm, trash0=TOP_K * n_tok, e_off=l * n_experts)
            x = _final(x1, yg, rt, g2, p['ln_g'][l, 1][None], p['ln_b'][l, 1][None], alpha=alpha, combine=True, tt=tt)
    return x, jnp.stack(conv_new), jnp.stack(h_new), kv_state


def _qg_weight_t(w_in, n_kv, group, dh):
    d = w_in.shape[0]
    hd = n_kv * group * dh
    wg = w_in[:, hd:].reshape(d, n_kv, 3 * group)
    wg = jnp.pad(wg, ((0, 0), (0, 0), (0, 16 - 3 * group))).reshape(d, n_kv * 16)
    return jnp.concatenate([w_in[:, :hd], wg], axis=1).T.astype(BF16)


def _toeplitz_bias(rel_table, tq):
    thr = _rel_thresholds(rel_table.shape[0])
    kk = lax.broadcasted_iota(jnp.int32, (2, tq, tq), 1)
    ii = lax.broadcasted_iota(jnp.int32, (2, tq, tq), 2)
    dist = lax.broadcasted_iota(jnp.int32, (2, tq, tq), 0) * tq + ii - kk
    tab = rel_table.astype(F32)[:, :, None, None, None]
    bias = jnp.broadcast_to(tab[0], (rel_table.shape[1], 2, tq, tq))
    for b in range(1, len(thr)):
        bias = jnp.where(dist[None] >= thr[b], tab[b], bias)
    return bias


def kernel(x_prompt, x_sample, c_prompt, c_sample, state_conv, state_h, cache_kv, cache_win, page_table, w_ada, b_ada, ln_g, ln_b, w_in_a, w_conv, b_conv, w_gate_a, b_gate_a, lru_lambda, w_out_a, w_ada_kv, b_ada_kv, w_kv, cmp_w1, cmp_b1, cmp_w2, cmp_b2, cmp_pe, w_in_b, w_out_b, rel_table, w_router, b_router, w_gu, b_gu, w_down, b_down):
    p = dict(w_ada=w_ada, b_ada=b_ada, ln_g=ln_g, ln_b=ln_b, w_in_a=w_in_a, w_conv=w_conv, b_conv=b_conv,
             w_gate_a=w_gate_a, b_gate_a=b_gate_a, lru_lambda=lru_lambda, w_out_a=w_out_a, w_kv=w_kv,
             w_in_b=w_in_b, w_out_b=w_out_b, w_router=w_router, b_router=b_router, w_gu=w_gu, b_gu=b_gu,
             w_down=w_down, b_down=b_down)
    bsz, seq, d = x_prompt.shape
    dec_b = x_sample.shape[0]
    n_a = w_in_a.shape[0]
    cw = w_conv.shape[1]
    lru_w = w_conv.shape[2]
    n_kv, dh = cache_kv.shape[3], cache_kv.shape[4]
    n_heads = rel_table.shape[1]
    group = n_heads // n_kv
    hd = n_heads * dh
    gd = n_kv * dh
    n_parts = w_kv.shape[1] // gd
    assert n_parts == 6 and 3 * group <= 16
    cmp_block = cmp_pe.shape[1]
    scale = dh ** -0.5
    assert math.log2(scale) == round(math.log2(scale))

    c_all = jnp.concatenate([c_prompt, c_sample], axis=0)
    mods = _linear(c_all, w_ada, b_ada[:, None, :], act_in='silu')
    kv_mod = _linear(c_all, w_ada_kv[None], b_ada_kv[None, None, :], act_in='silu')[0]
    tab = rel_table.astype(F32).reshape(-1)

    def cmp_kv(k_head, v_head, length):
        kc = _compress(k_head, length, cmp_w1[0], cmp_b1[0], cmp_w2[0], cmp_b2[0], cmp_pe[0])
        vc = _compress(v_head, length, cmp_w1[1], cmp_b1[1], cmp_w2[1], cmp_b2[1], cmp_pe[1])
        return kc, vc

    def make_prompt_attend(x, ksc, ksh):
        tq = min(NSA_TILE, seq)
        wvt = jnp.concatenate([w_kv[:, 3 * gd:4 * gd], w_kv[:, 5 * gd:6 * gd]], axis=1).T.astype(BF16)
        kv, kk, vt = _kvproj(x, ksc, ksh, w_kv.astype(BF16), wvt, n_kv=n_kv, dh=dh, k_parts=(2, 4), tt=tq)
        kv6 = kv.reshape(bsz, seq, n_parts, n_kv, dh)
        kc, vc = cmp_kv(kv6[:, :cmp_block, 0], kv6[:, :cmp_block, 1], seq)
        n_cmp = kc.shape[1]
        ncp = -(-n_cmp // SUBLANES_BF16) * SUBLANES_BF16
        padc = lambda a: jnp.pad(a, ((0, 0), (0, ncp - n_cmp), (0, 0), (0, 0))).astype(BF16)
        kc_g = padc(kc).transpose(0, 2, 1, 3)
        vc_t = padc(vc).transpose(0, 2, 3, 1)
        toe = _toeplitz_bias(rel_table, tq)

        def attend(x, sc1, sh1, w_in):
            qt, gt = _qproj(x, sc1, sh1, _qg_weight_t(w_in, n_kv, group, dh), hd=hd, scale=scale, tt=tq)
            return _nsa_seq(tab, qt, gt, kc_g, vc_t, kk, vt, toe, group=group, dh=dh, cmp_block=cmp_block,
                            n_cmp_real=n_cmp, tq=tq)

        n_win = min(WINDOW, seq)
        return attend, (kv6[:, :, :4], kv6[:, seq - n_win:, 4:])

    def make_sample_attend(x, ksc, ksh):
        kv = _modlinear(x, ksc, ksh, w_kv.astype(BF16), tt=dec_b)
        kv6 = kv.reshape(dec_b, 1, n_parts, n_kv, dh)
        page = cache_kv.shape[1]
        past_len = page_table.shape[1] * page
        n_buf = cache_win.shape[1]
        assert page >= cmp_block and past_len >= cmp_block
        cache2 = cache_kv.transpose(0, 2, 3, 4, 1)
        cache_win2 = cache_win.transpose(0, 2, 3, 4, 1)
        head = cache2[page_table[:, 0], :2, :, :, :cmp_block].transpose(0, 1, 4, 2, 3)
        kc, vc = cmp_kv(head[:, 0], head[:, 1], past_len + 1)
        n_cmp = kc.shape[1]
        ncp = -(-n_cmp // SUBLANES_BF16) * SUBLANES_BF16
        padc = lambda a: jnp.pad(a, ((0, 0), (0, ncp - n_cmp), (0, 0), (0, 0))).astype(BF16).reshape(dec_b, ncp, gd)
        kct, vcf = padc(kc).transpose(0, 2, 1), padc(vc)
        c = np.arange(ncp)
        dist_c = past_len - (c * CMP_STRIDE + cmp_block - 1)
        bias_c = _static_bias(rel_table, dist_c, (dist_c >= 0) & (c < n_cmp))
        bias_s = _static_bias(rel_table, past_len - np.arange(past_len), np.ones(past_len, bool))
        dist_w = n_buf - np.arange(n_buf)
        bias_w = _static_bias(rel_table, dist_w, (dist_w <= WINDOW) & (past_len - dist_w >= 0))
        bias_n = rel_table.astype(F32)[0][:, None]
        win =jnp.concatenate([cache_win, kv6[:, :, 4:]], axis=1)
        own_group = (jnp.arange(n_heads)[:, None] // group == jnp.arange(n_kv)[None, :]).astype(BF16)

        def attend(x, sc1, sh1, w_in):
            qt, gt = _qproj(x, sc1, sh1, _qg_weight_t(w_in, n_kv, group, dh), hd=hd, scale=scale, tt=dec_b)
            q = qt[0].T.reshape(dec_b, n_heads, dh)
            qbd = (q[:, :, None, :] * own_group[None, :, :, None]).reshape(dec_b, n_heads, gd)
            gates = gt[0].reshape(n_kv, 16, dec_b)[:, :3 * group].transpose(2, 0, 1).reshape(dec_b, n_heads, 3)
            gates = jnp.pad(gates, ((0, 0), (0, 0), (0, LANES - 3)))
            o = _nsa_dec(page_table, qbd, gates, kct, vcf, bias_c, bias_s, bias_w, bias_n, cache_win2,
                         kv.reshape(dec_b, 1, n_parts * gd), cache2, group=group, n_kv=n_kv, dh=dh, cmp_block=cmp_block, qpos=past_len)
            return o.reshape(1, dec_b, hd)

        return attend, (kv6[:, :, :4], win[:, -n_buf:])

    conv0 = jnp.zeros((n_a, bsz, cw - 1, lru_w), F32)
    h0 = jnp.zeros((n_a, bsz, lru_w), F32)
    y_prompt, prompt_conv, prompt_h, (prompt_kv, prompt_win) = _trunk(
        x_prompt, mods[:, :bsz], kv_mod[:bsz], conv0, h0, make_prompt_attend, p, per_row=False)
    y_sample, sample_conv, sample_h, (sample_kv, sample_win) = _trunk(
        x_sample.reshape(1, dec_b, d), mods[:, bsz:], kv_mod[bsz:], state_conv, state_h, make_sample_attend, p, per_row=True)
    return (y_prompt, y_sample.reshape(dec_b, 1, d), prompt_conv, prompt_h, prompt_kv, prompt_win,
            sample_conv, sample_h, sample_kv, sample_win)
```

```python
import functools
import math

import numpy as np
import jax
import jax.numpy as jnp
from jax import lax
from jax.experimental import pallas as pl
from jax.experimental.pallas import tpu as pltpu

F32 = jnp.float32
BF16 = jnp.bfloat16
HIGHEST = lax.Precision.HIGHEST
NT_DIMS = (((1,), (1,)), ((), ()))

CMP_STRIDE = 16
SEL_BLOCK = 64
N_SELECT = 16
WINDOW = 512
REL_MAX_DIST = 128
TOP_K = 4
RGLRU_C = 8.0
SWIGLU_LIMIT = 7.0
SWIGLU_ALPHA = 1.702
MOE_TILE = 256
LN_EPS = 1e-5
SEL_BIG = 1e9
NEG_INF = -1e30
M_INIT = -1e29

LANES = 128
SUBLANES_BF16 = 16
VMEM_LIMIT = 48 * 1024 * 1024
NSA_TILE = 256


def _cp(sem, vmem=VMEM_LIMIT):
    return pltpu.CompilerParams(dimension_semantics=sem, vmem_limit_bytes=vmem)


def _rel_thresholds(n_buckets):
    exact = n_buckets // 2
    d_max = REL_MAX_DIST + 1
    buckets = []
    for d in range(d_max + 1):
        if d < exact:
            buckets.append(d)
        else:
            large = exact + int(math.log(max(d, 1) / exact) / math.log(REL_MAX_DIST / exact) * (n_buckets - exact))
            buckets.append(min(large, n_buckets - 1))
    return [next(d for d in range(d_max + 1) if buckets[d] >= b) for b in range(n_buckets)]


def _linear_kernel(x_ref, w_ref, b_ref, o_ref, *, act_in):
    x = x_ref[...]
    if act_in == 'silu':
        x = x * jax.nn.sigmoid(x)
    elif act_in == 'gelu':
        x = jax.nn.gelu(x)
    y = jnp.dot(x.astype(BF16), w_ref[0].astype(BF16), preferred_element_type=F32)
    o_ref[0] = y + b_ref[0]


def _linear(x, w, b, *, act_in=None, tn=512):
    m, k = x.shape
    nl, _, n = w.shape
    tn = min(tn, n)
    return pl.pallas_call(
        functools.partial(_linear_kernel, act_in=act_in),
        grid=(nl, n // tn),
        in_specs=[
            pl.BlockSpec((m, k), lambda l, j: (0, 0)),
            pl.BlockSpec((1, k, tn), lambda l, j: (l, 0, j)),
            pl.BlockSpec((1, 1, tn), lambda l, j: (l, 0, j)),
        ],
        out_specs=pl.BlockSpec((1, m, tn), lambda l, j: (l, 0, j)),
        out_shape=jax.ShapeDtypeStruct((nl, m, n), F32),
        compiler_params=_cp(("parallel", "parallel")),
        name="linear",
    )(x, w, b)


def _mod_spec(mod, tt):
    d = mod.shape[-1]
    if mod.shape[1] == 1:
        return pl.BlockSpec((1, 1, d), lambda b, i: (b, 0, 0))
    return pl.BlockSpec((1, tt, d), lambda b, i: (b, i, 0))


def _modlinear_kernel(x_ref, sc_ref, sh_ref, w_ref, o_ref):
    hin = x_ref[0] * (1.0 + sc_ref[0]) + sh_ref[0]
    o_ref[0] = jnp.dot(hin.astype(BF16), w_ref[...], preferred_element_type=F32)


def _modlinear(x, sc, sh, w_bf, *, tt):
    bsz, t, d = x.shape
    n = w_bf.shape[1]
    return pl.pallas_call(
        _modlinear_kernel,
        grid=(bsz, t // tt),
        in_specs=[
            pl.BlockSpec((1, tt, d), lambda b, i: (b, i, 0)),
            _mod_spec(sc, tt),
            _mod_spec(sh, tt),
            pl.BlockSpec((d, n), lambda b, i: (0, 0)),
        ],
        out_specs=pl.BlockSpec((1, tt, n), lambda b, i: (b, i, 0)),
        out_shape=jax.ShapeDtypeStruct((bsz, t, n), F32),
        compiler_params=_cp(("parallel", "parallel")),
        name="modlinear",
    )(x, sc, sh, w_bf)


def _kvproj_kernel(x_ref, sc_ref, sh_ref, w_ref, wvt_ref, kv_ref, kk_ref, vt_ref, *, n_kv, dh, k_parts):
    hin = (x_ref[0] * (1.0 + sc_ref[0]) + sh_ref[0]).astype(BF16)
    y = jnp.dot(hin, w_ref[...], preferred_element_type=F32)
    kv_ref[0] = y
    gd = n_kv * dh
    for j, part in enumerate(k_parts):
        for g in range(n_kv):
            kk_ref[j, 0, g] = y[:, part * gd + g * dh:part * gd + (g + 1) * dh].astype(BF16)
    yt = lax.dot_general(wvt_ref[...], hin, NT_DIMS, preferred_element_type=F32)
    for j in range(vt_ref.shape[1]):
        vt_ref[0, j, 0] = yt[j * dh:(j + 1) * dh].astype(BF16)


def _kvproj(x, sc, sh, w_bf, wvt_bf, *, n_kv, dh, k_parts, tt):
    bsz, t, d = x.shape
    n = w_bf.shape[1]
    nv = wvt_bf.shape[0] // dh
    return pl.pallas_call(
        functools.partial(_kvproj_kernel, n_kv=n_kv, dh=dh, k_parts=k_parts),
        grid=(bsz, t // tt),
        in_specs=[
            pl.BlockSpec((1, tt, d), lambda b, i: (b, i, 0)),
            _mod_spec(sc, tt),
            _mod_spec(sh, tt),
            pl.BlockSpec((d, n), lambda b, i: (0, 0)),
            pl.BlockSpec((nv * dh, d), lambda b, i: (0, 0)),
        ],
        out_specs=[
            pl.BlockSpec((1, tt, n), lambda b, i: (b, i, 0)),
            pl.BlockSpec((len(k_parts), 1, n_kv, tt, dh), lambda b, i: (0, b, 0, i, 0)),
            pl.BlockSpec((1, nv, 1, dh, tt), lambda b, i: (b, 0, i, 0, 0)),
        ],
        out_shape=[
            jax.ShapeDtypeStruct((bsz, t, n), F32),
            jax.ShapeDtypeStruct((len(k_parts), bsz, n_kv, t, dh), BF16),
            jax.ShapeDtypeStruct((bsz, nv, t // tt, dh, tt), BF16),
        ],
        compiler_params=_cp(("parallel", "parallel")),
        name="kvproj",
    )(x, sc, sh, w_bf, wvt_bf)


def _qproj_kernel(x_ref, sc_ref, sh_ref, wt_ref, qt_ref, gt_ref, *, hd, scale):
    hin = (x_ref[0] * (1.0 + sc_ref[0]) + sh_ref[0]).astype(BF16)
    yt = lax.dot_general(wt_ref[...], hin, NT_DIMS, preferred_element_type=F32)
    qt_ref[0] = (yt[:hd] * scale).astype(BF16)
    gt_ref[0] = jax.nn.sigmoid(yt[hd:])


def _qproj(x, sc, sh, wt_bf, *, hd, scale, tt):
    bsz, t, d = x.shape
    n = wt_bf.shape[0]
    return pl.pallas_call(
        functools.partial(_qproj_kernel, hd=hd, scale=scale),
        grid=(bsz, t // tt),
        in_specs=[
            pl.BlockSpec((1, tt, d), lambda b, i: (b, i, 0)),
            _mod_spec(sc, tt),
            _mod_spec(sh, tt),
            pl.BlockSpec((n, d), lambda b, i: (0, 0)),
        ],
        out_specs=[
            pl.BlockSpec((1, hd, tt), lambda b, i: (b, 0, i)),
            pl.BlockSpec((1, n - hd, tt), lambda b, i: (b, 0, i)),
        ],
        out_shape=[
            jax.ShapeDtypeStruct((bsz, hd, t), BF16),
            jax.ShapeDtypeStruct((bsz, n - hd, t), F32),
        ],
        compiler_params=_cp(("parallel", "parallel")),
        name="qproj",
    )(x, sc, sh, wt_bf)


def _log1p(x):
    u = 1.0 + x
    return jnp.where(u == 1.0, x, jnp.log(u) * x / jnp.where(u == 1.0, 1.0, u - 1.0))


def _expm1(x):
    u = jnp.exp(x)
    safe = (u != 1.0) & (u > 0.0)
    return jnp.where(u == 1.0, x, jnp.where(u > 0.0, (u - 1.0) * x / jnp.where(safe, jnp.log(u), 1.0), -1.0))


def _softplus(z):
    return jnp.maximum(z, 0.0) + _log1p(jnp.exp(-jnp.abs(z)))


def _rglru_gates(xc, wg_ref, bg_ref, lam_ref, n_heads):
    w = xc.shape[1]
    bw = w // n_heads
    xcb = xc.astype(BF16)
    gates = []
    for g in range(2):
        cols = [jnp.dot(xcb[:, i * bw:(i + 1) * bw], wg_ref[g, i], preferred_element_type=F32) for i in range(n_heads)]
        gates.append(jnp.concatenate(cols, axis=1) + bg_ref[g:g + 1, :])
    r = jax.nn.sigmoid(gates[0])
    i_g = jax.nn.sigmoid(gates[1])
    log_a = (-RGLRU_C * _softplus(-lam_ref[...])) * r
    a = jnp.exp(log_a)
    b_in = jnp.sqrt(-_expm1(2.0 * log_a)) * (i_g * xc)
    return a, b_in


def _rglru_seq_kernel(x_ref, sc_ref, sh_ref, conv0_ref, h0_ref, win_ref, wconv_ref, bconv_ref, wg_ref, bg_ref, lam_ref,
                      gated_ref, convout_ref, hout_ref, ext_ref, a_ref, b_ref, yb_ref, h_ref, *, tt, n_heads, cw):
    t = pl.program_id(1)
    w = a_ref.shape[1]
    pad = 8

    @pl.when(t == 0)
    def _():
        ext_ref[0:pad - (cw - 1), :] = jnp.zeros((pad - (cw - 1), w), F32)
        ext_ref[pad - (cw - 1):pad, :] = conv0_ref[0]
        h_ref[...] = h0_ref[0]

    hin = x_ref[0] * (1.0 + sc_ref[0]) + sh_ref[0]
    u = jnp.dot(hin.astype(BF16), win_ref[...], preferred_element_type=F32)
    yb_ref[...] = jax.nn.gelu(u[:, :w])
    ext_ref[pad:pad + tt, :] = u[:, w:]
    xc = bconv_ref[...] + sum(ext_ref[pad - (cw - 1) + k:pad - (cw - 1) + k + tt, :] * wconv_ref[k:k + 1, :] for k in range(cw))
    tail = ext_ref[tt:tt + pad, :]
    ext_ref[0:pad, :] = tail

    a, b_in = _rglru_gates(xc, wg_ref, bg_ref, lam_ref, n_heads)
    a_ref[...] = a
    b_ref[...] = b_in

    row = lax.broadcasted_iota(jnp.int32, (tt, LANES), 0)
    for c in range(w // LANES):
        cs = slice(c * LANES, (c + 1) * LANES)
        av = a_ref[:, cs]
        bv = b_ref[:, cs]
        s = 1
        while s < tt:
            keep = row >= s
            b_sh = jnp.where(keep, pltpu.roll(bv, s, 0), 0.0)
            a_sh = jnp.where(keep, pltpu.roll(av, s, 0), 1.0)
            bv = av * b_sh + bv
            av = av * a_sh
            s *= 2
        hs = av * h_ref[:, cs] + bv
        h_ref[:, cs] = hs[tt - 1:tt, :]
        gated_ref[0, :, cs] = (hs * yb_ref[:, cs]).astype(gated_ref.dtype)

    @pl.when(t == pl.num_programs(1) - 1)
    def _():
        convout_ref[0] = tail[pad - (cw - 1):pad, :]
        hout_ref[0] = h_ref[...]


def _rglru_seq(x, sc, sh, conv0, h0, win_bf, wconv, bconv, wg_bf, bg, lam, *, tt):
    bsz, t, d = x.shape
    cw, w = wconv.shape
    n_heads = wg_bf.shape[1]
    full = lambda shape: pl.BlockSpec(shape, lambda b, i: (0,) * len(shape))
    return pl.pallas_call(
        functools.partial(_rglru_seq_kernel, tt=tt, n_heads=n_heads, cw=cw),
        grid=(bsz, t // tt),
        in_specs=[
            pl.BlockSpec((1, tt, d), lambda b, i: (b, i, 0)),
            _mod_spec(sc, tt),
            _mod_spec(sh, tt),
            pl.BlockSpec((1, cw - 1, w), lambda b, i: (b, 0, 0)),
            pl.BlockSpec((1, 1, w), lambda b, i: (b, 0, 0)),
            full((d, 2 * w)),
            full((cw, w)),
            full((1, w)),
            full(wg_bf.shape),
            full((2, w)),
            full((1, w)),
        ],
        out_specs=[
            pl.BlockSpec((1, tt, w), lambda b, i: (b, i, 0)),
            pl.BlockSpec((1, cw - 1, w), lambda b, i: (b, 0, 0)),
            pl.BlockSpec((1, 1, w), lambda b, i: (b, 0, 0)),
        ],
        out_shape=[
            jax.ShapeDtypeStruct((bsz, t, w), BF16),
            jax.ShapeDtypeStruct((bsz, cw - 1, w), F32),
            jax.ShapeDtypeStruct((bsz, 1, w), F32),
        ],
        scratch_shapes=[
            pltpu.VMEM((tt + 8, w), F32),
            pltpu.VMEM((tt, w), F32),
            pltpu.VMEM((tt, w), F32),
            pltpu.VMEM((tt, w), F32),
            pltpu.VMEM((1, w), F32),
        ],
        compiler_params=_cp(("parallel", "arbitrary")),
        name="rglru_seq",
    )(x, sc, sh, conv0, h0, win_bf, wconv, bconv, wg_bf, bg, lam)


def _rglru_step_kernel(x_ref, sc_ref, sh_ref, conv_ref, h0_ref, win_ref, wconv_ref, bconv_ref, wg_ref, bg_ref, lam_ref,
                       gated_ref, convout_ref, hout_ref, *, n_heads, cw):
    w = h0_ref.shape[1]
    hin = x_ref[...] * (1.0 + sc_ref[...]) + sh_ref[...]
    u = jnp.dot(hin.astype(BF16), win_ref[...], preferred_element_type=F32)
    yb = jax.nn.gelu(u[:, :w])
    xb = u[:, w:]
    taps = [conv_ref[k] for k in range(cw - 1)] + [xb]
    xc = bconv_ref[...] + sum(taps[k] * wconv_ref[k:k + 1, :] for k in range(cw))
    a, b_in = _rglru_gates(xc, wg_ref, bg_ref, lam_ref, n_heads)
    h = a * h0_ref[...] + b_in
    gated_ref[...] = (h * yb).astype(gated_ref.dtype)
    for k in range(cw - 1):
        convout_ref[k] = taps[k + 1]
    hout_ref[...] = h


def _rglru_step(x, sc, sh, conv, h0, win_bf, wconv, bconv, wg_bf, bg, lam):
    r, _ = x.shape
    cw, w = wconv.shape
    n_heads = wg_bf.shape[1]
    return pl.pallas_call(
        functools.partial(_rglru_step_kernel, n_heads=n_heads, cw=cw),
        out_shape=[
            jax.ShapeDtypeStruct((r, w), BF16),
            jax.ShapeDtypeStruct((cw - 1, r, w), F32),
            jax.ShapeDtypeStruct((r, w), F32),
        ],
        compiler_params=pltpu.CompilerParams(vmem_limit_bytes=VMEM_LIMIT),
        name="rglru_step",
    )(x, sc, sh, conv, h0, win_bf, wconv, bconv, wg_bf, bg, lam)


def _layer_norm(v, g, b):
    mu = jnp.mean(v, axis=-1, keepdims=True)
    dv = v - mu
    var = jnp.mean(dv * dv, axis=-1, keepdims=True)
    return dv * lax.rsqrt(var + LN_EPS) * g + b


def _post_kernel(a_ref, x_ref, g1_ref, sc2_ref, sh2_ref, wout_ref, lng_ref, lnb_ref, wr_ref, br_ref,
                 x1_ref, hin2_ref, rt_ref, *, alpha, n_experts):
    y = jnp.dot(a_ref[0].astype(BF16), wout_ref[...], preferred_element_type=F32)
    x1 = _layer_norm(alpha * x_ref[0] + (1.0 + g1_ref[0]) * y, lng_ref[...], lnb_ref[...])
    x1_ref[0] = x1
    hin2 = x1 * (1.0 + sc2_ref[0]) + sh2_ref[0]
    hin2_ref[0] = hin2.astype(hin2_ref.dtype)
    logits = jnp.dot(hin2.astype(BF16), wr_ref[...].astype(BF16), preferred_element_type=F32) + br_ref[...]
    lane = lax.broadcasted_iota(jnp.int32, logits.shape, 1)
    work = jnp.where(lane < n_experts, logits, -jnp.inf)
    vals, idxs = [], []
    for _ in range(TOP_K):
        m = jnp.max(work, axis=-1, keepdims=True)
        idx = jnp.min(jnp.where(work == m, lane, LANES), axis=-1, keepdims=True)
        vals.append(m)
        idxs.append(idx)
        work = jnp.where(lane == idx, -jnp.inf, work)
    es = [jnp.exp(v - vals[0]) for v in vals]
    z = sum(es)
    rt = jnp.zeros(logits.shape, F32)
    for k in range(TOP_K):
        rt = jnp.where(lane == k, es[k] / z, rt)
        rt = jnp.where(lane == TOP_K + k, idxs[k].astype(F32), rt)
    rt_ref[0] = rt


def _post(a, x, g1, sc2, sh2, wout_bf, lng, lnb, wr_pad, br_pad, *, alpha, n_experts, tt):
    bsz, t, d = x.shape
    dk = a.shape[-1]
    full = lambda shape: pl.BlockSpec(shape, lambda b, i: (0,) * len(shape))
    row = pl.BlockSpec((1, tt, d), lambda b, i: (b, i, 0))
    return pl.pallas_call(
        functools.partial(_post_kernel, alpha=alpha, n_experts=n_experts),
        grid=(bsz, t // tt),
        in_specs=[
            pl.BlockSpec((1, tt, dk), lambda b, i: (b, i, 0)),
            row,
            _mod_spec(g1, tt),
            _mod_spec(sc2, tt),
            _mod_spec(sh2, tt),
            full((dk, d)),
            full((1, d)),
            full((1, d)),
            full((d, LANES)),
            full((1, LANES)),
        ],
        out_specs=[row, row, pl.BlockSpec((1, tt, LANES), lambda b, i: (b, i, 0))],
        out_shape=[
            jax.ShapeDtypeStruct((bsz, t, d), F32),
            jax.ShapeDtypeStruct((bsz, t, d), F32),
            jax.ShapeDtypeStruct((bsz, t, LANES), F32),
        ],
        compiler_params=_cp(("parallel", "parallel")),
        name="post",
    )(a, x, g1, sc2, sh2, wout_bf, lng, lnb, wr_pad, br_pad)


def _swiglu(g, u):
    g = jnp.minimum(g, SWIGLU_LIMIT)
    u = jnp.clip(u, -SWIGLU_LIMIT, SWIGLU_LIMIT)
    return g * jax.nn.sigmoid(SWIGLU_ALPHA * g) * (u + 1.0)


def _moe_grouped_kernel(vb_ref, ve_ref, lo_ref, hi_ref, dst_ref, xs_ref, wgu_ref, bgu_ref, wdn_ref, bdn_ref, yg_ref,
                        wgu_bf, wdn_bf, stage_ref, sem, *, chunk, tm, trash0):
    v = pl.program_id(0)
    last = pl.num_programs(0) - 1
    d, de2 = wgu_bf.shape
    de = de2 // 2
    e = ve_ref[v]
    pv = jnp.maximum(v - 1, 0)
    slot = v % 2

    def live(u):
        return hi_ref[u] > lo_ref[u]

    def wait_rows(s):
        pltpu.make_async_copy(stage_ref.at[s], yg_ref.at[pl.ds(0, tm)], sem.at[s]).wait()

    @pl.when((v == 0) | (e != ve_ref[pv]))
    def _():
        for c in range(de2 // chunk):
            wgu_bf[:, c * chunk:(c + 1) * chunk] = wgu_ref[0, :, c * chunk:(c + 1) * chunk].astype(BF16)
        for c in range(de // chunk):
            wdn_bf[c * chunk:(c + 1) * chunk, :] = wdn_ref[0, c * chunk:(c + 1) * chunk, :].astype(BF16)

    @pl.when(v == 0)
    def _():
        stage_ref[...] = jnp.zeros(stage_ref.shape, F32)
        for s in range(2):
            fill = pltpu.make_async_copy(stage_ref.at[s], yg_ref.at[pl.ds(trash0 + s * tm, tm)], sem.at[s])
            fill.start()
            fill.wait()

    @pl.when((v >= 2) & live(jnp.maximum(v - 2, 0)))
    def _():
        wait_rows(slot)

    lo = lo_ref[v]
    hi = hi_ref[v]

    @pl.when(hi > lo)
    def _():
        x = xs_ref[...].astype(BF16)
        acc = jnp.zeros((tm, d), F32) + bdn_ref[0]
        for c in range(de // chunk):
            g = jnp.dot(x, wgu_bf[:, c * chunk:(c + 1) * chunk], preferred_element_type=F32) + bgu_ref[0, :, c * chunk:(c + 1) * chunk]
            u = jnp.dot(x, wgu_bf[:, de + c * chunk:de + (c + 1) * chunk], preferred_element_type=F32) + bgu_ref[0, :, de + c * chunk:de + (c + 1) * chunk]
            h = _swiglu(g, u).astype(BF16)
            acc = acc + jnp.dot(h, wdn_bf[c * chunk:(c + 1) * chunk, :], preferred_element_type=F32)
        for s in range(2):
            @pl.when(slot == s)
            def _(s=s):
                stage_ref[s] = acc

                for i in range(tm):
                    pltpu.make_async_copy(stage_ref.at[s, pl.ds(i, 1)], yg_ref.at[pl.ds(dst_ref[0, 0, i], 1)],
                                          sem.at[s]).start()

    @pl.when(v == last)
    def _():
        @pl.when(hi > lo)
        def _():
            wait_rows(slot)

        @pl.when((v >= 1) & live(pv))
        def _():
            wait_rows(1 - slot)


def _moe_grouped(visits, dst, xs, w_gu, b_gu, w_dn, b_dn, *, tm, trash0, e_off):
    n_rows, d = xs.shape
    n_e, _, de2 = w_gu.shape
    de = de2 // 2
    n_vis = visits[0].shape[0]
    grid_spec = pltpu.PrefetchScalarGridSpec(
        num_scalar_prefetch=4,
        grid=(n_vis,),
        in_specs=[
            pl.BlockSpec((1, 1, tm), lambda v, vb, ve, lo, hi: (v, 0, 0), memory_space=pltpu.SMEM),
            pl.BlockSpec((tm, d), lambda v, vb, ve, lo, hi: (vb[v], 0)),
            pl.BlockSpec((1, d, de2), lambda v, vb, ve, lo, hi: (ve[v] + e_off, 0, 0)),
            pl.BlockSpec((1, 1, de2), lambda v, vb, ve, lo, hi: (ve[v] + e_off, 0, 0)),
            pl.BlockSpec((1, de, d), lambda v, vb, ve, lo, hi: (ve[v] + e_off, 0, 0)),
            pl.BlockSpec((1, 1, d), lambda v, vb, ve, lo, hi: (ve[v] + e_off, 0, 0)),
        ],
        out_specs=pl.BlockSpec(memory_space=pl.ANY),
        scratch_shapes=[
            pltpu.VMEM((d, de2), BF16),
            pltpu.VMEM((de, d), BF16),
            pltpu.VMEM((2, tm, d), F32),
            pltpu.SemaphoreType.DMA((2,)),
        ],
    )
    return pl.pallas_call(
        functools.partial(_moe_grouped_kernel, chunk=min(512, de), tm=tm, trash0=trash0),
        grid_spec=grid_spec,
        out_shape=jax.ShapeDtypeStruct((trash0 + 2 * tm, d), F32),
        compiler_params=_cp(("arbitrary",), 56 * 1024 * 1024),
        name="moe_grouped",
    )(*visits, dst, xs, w_gu, b_gu.reshape(n_e, 1, de2), w_dn, b_dn.reshape(n_e, 1, d))


def _moe_dense_kernel(x_ref, rt_ref, wg_ref, wu_ref, bg_ref, bu_ref, wdn_ref, bdn_ref, o_ref):
    e = pl.program_id(0)
    c = pl.program_id(1)

    @pl.when((e == 0) & (c == 0))
    def _():
        o_ref[...] = jnp.zeros(o_ref.shape, F32)

    rt = rt_ref[...]
    ef = e.astype(F32)
    gate = sum(jnp.where(rt[:, TOP_K + k:TOP_K + k + 1] == ef, rt[:, k:k + 1], 0.0) for k in range(TOP_K))
    x = x_ref[...].astype(BF16)
    g = jnp.dot(x, wg_ref[0].astype(BF16), preferred_element_type=F32) + bg_ref[0]
    u = jnp.dot(x, wu_ref[0].astype(BF16), preferred_element_type=F32) + bu_ref[0]
    h = _swiglu(g, u).astype(BF16)
    y = jnp.dot(h, wdn_ref[0].astype(BF16), preferred_element_type=F32)
    y = y + jnp.where(c == 0, 1.0, 0.0) * bdn_ref[0]
    o_ref[...] += gate * y


def _moe_dense(x_bf, rt, w_gu, b_gu, w_dn, b_dn, *, n_e, e_off, chunk=512):
    r, d = x_bf.shape
    n_all, _, de2 = w_gu.shape
    de = de2 // 2
    nc = de // chunk
    b_gu3 = b_gu.reshape(n_all, 1, de2)
    return pl.pallas_call(
        _moe_dense_kernel,
        grid=(n_e, nc),
        in_specs=[
            pl.BlockSpec((r, d), lambda e, c: (0, 0)),
            pl.BlockSpec((r, LANES), lambda e, c: (0, 0)),
            pl.BlockSpec((1, d, chunk), lambda e, c: (e + e_off, 0, c)),
            pl.BlockSpec((1, d, chunk), lambda e, c: (e + e_off, 0, nc + c)),
            pl.BlockSpec((1, 1, chunk), lambda e, c: (e + e_off, 0, c)),
            pl.BlockSpec((1, 1, chunk), lambda e, c: (e + e_off, 0, nc + c)),
            pl.BlockSpec((1, chunk, d), lambda e, c: (e + e_off, c, 0)),
            pl.BlockSpec((1, 1, d), lambda e, c: (e + e_off, 0, 0)),
        ],
        out_specs=pl.BlockSpec((r, d), lambda e, c: (0, 0)),
        out_shape=jax.ShapeDtypeStruct((r, d), F32),
        compiler_params=_cp(("arbitrary", "arbitrary")),
        name="moe_dense",
    )(x_bf, rt, w_gu, w_gu, b_gu3, b_gu3, w_dn, b_dn.reshape(n_all, 1, d))


def _final_kernel(x_ref, *refs, alpha, combine):
    if combine:
        ff_refs, (rt_ref, g2_ref, lng_ref, lnb_ref, o_ref) = refs[:TOP_K], refs[TOP_K:]
        rt = rt_ref[0]
        ff = sum(rt[:, k:k + 1] * ff_refs[k][...] for k in range(TOP_K))
    else:
        ff_ref, rt_ref, g2_ref, lng_ref, lnb_ref, o_ref = refs
        ff = ff_ref[0]
    o_ref[0] = _layer_norm(alpha * x_ref[0] + (1.0 + g2_ref[0]) * ff, lng_ref[...], lnb_ref[...])


def _final(x1, ff, rt, g2, lng, lnb, *, alpha, combine, tt):
    bsz, t, d = x1.shape
    row = pl.BlockSpec((1, tt, d), lambda b, i: (b, i, 0))
    nblk = bsz * t // tt
    if combine:
        ff_specs = [pl.BlockSpec((tt, d), functools.partial(lambda b, i, k: (k * nblk + b * (t // tt) + i, 0), k=k))
                    for k in range(TOP_K)]
        ffs = [ff] * TOP_K
    else:
        ff_specs, ffs = [row], [ff]
    full = lambda shape: pl.BlockSpec(shape, lambda b, i: (0,) * len(shape))
    return pl.pallas_call(
        functools.partial(_final_kernel, alpha=alpha, combine=combine),
        grid=(bsz, t // tt),
        in_specs=[row, *ff_specs, pl.BlockSpec((1, tt, LANES), lambda b, i: (b, i, 0)), _mod_spec(g2, tt), full((1, d)), full((1, d))],
        out_specs=row,
        out_shape=jax.ShapeDtypeStruct((bsz, t, d), F32),
        compiler_params=_cp(("parallel", "parallel")),
        name="final",
    )(x1, *ffs, rt, g2, lng, lnb)


def _bias_chain(dist, tab_ref, h0, n_heads_tab, group, thr):
    out = [jnp.full(dist.shape, tab_ref[h0 + r], F32) for r in range(group)]
    for b in range(1, len(thr)):
        ind = dist >= thr[b]
        out = [jnp.where(ind, tab_ref[b * n_heads_tab + h0 + r], out[r]) for r in range(group)]
    return out


def _nsa_seq_kernel(tab_ref, qt_ref, gt_ref, kc_ref, vct_ref, ks_ref, vst_ref, kw_ref, vwt_ref, toe_ref, o_ref,
                    sel_ref, flag_ref, ms_ref, accs_ref, mw_ref, accw_ref,
                    *, tq, group, dh, n_heads, thr, cmp_block, n_cmp_real, n_blk):
    g = pl.program_id(1)
    it = pl.program_id(2)
    q0 = it * tq
    h0 = g * group
    n_bkt = len(thr)
    far = thr[-1]
    bpt = tq // SEL_BLOCK
    n_tiles = ks_ref.shape[0] // tq
    assert far <= tq + 1 and WINDOW % tq == 0 and WINDOW >= 2 * tq - 1

    def lanes(parts):
        return jnp.concatenate(parts, axis=1)

    q4t = lanes([qt_ref[0, r * dh:(r + 1) * dh, :] for r in range(group)])
    far_bias = lanes([jnp.full((1, tq), tab_ref[(n_bkt - 1) * n_heads + h0 + r], F32) for r in range(group)])
    qi = lax.broadcasted_iota(jnp.int32, (1, tq), 1)
    ki = lax.broadcasted_iota(jnp.int32, (tq, 1), 0)
    qpos = q0 + qi

    ncp = kc_ref.shape[0]
    cidx = lax.broadcasted_iota(jnp.int32, (ncp, 1), 0)
    dist_c = qpos - (cidx * CMP_STRIDE + (cmp_block - 1))
    madd_c = jnp.where((dist_c >= 0) & (cidx < n_cmp_real), 0.0, NEG_INF)
    bias_c = _bias_chain(dist_c, tab_ref, h0, n_heads, group, thr)
    s_c = jnp.dot(kc_ref[...], q4t, preferred_element_type=F32) + lanes([b + madd_c for b in bias_c])
    m_c = jnp.maximum(jnp.max(s_c, axis=0, keepdims=True), M_INIT)
    e_c = jnp.exp(s_c - m_c)
    z_c = jnp.sum(e_c, axis=0, keepdims=True)
    p_c = e_c / jnp.where(z_c > 0, z_c, 1.0)
    o_c = jnp.dot(vct_ref[...], p_c.astype(BF16), preferred_element_type=F32)

    psum = sum(p_c[:, r * tq:(r + 1) * tq] for r in range(group))
    bj = lax.broadcasted_iota(jnp.int32, (n_blk, ncp), 0) * SEL_BLOCK
    ci = lax.broadcasted_iota(jnp.int32, (n_blk, ncp), 1) * CMP_STRIDE
    overlap_t = jnp.where((ci < bj + SEL_BLOCK) & (ci + cmp_block > bj), 1.0, 0.0)
    imp = jnp.dot(overlap_t, psum, preferred_element_type=F32, precision=HIGHEST)
    blk = lax.broadcasted_iota(jnp.int32, (n_blk, tq), 0)
    cur = qpos // SEL_BLOCK
    forced = (blk == 0) | ((blk >= cur - 1) & (blk <= cur))
    valid = blk * SEL_BLOCK <= qpos
    score = jnp.where(forced, SEL_BIG, jnp.where(valid, imp, -SEL_BIG))
    sel = jnp.full((n_blk, tq), NEG_INF, F32)
    for _ in range(min(N_SELECT, n_blk)):
        mx = jnp.max(score, axis=0, keepdims=True)
        idx = jnp.min(jnp.where(score == mx, blk, n_blk), axis=0, keepdims=True)
        hit = blk == idx
        sel = jnp.where(hit, 0.0, sel)
        score = jnp.where(hit, -jnp.inf, score)
    sel_ref[...] = sel
    for t in range(n_tiles):
        flag_ref[t] = jnp.max(sel[t * bpt:(t + 1) * bpt, :])

    ms_ref[...] = jnp.full(ms_ref.shape, M_INIT, F32)
    mw_ref[...] = jnp.full(mw_ref.shape, M_INIT, F32)
    accs_ref[...] = jnp.zeros(accs_ref.shape, F32)
    accw_ref[...] = jnp.zeros(accw_ref.shape, F32)
    ones = jnp.ones((SUBLANES_BF16, tq), BF16)

    def sel_madd(t):
        b0 = t * bpt
        return jnp.concatenate([jnp.broadcast_to(sel_ref[pl.ds(b0 + j, 1), :], (SEL_BLOCK, tq)) for j in range(bpt)], axis=0)

    def toeplitz(kind):
        return lanes([toe_ref[r, kind] for r in range(group)])

    def attend(k_ref, vt_ref, t, bias, madd, m_ref, acc_ref):
        kt = k_ref[pl.ds(pl.multiple_of(t * tq, tq), tq), :]
        s = jnp.dot(kt, q4t, preferred_element_type=F32) + bias
        if madd is not None:
            s = s + lanes([madd] * group)
        m_old = m_ref[...]
        m_new = jnp.maximum(m_old, jnp.max(s, axis=0, keepdims=True))
        e = jnp.exp(s - m_new).astype(BF16)
        vt = jnp.concatenate([vt_ref[t], ones], axis=0)
        acc_ref[...] = jnp.exp(m_old - m_new) * acc_ref[...] + jnp.dot(vt, e, preferred_element_type=F32)
        m_ref[...] = m_new

    causal = jnp.where(qi >= ki, 0.0, NEG_INF)

    def far_body(t, carry):
        @pl.when(flag_ref[t] > -1.0)
        def _():
            attend(ks_ref, vst_ref, t, far_bias, sel_madd(t), ms_ref, accs_ref)
        return carry

    lax.fori_loop(0, jnp.maximum(it - 1, 0), far_body, 0)

    t_prev = jnp.maximum(it - 1, 0)
    kill_prev = jnp.where(it >= 1, 0.0, NEG_INF)
    attend(ks_ref, vst_ref, t_prev, toeplitz(1), sel_madd(t_prev) + kill_prev, ms_ref, accs_ref)
    attend(ks_ref, vst_ref, it, toeplitz(0), sel_madd(it) + causal, ms_ref, accs_ref)

    for j in range(WINDOW // tq + 1):
        off = WINDOW - j * tq
        t_w = jnp.maximum(it - off // tq, 0)
        madd = None
        if off == 0:
            bias, madd = toeplitz(0), causal
        elif off == tq:
            bias = toeplitz(1)
        else:
            bias = far_bias
            if off + tq - 1 > WINDOW:
                madd = jnp.where(off + qi - ki <= WINDOW, 0.0, NEG_INF)
        if off > 0:
            kill = jnp.where(it >= off // tq, 0.0, NEG_INF)
            madd = kill if madd is None else madd + kill
            if madd.shape != (tq, tq):
                madd = jnp.broadcast_to(madd, (tq, tq))
        attend(kw_ref, vwt_ref, t_w, bias, madd, mw_ref, accw_ref)

    def finish(acc_ref):
        acc = acc_ref[...]
        l = acc[dh:dh + 1, :]
        return acc[:dh, :] / jnp.where(l > 0, l, 1.0)

    def gate(j):
        return lanes([gt_ref[0, r * 3 + j:r * 3 + j + 1, :] for r in range(group)])

    ot = gate(0) * o_c + gate(1) * finish(accs_ref) + gate(2) * finish(accw_ref)
    heads_per_store = LANES // dh
    for r in range(0, group, heads_per_store):
        stacked = jnp.concatenate([ot[:, (r + u) * tq:(r + u + 1) * tq] for u in range(heads_per_store)], axis=0)
        o_ref[0, :, r * dh:(r + heads_per_store) * dh] = stacked.T


def _nsa_seq(tab, qt, gt, kc, vct, kk, vt, toe, *, group, dh, cmp_block, n_cmp_real, tq):
    bsz, hd, s = qt.shape
    n_kv = hd // (group * dh)
    n_heads = n_kv * group
    ncp = kc.shape[2]
    n_tiles = s // tq
    n_blk = -(-s // SEL_BLOCK // 8) * 8
    thr = tuple(_rel_thresholds(tab.shape[0] // n_heads))
    rows = group * tq
    gpad = gt.shape[1] // n_kv
    assert s % tq == 0 and tq % SEL_BLOCK == 0 and dh * (LANES // dh) == LANES and group % (LANES // dh) == 0
    sq = lambda *dims: pl.BlockSpec(tuple(None if d == 0 else d for d in dims[:-1]), dims[-1])
    kern = functools.partial(_nsa_seq_kernel, tq=tq, group=group, dh=dh, n_heads=n_heads, thr=thr, cmp_block=cmp_block,
                             n_cmp_real=n_cmp_real, n_blk=n_blk)
    return pl.pallas_call(
        kern,
        grid=(bsz, n_kv, n_tiles),
        in_specs=[
            pl.BlockSpec(memory_space=pltpu.SMEM),
            pl.BlockSpec((1, group * dh, tq), lambda b, g, i: (b, g, i)),
            pl.BlockSpec((1, gpad, tq), lambda b, g, i: (b, g, i)),
            pl.BlockSpec((None, None, ncp, dh), lambda b, g, i: (b, g, 0, 0)),
            pl.BlockSpec((None, None, dh, ncp), lambda b, g, i: (b, g, 0, 0)),
            pl.BlockSpec((None, None, None, s, dh), lambda b, g, i: (0, b, g, 0, 0)),
            pl.BlockSpec((None, None, n_tiles, dh, tq), lambda b, g, i: (b, g, 0, 0, 0)),
            pl.BlockSpec((None, None, None, s, dh), lambda b, g, i: (1, b, g, 0, 0)),
            pl.BlockSpec((None, None, n_tiles, dh, tq), lambda b, g, i: (b, n_kv + g, 0, 0, 0)),
            pl.BlockSpec((group, 2, tq, tq), lambda b, g, i: (g, 0, 0, 0)),
        ],
        out_specs=pl.BlockSpec((1, tq, group * dh), lambda b, g, i: (b, i, g)),
        out_shape=jax.ShapeDtypeStruct((bsz, s, hd), F32),
        scratch_shapes=[
            pltpu.VMEM((n_blk, tq), F32),
            pltpu.SMEM((n_tiles,), F32),
            pltpu.VMEM((1, rows), F32),
            pltpu.VMEM((dh + SUBLANES_BF16, rows), F32),
            pltpu.VMEM((1, rows), F32),
            pltpu.VMEM((dh + SUBLANES_BF16, rows), F32),
        ],
        compiler_params=_cp(("parallel", "parallel", "arbitrary")),
        name="nsa_seq",
    )(tab, qt, gt, kc, vct, kk, vt, kk, vt, toe)


def _compress(k_head, length, w1, b1, w2, b2, pe):
    bsz, _, n_kv, dh = k_head.shape
    cmp_block = pe.shape[0]
    ratio = cmp_block // CMP_STRIDE
    hidden = w1.shape[-1]
    nc = (length - cmp_block) // CMP_STRIDE + 1
    r = bsz * n_kv
    x = k_head[:, :cmp_block].reshape(bsz, ratio, CMP_STRIDE, n_kv, dh).transpose(1, 0, 3, 2, 4).reshape(ratio, r, CMP_STRIDE * dh)
    w1r = w1.reshape(ratio, CMP_STRIDE * dh, hidden)
    part = [_linear(x[n], w1r, jnp.zeros((ratio, 1, hidden), F32)) for n in range(ratio)]
    const = _linear(pe.reshape(1, -1), w1.reshape(1, cmp_block * dh, hidden), b1.reshape(1, 1, hidden))[0]
    pre = sum(part[u][u:u + nc] for u in range(ratio)) + const
    n_cmp = pre.shape[0]
    out = _linear(pre.reshape(n_cmp * r, hidden), w2[None], b2.reshape(1, 1, dh), act_in='gelu')[0]
    return out.reshape(n_cmp, bsz, n_kv, dh).transpose(1, 0, 2, 3)


def _nsa_dec_kernel(pt_ref, qbd_ref, gt_ref, kct_ref, vcf_ref, bc_ref, bs_ref, bw_ref, bn_ref, win_ref, new_ref, *rest,
                    n_pages, group, n_kv, dh, cmp_block, qpos):
    page_refs, o_ref = rest[:n_pages], rest[n_pages]
    n_heads = group * n_kv
    gd = n_kv * dh
    page = page_refs[0].shape[4]
    qbd = qbd_ref[0]
    new = new_ref[0]

    s_c = jnp.dot(qbd, kct_ref[0], preferred_element_type=F32) + bc_ref[...]
    m_c = jnp.maximum(jnp.max(s_c, axis=-1, keepdims=True), M_INIT)
    e_c = jnp.exp(s_c - m_c)
    z_c = jnp.sum(e_c, axis=-1, keepdims=True)
    p_c = e_c / jnp.where(z_c > 0, z_c, 1.0)
    o_c = jnp.dot(p_c.astype(BF16), vcf_ref[0], preferred_element_type=F32)

    ncp = p_c.shape[1]
    g_of_h = lax.broadcasted_iota(jnp.int32, (n_kv, n_heads), 1) // group
    gsel = jnp.where(g_of_h == lax.broadcasted_iota(jnp.int32, (n_kv, n_heads), 0), 1.0, 0.0)
    psum = jnp.dot(gsel, p_c, preferred_element_type=F32, precision=HIGHEST)
    ci = lax.broadcasted_iota(jnp.int32, (ncp, LANES), 0) * CMP_STRIDE
    bj = lax.broadcasted_iota(jnp.int32, (ncp, LANES), 1) * SEL_BLOCK
    overlap = jnp.where((ci < bj + SEL_BLOCK) & (ci + cmp_block > bj), 1.0, 0.0)
    imp = jnp.dot(psum, overlap, preferred_element_type=F32, precision=HIGHEST)
    n_blk = qpos // SEL_BLOCK + 1
    assert n_blk <= LANES
    cur = qpos // SEL_BLOCK
    blk = lax.broadcasted_iota(jnp.int32, (n_kv, LANES), 1)
    forced = (blk == 0) | ((blk >= cur - 1) & (blk <= cur))
    valid = blk * SEL_BLOCK <= qpos
    score = jnp.where(forced, SEL_BIG, jnp.where(valid, imp, -SEL_BIG))
    score = jnp.where(blk < n_blk, score, -jnp.inf)
    si = lax.broadcasted_iota(jnp.int32, (LANES, LANES), 0)
    li = lax.broadcasted_iota(jnp.int32, (LANES, LANES), 1)
    sel_rows = []
    for g in range(n_kv):
        a = jnp.broadcast_to(score[g:g + 1, :], (LANES, LANES))
        bcol = jnp.sum(jnp.where(si == li, a, 0.0), axis=1, keepdims=True)
        ahead = (a > bcol) | ((a == bcol) & (li < si))
        rank = jnp.sum(jnp.where(ahead, 1.0, 0.0), axis=1, keepdims=True)
        picked = jnp.where(rank < min(N_SELECT, n_blk), 1.0, 0.0)
        sel_rows.append(jnp.sum(jnp.where(si == li, picked, 0.0), axis=0, keepdims=True))
    sel = jnp.concatenate(sel_rows, axis=0)
    h_of_g = lax.broadcasted_iota(jnp.int32, (n_heads, n_kv), 0) // group
    gsel_t = jnp.where(h_of_g == lax.broadcasted_iota(jnp.int32, (n_heads, n_kv), 1), 1.0, 0.0)
    madd_blk = (jnp.dot(gsel_t, sel, preferred_element_type=F32) - 1.0) * -NEG_INF

    def attend(kts, vts, bias, s_new, k_new, v_new):
        s = jnp.concatenate([jnp.dot(qbd, kt, preferred_element_type=F32) for kt in kts], axis=1) + bias
        s_new = s_new + jnp.sum(qbd.astype(F32) * k_new.astype(BF16).astype(F32), axis=-1, keepdims=True)
        m = jnp.maximum(jnp.maximum(jnp.max(s, axis=-1, keepdims=True), s_new), M_INIT)
        e = jnp.exp(s - m)
        e_new = jnp.exp(s_new - m)
        l = jnp.sum(e, axis=-1, keepdims=True) + e_new
        acc = e_new.astype(BF16).astype(F32) * v_new.astype(BF16).astype(F32)
        off = 0
        for vt in vts:
            n = vt.shape[1]
            acc = acc + lax.dot_general(e[:, off:off + n].astype(BF16), vt, NT_DIMS, preferred_element_type=F32)
            off += n
        return acc / jnp.where(l > 0, l, 1.0)

    n_keys = n_pages * page
    ej = lax.broadcasted_iota(jnp.int32, (LANES, n_keys), 0)
    ek = lax.broadcasted_iota(jnp.int32, (LANES, n_keys), 1) // SEL_BLOCK
    expand = jnp.where(ej == ek, 1.0, 0.0).astype(BF16)
    madd = jnp.dot((madd_blk == 0.0).astype(BF16), expand, preferred_element_type=F32)
    madd = (madd - 1.0) * -NEG_INF
    pair = 2 if n_pages % 2 == 0 else 1
    tile = lambda refs, u: jnp.concatenate([r[0, u].reshape(gd, page) for r in refs], axis=1).astype(BF16)
    groups = [page_refs[i:i + pair] for i in range(0, n_pages, pair)]
    o_s = attend([tile(rs, 0) for rs in groups], [tile(rs, 1) for rs in groups], bs_ref[...] + madd,
                 bn_ref[...] + madd_blk[:, cur:cur + 1], new[:, 2 * gd:3 * gd], new[:, 3 * gd:4 * gd])

    n_buf = win_ref.shape[4]
    o_w = attend([win_ref[0, 0].reshape(gd, n_buf).astype(BF16)], [win_ref[0, 1].reshape(gd, n_buf).astype(BF16)],
                 bw_ref[...], bn_ref[...], new[:, 4 * gd:5 * gd], new[:, 5 * gd:6 * gd])

    gt = gt_ref[0]
    o = gt[:, 0:1] * o_c + gt[:, 1:2] * o_s + gt[:, 2:3] * o_w
    hg = lax.broadcasted_iota(jnp.int32, (n_heads, 1), 0) // group
    o_ref[0] = sum(jnp.where(hg == g, o[:, g * dh:(g + 1) * dh], 0.0) for g in range(n_kv))


def _nsa_dec(page_table, qbd, gates, kct, vcf, bias_c, bias_s, bias_w, bias_n, cache_win2, kv_new, cache2,
             *, group, n_kv, dh, cmp_block, qpos):
    bsz, n_heads, gd = qbd.shape
    n_pages = page_table.shape[1]
    page = cache2.shape[4]
    n_buf = cache_win2.shape[4]
    assert page % SEL_BLOCK == 0 and n_pages * page == qpos and cache2.shape[1] == 4
    full = lambda a: pl.BlockSpec(a.shape, lambda b, pt: (0,) * a.ndim)
    per_b = lambda a: pl.BlockSpec((1,) + a.shape[1:], lambda b, pt: (b,) + (0,) * (a.ndim - 1))
    page_specs = [pl.BlockSpec((1, 2, n_kv, dh, page), functools.partial(lambda b, pt, p: (pt[b, p], 1, 0, 0, 0), p=p))
                  for p in range(n_pages)]
    win_spec = pl.BlockSpec((1, 2, n_kv, dh, n_buf), lambda b, pt: (b, 0, 0, 0, 0))
    grid_spec = pltpu.PrefetchScalarGridSpec(
        num_scalar_prefetch=1,
        grid=(bsz,),
        in_specs=[per_b(qbd), per_b(gates), per_b(kct), per_b(vcf), full(bias_c), full(bias_s), full(bias_w), full(bias_n),
                  win_spec, per_b(kv_new), *page_specs],
        out_specs=pl.BlockSpec((1, n_heads, dh), lambda b, pt: (b, 0, 0)),
    )
    kern = functools.partial(_nsa_dec_kernel, n_pages=n_pages, group=group, n_kv=n_kv, dh=dh, cmp_block=cmp_block, qpos=qpos)
    return pl.pallas_call(
        kern,
        grid_spec=grid_spec,
        out_shape=jax.ShapeDtypeStruct((bsz, n_heads, dh), F32),
        compiler_params=_cp(("parallel",)),
        name="nsa_dec",
    )(page_table, qbd, gates, kct, vcf, bias_c, bias_s, bias_w, bias_n, cache_win2, kv_new, *([cache2] * n_pages))


def _static_bias(rel_table, dist, valid):
    thr = np.asarray(_rel_thresholds(rel_table.shape[0]))
    bucket = np.searchsorted(thr, np.maximum(dist, 0), side='right') - 1
    return jnp.where(jnp.asarray(valid)[None, :], rel_table.astype(F32).T[:, bucket], NEG_INF)


def _route(rt, n_experts, tm):
    n_tok = rt.shape[0]
    n_as = n_tok * TOP_K
    assert n_as % tm == 0
    n_blocks = n_as // tm
    flat_e = rt[:, TOP_K:2 * TOP_K].astype(jnp.int32).reshape(-1)
    assert n_experts * n_as < 2 ** 31
    packed = jnp.sort(flat_e * n_as + jnp.arange(n_as, dtype=jnp.int32))
    order = packed % n_as
    se = packed // n_as
    tok_sorted = (order // TOP_K).astype(jnp.int32)
    dst_sorted = ((order % TOP_K) * n_tok + order // TOP_K).astype(jnp.int32).reshape(n_blocks, tm)
    experts = jnp.arange(n_experts)
    counts = jnp.sum(flat_e[:, None] == experts[None, :], axis=0)
    ends = jnp.cumsum(counts).astype(jnp.int32)
    starts = ends - counts.astype(jnp.int32)
    first_blk = starts // tm
    n_vis_e = jnp.where(ends > starts, (ends - 1) // tm - first_blk + 1, 0)
    cum = jnp.cumsum(n_vis_e)
    v = jnp.arange(n_blocks + n_experts - 1)
    ve = jnp.minimum(jnp.sum(cum[None, :] <= v[:, None], axis=1), n_experts - 1)
    live = v < cum[-1]
    vb = first_blk[ve] + v - (cum[ve] - n_vis_e[ve])
    e_last = jnp.max(jnp.where(ends > starts, experts, 0))
    ve = jnp.where(live, ve, e_last).astype(jnp.int32)
    vb = jnp.where(live, vb, n_blocks - 1).astype(jnp.int32)
    lo = jnp.where(live, starts[ve], 0).astype(jnp.int32)
    hi = jnp.where(live, ends[ve], 0).astype(jnp.int32)
    i = jnp.arange(tm)[None, :]
    row = vb[:, None] * tm + i
    trash = n_as + (v % 2)[:, None] * tm + i
    dst = jnp.where((row >= lo[:, None]) & (row < hi[:, None]), dst_sorted[vb], trash).astype(jnp.int32)
    return tok_sorted, dst[:, None, :], (vb, ve, lo, hi)


def _split_mod(mod, bsz, per_row):
    parts = jnp.split(mod, 6, axis=-1)
    if per_row:
        return [p[None] for p in parts]
    return [p[:, None] for p in parts]


def _trunk(x, mods, kv_mod, conv0, h0, make_attend, p, *, per_row):
    depth = p['w_ada'].shape[0]
    n_a = p['w_in_a'].shape[0]
    bsz, t, d = x.shape
    n_experts = p['w_router'].shape[-1]
    alpha = (2 * depth) ** 0.25
    tt = min(256, t)
    conv_new, h_new = [], []
    attend, kv_state = None, None
    for l in range(depth):
        sh1, sc1, g1, sh2, sc2, g2 = _split_mod(mods[l], bsz, per_row)
        if l < n_a:
            win_bf = p['w_in_a'][l].astype(BF16)
            wg_bf = p['w_gate_a'][l].astype(BF16)
            args = (win_bf, p['w_conv'][l], p['b_conv'][l][None], wg_bf, p['b_gate_a'][l], p['lru_lambda'][l][None])
            if per_row:
                gated, cb, hl = _rglru_step(x[0], sc1[0], sh1[0], conv0[l].swapaxes(0, 1), h0[l], *args)
                gated, cb = gated[None], cb.swapaxes(0, 1)
            else:
                gated, cb, hl = _rglru_seq(x, sc1, sh1, conv0[l], h0[l][:, None], *args, tt=tt)
                hl = hl[:, 0]
            conv_new.append(cb)
            h_new.append(hl)
            mix, wout = gated, p['w_out_a'][l]
        else:
            if l == n_a:
                ksh, ksc = jnp.split(kv_mod, 2, axis=-1)
                ksh, ksc = (ksh[None], ksc[None]) if per_row else (ksh[:, None], ksc[:, None])
                attend, kv_state = make_attend(x, ksc, ksh)
            lb = l - n_a
            mix, wout = attend(x, sc1, sh1, p['w_in_b'][lb]), p['w_out_b'][lb]
        wr_pad = jnp.zeros((d, LANES), F32).at[:, :n_experts].set(p['w_router'][l])
        br_pad = jnp.zeros((1, LANES), F32).at[0, :n_experts].set(p['b_router'][l])
        x1, hin2, rt = _post(mix, x, g1, sc2, sh2, wout.astype(BF16), p['ln_g'][l, 0][None], p['ln_b'][l, 0][None],
                             wr_pad, br_pad, alpha=alpha, n_experts=n_experts, tt=tt)
        n_tok = bsz * t
        rt2 = rt.reshape(n_tok, LANES)
        moe_w = tuple(p[k].reshape((-1,) + p[k].shape[2:]) for k in ('w_gu', 'b_gu', 'w_down', 'b_down'))
        if per_row:
            ff = _moe_dense(hin2.reshape(n_tok, d), rt2, *moe_w, n_e=n_experts, e_off=l * n_experts).reshape(bsz, t, d)
            x = _final(x1, ff, rt, g2, p['ln_g'][l, 1][None], p['ln_b'][l, 1][None], alpha=alpha, combine=False, tt=tt)
        else:
            tm = math.gcd(MOE_TILE, n_tok * TOP_K)
            tok_sorted, dst, visits = _route(rt2, n_experts, tm)
            xs = hin2.reshape(n_tok, d)[tok_sorted]
            yg = _moe_grouped(visits, dst, xs, *moe_w, tm=tm, trash0=TOP_K * n_tok, e_off=l * n_experts)
            x = _final(x1, yg, rt, g2, p['ln_g'][l, 1][None], p['ln_b'][l, 1][None], alpha=alpha, combine=True, tt=tt)
    return x, jnp.stack(conv_new), jnp.stack(h_new), kv_state


def _qg_weight_t(w_in, n_kv, group, dh):
    d = w_in.shape[0]
    hd = n_kv * group * dh
    wg = w_in[:, hd:].reshape(d, n_kv, 3 * group)
    wg = jnp.pad(wg, ((0, 0), (0, 0), (0, 16 - 3 * group))).reshape(d, n_kv * 16)
    return jnp.concatenate([w_in[:, :hd], wg], axis=1).T.astype(BF16)


def _toeplitz_bias(rel_table, tq):
    thr = _rel_thresholds(rel_table.shape[0])
    kk = lax.broadcasted_iota(jnp.int32, (2, tq, tq), 1)
    ii = lax.broadcasted_iota(jnp.int32, (2, tq, tq), 2)
    dist = lax.broadcasted_iota(jnp.int32, (2, tq, tq), 0) * tq + ii - kk
    tab = rel_table.astype(F32)[:, :, None, None, None]
    bias = jnp.broadcast_to(tab[0], (rel_table.shape[1], 2, tq, tq))
    for b in range(1, len(thr)):
        bias = jnp.where(dist[None] >= thr[b], tab[b], bias)
    return bias


def kernel(x_prompt, x_sample, c_prompt, c_sample, state_conv, state_h, cache_kv, cache_win, page_table, w_ada, b_ada, ln_g, ln_b, w_in_a, w_conv, b_conv, w_gate_a, b_gate_a, lru_lambda, w_out_a, w_ada_kv, b_ada_kv, w_kv, cmp_w1, cmp_b1, cmp_w2, cmp_b2, cmp_pe, w_in_b, w_out_b, rel_table, w_router, b_router, w_gu, b_gu, w_down, b_down):
    p = dict(w_ada=w_ada, b_ada=b_ada, ln_g=ln_g, ln_b=ln_b, w_in_a=w_in_a, w_conv=w_conv, b_conv=b_conv,
             w_gate_a=w_gate_a, b_gate_a=b_gate_a, lru_lambda=lru_lambda, w_out_a=w_out_a, w_kv=w_kv,
             w_in_b=w_in_b, w_out_b=w_out_b, w_router=w_router, b_router=b_router, w_gu=w_gu, b_gu=b_gu,
             w_down=w_down, b_down=b_down)
    bsz, seq, d = x_prompt.shape
    dec_b = x_sample.shape[0]
    n_a = w_in_a.shape[0]
    cw = w_conv.shape[1]
    lru_w = w_conv.shape[2]
    n_kv, dh = cache_kv.shape[3], cache_kv.shape[4]
    n_heads = rel_table.shape[1]
    group = n_heads // n_kv
    hd = n_heads * dh
    gd = n_kv * dh
    n_parts = w_kv.shape[1] // gd
    assert n_parts == 6 and 3 * group <= 16
    cmp_block = cmp_pe.shape[1]
    scale = dh ** -0.5
    assert math.log2(scale) == round(math.log2(scale))

    c_all = jnp.concatenate([c_prompt, c_sample], axis=0)
    mods = _linear(c_all, w_ada, b_ada[:, None, :], act_in='silu')
    kv_mod = _linear(c_all, w_ada_kv[None], b_ada_kv[None, None, :], act_in='silu')[0]
    tab = rel_table.astype(F32).reshape(-1)

    def cmp_kv(k_head, v_head, length):
        kc = _compress(k_head, length, cmp_w1[0], cmp_b1[0], cmp_w2[0], cmp_b2[0], cmp_pe[0])
        vc = _compress(v_head, length, cmp_w1[1], cmp_b1[1], cmp_w2[1], cmp_b2[1], cmp_pe[1])
        return kc, vc

    def make_prompt_attend(x, ksc, ksh):
        tq = min(NSA_TILE, seq)
        wvt = jnp.concatenate([w_kv[:, 3 * gd:4 * gd], w_kv[:, 5 * gd:6 * gd]], axis=1).T.astype(BF16)
        kv, kk, vt = _kvproj(x, ksc, ksh, w_kv.astype(BF16), wvt, n_kv=n_kv, dh=dh, k_parts=(2, 4), tt=tq)
        kv6 = kv.reshape(bsz, seq, n_parts, n_kv, dh)
        kc, vc = cmp_kv(kv6[:, :cmp_block, 0], kv6[:, :cmp_block, 1], seq)
        n_cmp = kc.shape[1]
        ncp = -(-n_cmp // SUBLANES_BF16) * SUBLANES_BF16
        padc = lambda a: jnp.pad(a, ((0, 0), (0, ncp - n_cmp), (0, 0), (0, 0))).astype(BF16)
        kc_g = padc(kc).transpose(0, 2, 1, 3)
        vc_t = padc(vc).transpose(0, 2, 3, 1)
        toe = _toeplitz_bias(rel_table, tq)

        def attend(x, sc1, sh1, w_in):
            qt, gt = _qproj(x, sc1, sh1, _qg_weight_t(w_in, n_kv, group, dh), hd=hd, scale=scale, tt=tq)
            return _nsa_seq(tab, qt, gt, kc_g, vc_t, kk, vt, toe, group=group, dh=dh, cmp_block=cmp_block,
                            n_cmp_real=n_cmp, tq=tq)

        n_win = min(WINDOW, seq)
        return attend, (kv6[:, :, :4], kv6[:, seq - n_win:, 4:])

    def make_sample_attend(x, ksc, ksh):
        kv = _modlinear(x, ksc, ksh, w_kv.astype(BF16), tt=dec_b)
        kv6 = kv.reshape(dec_b, 1, n_parts, n_kv, dh)
        page = cache_kv.shape[1]
        past_len = page_table.shape[1] * page
        n_buf = cache_win.shape[1]
        assert page >= cmp_block and past_len >= cmp_block
        cache2 = cache_kv.transpose(0, 2, 3, 4, 1)
        cache_win2 = cache_win.transpose(0, 2, 3, 4, 1)
        head = cache2[page_table[:, 0], :2, :, :, :cmp_block].transpose(0, 1, 4, 2, 3)
        kc, vc = cmp_kv(head[:, 0], head[:, 1], past_len + 1)
        n_cmp = kc.shape[1]
        ncp = -(-n_cmp // SUBLANES_BF16) * SUBLANES_BF16
        padc = lambda a: jnp.pad(a, ((0, 0), (0, ncp - n_cmp), (0, 0), (0, 0))).astype(BF16).reshape(dec_b, ncp, gd)
        kct, vcf = padc(kc).transpose(0, 2, 1), padc(vc)
        c = np.arange(ncp)
        dist_c = past_len - (c * CMP_STRIDE + cmp_block - 1)
        bias_c = _static_bias(rel_table, dist_c, (dist_c >= 0) & (c < n_cmp))
        bias_s = _static_bias(rel_table, past_len - np.arange(past_len), np.ones(past_len, bool))
        dist_w = n_buf - np.arange(n_buf)
        bias_w = _static_bias(rel_table, dist_w, (dist_w <= WINDOW) & (past_len - dist_w >= 0))
        bias_n = rel_table.astype(F32)[0][:, None]
        win =jnp.concatenate([cache_win, kv6[:, :, 4:]], axis=1)
        own_group = (jnp.arange(n_heads)[:, None] // group == jnp.arange(n_kv)[None, :]).astype(BF16)

        def attend(x, sc1, sh1, w_in):
            qt, gt = _qproj(x, sc1, sh1, _qg_weight_t(w_in, n_kv, group, dh), hd=hd, scale=scale, tt=dec_b)
            q = qt[0].T.reshape(dec_b, n_heads, dh)
            qbd = (q[:, :, None, :] * own_group[None, :, :, None]).reshape(dec_b, n_heads, gd)
            gates = gt[0].reshape(n_kv, 16, dec_b)[:, :3 * group].transpose(2, 0, 1).reshape(dec_b, n_heads, 3)
            gates = jnp.pad(gates, ((0, 0), (0, 0), (0, LANES - 3)))
            o = _nsa_dec(page_table, qbd, gates, kct, vcf, bias_c, bias_s, bias_w, bias_n, cache_win2,
                         kv.reshape(dec_b, 1, n_parts * gd), cache2, group=group, n_kv=n_kv, dh=dh, cmp_block=cmp_block, qpos=past_len)
            return o.reshape(1, dec_b, hd)

        return attend, (kv6[:, :, :4], win[:, -n_buf:])

    conv0 = jnp.zeros((n_a, bsz, cw - 1, lru_w), F32)
    h0 = jnp.zeros((n_a, bsz, lru_w), F32)
    y_prompt, prompt_conv, prompt_h, (prompt_kv, prompt_win) = _trunk(
        x_prompt, mods[:, :bsz], kv_mod[:bsz], conv0, h0, make_prompt_attend, p, per_row=False)
    y_sample, sample_conv, sample_h, (sample_kv, sample_win) = _trunk(
        x_sample.reshape(1, dec_b, d), mods[:, bsz:], kv_mod[bsz:], state_conv, state_h, make_sample_attend, p, per_row=True)
    return (y_prompt, y_sample.reshape(dec_b, 1, d), prompt_conv, prompt_h, prompt_kv, prompt_win,
            sample_conv, sample_h, sample_kv, sample_win)
```
